```python
import jax, jax.numpy as jnp
from jax import lax
import numpy as np

D_MODEL = 2048
BATCH = 1
SEQ = 16384
DEPTH = 2

SSM_EXPAND = 2
D_INNER = SSM_EXPAND * D_MODEL
SSM_HEAD_DIM = 64
SSM_HEADS = D_INNER // SSM_HEAD_DIM
SSM_GROUPS = 8
SSM_STATE = 128
CONV_WIDTH = 4
CHUNK = 256
GN = SSM_GROUPS * SSM_STATE
CONV_DIM = D_INNER + 2 * GN
IN_PROJ_DIM = D_INNER + CONV_DIM + SSM_HEADS

N_Q_HEADS = 32
N_KV_HEADS = 4
HEAD_DIM = 64
WINDOW = 128
BLOCK = 128
KV_DIM = N_KV_HEADS * HEAD_DIM

D_FF = -(-8 * D_MODEL // (3 * 256)) * 256

N_A_LAYERS = DEPTH // 2
N_B_LAYERS = DEPTH - N_A_LAYERS
EPS = 1e-6

kernel_name = "yoco_ssd_swa_sink_alibi_sandwich"


def rmsnorm(x, w):
    xf = x.astype(jnp.float32)
    xf = xf * lax.rsqrt(jnp.mean(xf * xf, axis=-1, keepdims=True) + EPS)
    return (xf * w.astype(jnp.float32)).astype(x.dtype)


def causal_depthwise_conv(u, w, b):
    out = lax.conv_general_dilated(
        u, w[:, None, :].astype(u.dtype), window_strides=(1,),
        padding=[(CONV_WIDTH - 1, 0)],
        dimension_numbers=("NWC", "WIO", "NWC"),
        feature_group_count=u.shape[-1])
    return out + b.astype(u.dtype)


def ssd_chunked_scan(x, dt, A, Bm, Cm):
    b, L = x.shape[0], x.shape[1]
    R = SSM_HEADS // SSM_GROUPS
    pad = (-L) % CHUNK
    n_chunks = (L + pad) // CHUNK

    def to_chunks(t):
        t = jnp.pad(t.astype(jnp.float32), [(0, 0), (0, pad)] + [(0, 0)] * (t.ndim - 2))
        t = t.reshape((b, n_chunks, CHUNK) + t.shape[2:])
        return jnp.moveaxis(t, 1, 0)

    xc = to_chunks(x.reshape(b, L, SSM_GROUPS, R, SSM_HEAD_DIM))
    dtc = to_chunks(dt.reshape(b, L, SSM_GROUPS, R))
    Bc = to_chunks(Bm)
    Cc = to_chunks(Cm)
    A_gr = A.astype(jnp.float32).reshape(SSM_GROUPS, R)
    causal = jnp.tril(jnp.ones((CHUNK, CHUNK), dtype=bool))

    def step(state, inp):
        xq, dtq, Bq, Cq = inp
        cum = jnp.moveaxis(jnp.cumsum(dtq * A_gr, axis=1), 1, -1)
        seg = cum[..., :, None] - cum[..., None, :]
        decay = jnp.where(causal, jnp.exp(jnp.where(causal, seg, 0.0)), 0.0)
        xdt = xq * dtq[..., None]
        cb = jnp.einsum("bign,bjgn->bgij", Cq, Bq)
        y_diag = jnp.einsum("bgij,bgrij,bjgrp->bigrp", cb, decay, xdt)
        y_off = jnp.einsum("bign,bgrpn,bgri->bigrp", Cq, state, jnp.exp(cum))
        decay_to_end = jnp.exp(cum[..., -1:] - cum)
        new_state = (state * jnp.exp(cum[..., -1])[..., None, None]
                     + jnp.einsum("bjgn,bgrj,bjgrp->bgrpn", Bq, decay_to_end, xdt))
        return new_state, y_diag + y_off

    state0 = jnp.zeros((b, SSM_GROUPS, R, SSM_HEAD_DIM, SSM_STATE), jnp.float32)
    _, y = lax.scan(step, state0, (xc, dtc, Bc, Cc))
    y = jnp.moveaxis(y, 0, 1).reshape(b, n_chunks * CHUNK, SSM_HEADS, SSM_HEAD_DIM)
    return y[:, :L]


def mamba2_mixer(u, w_in, conv_w, conv_b, dt_bias, A_log, D_skip, norm_w, w_out):
    b, L, _ = u.shape
    proj = u @ w_in
    z, xBC, dt_raw = jnp.split(proj, [D_INNER, D_INNER + CONV_DIM], axis=-1)
    xBC = jax.nn.silu(causal_depthwise_conv(xBC, conv_w, conv_b))
    xs, Bm, Cm = jnp.split(xBC, [D_INNER, D_INNER + GN], axis=-1)
    dt = jax.nn.softplus(dt_raw.astype(jnp.float32) + dt_bias.astype(jnp.float32))
    A = -jnp.exp(A_log.astype(jnp.float32))
    xh = xs.reshape(b, L, SSM_HEADS, SSM_HEAD_DIM)
    y = ssd_chunked_scan(xh, dt, A,
                         Bm.reshape(b, L, SSM_GROUPS, SSM_STATE),
                         Cm.reshape(b, L, SSM_GROUPS, SSM_STATE))
    y = y + D_skip.astype(jnp.float32)[:, None] * xh.astype(jnp.float32)
    y = y.reshape(b, L, D_INNER) * jax.nn.silu(z.astype(jnp.float32))
    yg = y.reshape(b, L, SSM_GROUPS, D_INNER // SSM_GROUPS)
    yg = yg * lax.rsqrt(jnp.mean(yg * yg, axis=-1, keepdims=True) + EPS)
    y = (yg.reshape(b, L, D_INNER) * norm_w.astype(jnp.float32)).astype(u.dtype)
    return y @ w_out


def shared_kv(h, kv_norm_w, w_kv):
    b, L, _ = h.shape
    kv = rmsnorm(h, kv_norm_w) @ w_kv
    k, v = jnp.split(kv, 2, axis=-1)
    return (k.reshape(b, L, N_KV_HEADS, HEAD_DIM), v.reshape(b, L, N_KV_HEADS, HEAD_DIM))


def alibi_slopes():
    h = jnp.arange(1, N_Q_HEADS + 1, dtype=jnp.float32)
    return jnp.exp2(-8.0 * h / N_Q_HEADS)


def sliding_window_sink_attention(u, w_q, k, v, sinks, w_o):
    b, L, _ = u.shape
    R = N_Q_HEADS // N_KV_HEADS
    pad = (-L) % BLOCK
    nb = (L + pad) // BLOCK

    def to_blocks(t):
        t = jnp.pad(t, [(0, 0), (0, pad)] + [(0, 0)] * (t.ndim - 2))
        return t.reshape((b, nb, BLOCK) + t.shape[2:])

    def with_prev(t):
        prev = jnp.concatenate([jnp.zeros_like(t[:, :1]), t[:, :-1]], axis=1)
        return jnp.concatenate([prev, t], axis=2)

    q = (u @ w_q).reshape(b, L, N_KV_HEADS, R, HEAD_DIM)
    qb = to_blocks(q).astype(jnp.float32)
    kw = with_prev(to_blocks(k)).astype(jnp.float32)
    vw = with_prev(to_blocks(v)).astype(jnp.float32)
    scores = jnp.einsum("bnqkrd,bnskd->bnkrqs", qb, kw) * (HEAD_DIM ** -0.5)

    qi = jnp.arange(BLOCK)[:, None]
    sj = jnp.arange(2 * BLOCK)[None, :]
    dist = (BLOCK + qi - sj).astype(jnp.float32)
    key_pos = (jnp.arange(nb)[:, None, None] - 1) * BLOCK + sj[None]
    valid = (dist >= 0) & (dist < WINDOW) & (key_pos >= 0)
    slopes = alibi_slopes().reshape(N_KV_HEADS, R, 1, 1)
    logits = scores - slopes * dist
    logits = jnp.where(valid[None, :, None, None], logits, -jnp.inf)

    sink = sinks.astype(jnp.float32).reshape(1, 1, N_KV_HEADS, R, 1, 1)
    m = jnp.maximum(jnp.max(logits, axis=-1, keepdims=True), sink)
    p = jnp.exp(logits - m)
    denom = jnp.sum(p, axis=-1, keepdims=True) + jnp.exp(sink - m)
    probs = p / denom
    out = jnp.einsum("bnkrqs,bnskd->bnqkrd", probs, vw)
    out = out.reshape(b, nb * BLOCK, N_Q_HEADS * HEAD_DIM)[:, :L].astype(u.dtype)
    return out @ w_o


def swiglu(u, w_gate, w_up, w_down):
    return (jax.nn.silu(u @ w_gate) * (u @ w_up)) @ w_down


def setup_inputs(seed: int = 0) -> dict:
    key = jax.random.key(seed)
    ks = jax.random.split(key, 20)
    f32 = jnp.float32
    nA, nB = N_A_LAYERS, N_B_LAYERS

    def normal(k, shape, scale):
        return jax.random.normal(k, shape, f32) * scale

    x = jax.random.normal(ks[0], (BATCH, SEQ, D_MODEL), f32)
    norm_w = 1.0 + normal(ks[1], (DEPTH, 4, D_MODEL), 0.02)
    ssm_w_in = normal(ks[2], (nA, D_MODEL, IN_PROJ_DIM), D_MODEL ** -0.5)
    ssm_conv_w = normal(ks[3], (nA, CONV_WIDTH, CONV_DIM), CONV_WIDTH ** -0.5)
    ssm_conv_b = normal(ks[4], (nA, CONV_DIM), 0.01)
    dt0 = jnp.exp(jax.random.uniform(ks[5], (nA, SSM_HEADS), f32, np.log(1e-3), np.log(1e-1)))
    ssm_dt_bias = dt0 + jnp.log(-jnp.expm1(-dt0))
    ssm_A_log = jnp.log(jax.random.uniform(ks[6], (nA, SSM_HEADS), f32, 1.0, 16.0))
    ssm_D = 1.0 + normal(ks[7], (nA, SSM_HEADS), 0.02)
    ssm_norm_w = 1.0 + normal(ks[8], (nA, D_INNER), 0.02)
    ssm_w_out = normal(ks[9], (nA, D_INNER, D_MODEL), D_INNER ** -0.5)
    kv_norm_w = 1.0 + normal(ks[10], (D_MODEL,), 0.02)
    w_kv = normal(ks[11], (D_MODEL, 2 * KV_DIM), D_MODEL ** -0.5)
    attn_w_q = normal(ks[12], (nB, D_MODEL, N_Q_HEADS * HEAD_DIM), D_MODEL ** -0.5)
    attn_sinks = normal(ks[13], (nB, N_Q_HEADS), 0.5)
    attn_w_o = normal(ks[14], (nB, N_Q_HEADS * HEAD_DIM, D_MODEL), (N_Q_HEADS * HEAD_DIM) ** -0.5)
    ffn_w_gate = normal(ks[15], (DEPTH, D_MODEL, D_FF), D_MODEL ** -0.5)
    ffn_w_up = normal(ks[16], (DEPTH, D_MODEL, D_FF), D_MODEL ** -0.5)
    ffn_w_down = normal(ks[17], (DEPTH, D_FF, D_MODEL), D_FF ** -0.5)
    return {"x": x, "norm_w": norm_w,
            "ssm_w_in": ssm_w_in, "ssm_conv_w": ssm_conv_w, "ssm_conv_b": ssm_conv_b,
            "ssm_dt_bias": ssm_dt_bias, "ssm_A_log": ssm_A_log, "ssm_D": ssm_D,
            "ssm_norm_w": ssm_norm_w, "ssm_w_out": ssm_w_out,
            "kv_norm_w": kv_norm_w, "w_kv": w_kv,
            "attn_w_q": attn_w_q, "attn_sinks": attn_sinks, "attn_w_o": attn_w_o,
            "ffn_w_gate": ffn_w_gate, "ffn_w_up": ffn_w_up, "ffn_w_down": ffn_w_down}


def reference(x, norm_w, ssm_w_in, ssm_conv_w, ssm_conv_b, ssm_dt_bias, ssm_A_log, ssm_D,
              ssm_norm_w, ssm_w_out, kv_norm_w, w_kv, attn_w_q, attn_sinks, attn_w_o,
              ffn_w_gate, ffn_w_up, ffn_w_down):
    h = x
    k_shared, v_shared = None, None
    for layer in range(DEPTH):
        g = norm_w[layer]
        if layer < N_A_LAYERS:
            i = layer
            mix = mamba2_mixer(rmsnorm(h, g[0]), ssm_w_in[i], ssm_conv_w[i], ssm_conv_b[i],
                               ssm_dt_bias[i], ssm_A_log[i], ssm_D[i], ssm_norm_w[i], ssm_w_out[i])
        else:
            if layer == N_A_LAYERS:
                k_shared, v_shared = shared_kv(h, kv_norm_w, w_kv)
            j = layer - N_A_LAYERS
            mix = sliding_window_sink_attention(rmsnorm(h, g[0]), attn_w_q[j], k_shared, v_shared,
                                                attn_sinks[j], attn_w_o[j])
        h = h + rmsnorm(mix, g[1])
        ffn = swiglu(rmsnorm(h, g[2]), ffn_w_gate[layer], ffn_w_up[layer], ffn_w_down[layer])
        h = h + rmsnorm(ffn, g[3])
    return h
```

```python
import functools

import jax
import jax.numpy as jnp
from jax import lax
from jax.experimental import pallas as pl
from jax.experimental.pallas import tpu as pltpu

D_MODEL = 2048
D_INNER = 4096
SSM_HEAD_DIM = 64
SSM_HEADS = 64
SSM_GROUPS = 8
HEADS_PER_GROUP = SSM_HEADS // SSM_GROUPS
SSM_STATE = 128
CONV_WIDTH = 4
CHUNK = 256
GN = SSM_GROUPS * SSM_STATE
GROUP_INNER = D_INNER // SSM_GROUPS
GROUP_CONV = GROUP_INNER + 2 * SSM_STATE
N_Q_HEADS = 32
N_KV_HEADS = 4
Q_PER_KV = N_Q_HEADS // N_KV_HEADS
HEAD_DIM = 64
WINDOW = 128
BLOCK = 128
KV_DIM = N_KV_HEADS * HEAD_DIM
D_FF = 5632
EPS = 1e-6

LANES = 128
SUBLANES = 8
VMEM_LIMIT_BYTES = 56 * 1024 * 1024

BF16 = jnp.bfloat16
F32 = jnp.float32


def _rms_scale(x):
    return lax.rsqrt(jnp.mean(x * x, axis=-1, keepdims=True) + EPS)


def _silu(x):
    return x * jax.nn.sigmoid(x)


def _params(*semantics):
    return pltpu.CompilerParams(dimension_semantics=semantics, vmem_limit_bytes=VMEM_LIMIT_BYTES)


def _norm_matmul_kernel(x_ref, g_ref, w_ref, o_ref, xn_ref):
    @pl.when(pl.program_id(1) == 0)
    def _():
        x = x_ref[...]
        xn_ref[...] = (x * _rms_scale(x) * g_ref[...]).astype(xn_ref.dtype)

    o_ref[...] = jnp.dot(xn_ref[...], w_ref[...], preferred_element_type=F32).astype(o_ref.dtype)


def _norm_matmul2_kernel(x_ref, g_ref, w_ref, w2_ref, o_ref, o2_ref, xn_ref):
    @pl.when(pl.program_id(1) == 0)
    def _():
        x = x_ref[...]
        xn = (x * _rms_scale(x) * g_ref[...]).astype(xn_ref.dtype)
        xn_ref[...] = xn
        o2_ref[...] = jnp.dot(xn, w2_ref[...], preferred_element_type=F32)

    o_ref[...] = jnp.dot(xn_ref[...], w_ref[...], preferred_element_type=F32).astype(o_ref.dtype)


def norm_matmul(x, g, w, *, w2=None, tm=1024, tn=1024, name):
    m, k = x.shape
    n = w.shape[1]
    tm, tn = min(tm, m), min(tn, n)
    grid = (m // tm, n // tn)
    x_spec = pl.BlockSpec((tm, k), lambda i, j: (i, 0))
    g_spec = pl.BlockSpec((1, k), lambda i, j: (0, 0))
    w_spec = pl.BlockSpec((k, tn), lambda i, j: (0, j))
    o_spec = pl.BlockSpec((tm, tn), lambda i, j: (i, j))
    scratch = [pltpu.VMEM((tm, k), BF16)]
    if w2 is None:
        return pl.pallas_call(
            _norm_matmul_kernel, grid=grid, in_specs=[x_spec, g_spec, w_spec], out_specs=o_spec,
            out_shape=jax.ShapeDtypeStruct((m, n), BF16), scratch_shapes=scratch,
            compiler_params=_params("parallel", "arbitrary"), name=name)(x, g, w)
    n2 = w2.shape[1]
    return pl.pallas_call(
        _norm_matmul2_kernel, grid=grid,
        in_specs=[x_spec, g_spec, w_spec, pl.BlockSpec((k, n2), lambda i, j: (0, 0))],
        out_specs=[o_spec, pl.BlockSpec((tm, n2), lambda i, j: (i, 0))],
        out_shape=[jax.ShapeDtypeStruct((m, n), BF16), jax.ShapeDtypeStruct((m, n2), F32)],
        scratch_shapes=scratch, compiler_params=_params("parallel", "arbitrary"), name=name)(x, g, w, w2)


def _matmul_norm_residual_kernel(y_ref, w_ref, h_ref, g_ref, o_ref, acc_ref):
    k = pl.program_id(1)
    d = jnp.dot(y_ref[...], w_ref[...], preferred_element_type=F32)

    @pl.when(k == 0)
    def _():
        acc_ref[...] = d

    @pl.when(k > 0)
    def _():
        acc_ref[...] += d

    @pl.when(k == pl.num_programs(1) - 1)
    def _():
        a = acc_ref[...]
        o_ref[...] = h_ref[...] + a * _rms_scale(a) * g_ref[...]


def matmul_norm_residual(y, w, h, g, *, tm=512, tk=1024, name):
    m, k = y.shape
    n = w.shape[1]
    tm, tk = min(tm, m), min(tk, k)
    return pl.pallas_call(
        _matmul_norm_residual_kernel, grid=(m // tm, k // tk),
        in_specs=[pl.BlockSpec((tm, tk), lambda i, kk: (i, kk)),
                  pl.BlockSpec((tk, n), lambda i, kk: (kk, 0)),
                  pl.BlockSpec((tm, n), lambda i, kk: (i, 0)),
                  pl.BlockSpec((1, n), lambda i, kk: (0, 0))],
        out_specs=pl.BlockSpec((tm, n), lambda i, kk: (i, 0)),
        out_shape=jax.ShapeDtypeStruct((m, n), F32),
        scratch_shapes=[pltpu.VMEM((tm, n), F32)],
        compiler_params=_params("parallel", "arbitrary"), name=name)(y, w, h, g)


def _ffn_kernel(h_ref, gpre_ref, wg_ref, wu_ref, wd_ref, gpost_ref, o_ref, xn_ref, acc_ref):
    f = pl.program_id(1)

    @pl.when(f == 0)
    def _():
        x = h_ref[...]
        xn_ref[...] = (x * _rms_scale(x) * gpre_ref[...]).astype(xn_ref.dtype)

    xn = xn_ref[...]
    gate = jnp.dot(xn, wg_ref[...], preferred_element_type=F32)
    up = jnp.dot(xn, wu_ref[...], preferred_element_type=F32)
    act = (_silu(gate) * up).astype(BF16)
    d = jnp.dot(act, wd_ref[...], preferred_element_type=F32)

    @pl.when(f == 0)
    def _():
        acc_ref[...] = d

    @pl.when(f > 0)
    def _():
        acc_ref[...] += d

    @pl.when(f == pl.num_programs(1) - 1)
    def _():
        a = acc_ref[...]
        o_ref[...] = h_ref[...] + a * _rms_scale(a) * gpost_ref[...]


def ffn(h, g_pre, w_gate, w_up, w_down, g_post, *, tm=512, tf=512, name):
    m, d = h.shape
    dff = w_gate.shape[1]
    tm = min(tm, m)
    return pl.pallas_call(
        _ffn_kernel, grid=(m // tm, dff // tf),
        in_specs=[pl.BlockSpec((tm, d), lambda i, f: (i, 0)),
                  pl.BlockSpec((1, d), lambda i, f: (0, 0)),
                  pl.BlockSpec((d, tf), lambda i, f: (0, f)),
                  pl.BlockSpec((d, tf), lambda i, f: (0, f)),
                  pl.BlockSpec((tf, d), lambda i, f: (f, 0)),
                  pl.BlockSpec((1, d), lambda i, f: (0, 0))],
        out_specs=pl.BlockSpec((tm, d), lambda i, f: (i, 0)),
        out_shape=jax.ShapeDtypeStruct((m, d), F32),
        scratch_shapes=[pltpu.VMEM((tm, d), BF16), pltpu.VMEM((tm, d), F32)],
        compiler_params=_params("parallel", "arbitrary"), name=name)(h, g_pre, w_gate, w_up, w_down, g_post)


def _expand_heads(v, rows):
    return jnp.concatenate(
        [jnp.broadcast_to(v[:, r:r + 1], (rows, SSM_HEAD_DIM)) for r in range(HEADS_PER_GROUP)], axis=1)


def _ssd_kernel(z_ref, xs_ref, b_ref, c_ref, dt_ref, cw_ref, cb_ref, dtb_ref, alog_ref, dskip_ref, nw_ref,
                y_ref, state_ref, tail_ref, pad_ref, dts_ref, cums_ref, dtT_ref, cumT_ref):
    c = pl.program_id(0)
    g = pl.program_id(1)
    q = CHUNK
    row = lax.broadcasted_iota(jnp.int32, (q, q), 0)
    col = lax.broadcasted_iota(jnp.int32, (q, q), 1)
    causal = row >= col

    @pl.when(c == 0)
    def _():
        state_ref[g] = jnp.zeros((SSM_STATE, GROUP_INNER), F32)
        tail_ref[g] = jnp.zeros((SUBLANES, GROUP_CONV), F32)

    @pl.when(g == 0)
    def _():
        dtv = jax.nn.softplus(dt_ref[...] + dtb_ref[...])
        dta = dtv * (-jnp.exp(alog_ref[...]))
        cum = jnp.dot(causal.astype(F32), dta, precision=lax.Precision.HIGHEST,
                      preferred_element_type=F32)
        dts_ref[...] = dtv
        cums_ref[...] = cum
        dtT_ref[...] = dtv.T
        cumT_ref[...] = cum.T

    pad_ref[0:SUBLANES, :] = tail_ref[g]
    pad_ref[SUBLANES:SUBLANES + q, 0:GROUP_INNER] = xs_ref[...].astype(F32)
    pad_ref[SUBLANES:SUBLANES + q, GROUP_INNER:GROUP_INNER + SSM_STATE] = b_ref[...].astype(F32)
    pad_ref[SUBLANES:SUBLANES + q, GROUP_INNER + SSM_STATE:GROUP_CONV] = c_ref[...].astype(F32)
    tail_ref[g] = pad_ref[q:q + SUBLANES, :]
    cw = cw_ref[...]
    acc = cb_ref[...]
    for k in range(CONV_WIDTH):
        lo = SUBLANES - (CONV_WIDTH - 1) + k
        acc = acc + cw[k:k + 1, :] * pad_ref[lo:lo + q, :]
    act = _silu(acc)
    xg = act[:, 0:GROUP_INNER]
    xb = xg.astype(BF16)
    bb = act[:, GROUP_INNER:GROUP_INNER + SSM_STATE].astype(BF16)
    cb_ = act[:, GROUP_INNER + SSM_STATE:GROUP_CONV].astype(BF16)

    shift = lax.rem(LANES - HEADS_PER_GROUP * g, LANES)
    dcol = pltpu.roll(dts_ref[...], shift, 1)
    ccol = pltpu.roll(cums_ref[...], shift, 1)
    r0 = pl.multiple_of(g * HEADS_PER_GROUP, HEADS_PER_GROUP)
    drow = dtT_ref[pl.ds(r0, HEADS_PER_GROUP), :]
    crow = cumT_ref[pl.ds(r0, HEADS_PER_GROUP), :]

    cbm = lax.dot_general(cb_, bb, (((1,), (1,)), ((), ())), preferred_element_type=F32)
    yd = []
    for r in range(HEADS_PER_GROUP):
        seg = ccol[:, r:r + 1] - crow[r:r + 1, :]
        decay = jnp.exp(jnp.where(causal, seg, -jnp.inf))
        m = (cbm * decay * drow[r:r + 1, :]).astype(BF16)
        yd.append(jnp.dot(m, xb[:, r * SSM_HEAD_DIM:(r + 1) * SSM_HEAD_DIM], preferred_element_type=F32))
    y = jnp.concatenate(yd, axis=1)

    st = state_ref[g]
    y = y + jnp.dot(cb_, st.astype(BF16), preferred_element_type=F32) * _expand_heads(jnp.exp(ccol), q)
    clast = ccol[q - 1:q, :]
    wl = jnp.exp(clast - ccol) * dcol
    xw = (xg * _expand_heads(wl, q)).astype(BF16)
    ds = lax.dot_general(bb, xw, (((0,), (0,)), ((), ())), preferred_element_type=F32)
    state_ref[g] = st * _expand_heads(jnp.exp(clast), 1) + ds

    y = y + dskip_ref[...] * xg
    y = y * _silu(z_ref[...].astype(F32))
    y_ref[...] = (y * _rms_scale(y) * nw_ref[...]).astype(y_ref.dtype)


def ssd_mixer(proj, dt_raw, conv_w_g, conv_b_g, dt_bias, a_log, d_skip, norm_w):
    seq = proj.shape[0]
    n_chunks = seq // CHUNK
    xcol0 = D_INNER // GROUP_INNER
    bcol0 = 2 * D_INNER // SSM_STATE
    ccol0 = bcol0 + GN // SSM_STATE
    return pl.pallas_call(
        _ssd_kernel, grid=(n_chunks, SSM_GROUPS),
        in_specs=[pl.BlockSpec((CHUNK, GROUP_INNER), lambda c, g: (c, g)),
                  pl.BlockSpec((CHUNK, GROUP_INNER), lambda c, g: (c, xcol0 + g)),
                  pl.BlockSpec((CHUNK, SSM_STATE), lambda c, g: (c, bcol0 + g)),
                  pl.BlockSpec((CHUNK, SSM_STATE), lambda c, g: (c, ccol0 + g)),
                  pl.BlockSpec((CHUNK, LANES), lambda c, g: (c, 0)),
                  pl.BlockSpec((None, CONV_WIDTH, GROUP_CONV), lambda c, g: (g, 0, 0)),
                  pl.BlockSpec((None, 1, GROUP_CONV), lambda c, g: (g, 0, 0)),
                  pl.BlockSpec((1, LANES), lambda c, g: (0, 0)),
                  pl.BlockSpec((1, LANES), lambda c, g: (0, 0)),
                  pl.BlockSpec((1, GROUP_INNER), lambda c, g: (0, g)),
                  pl.BlockSpec((1, GROUP_INNER), lambda c, g: (0, g))],
        out_specs=pl.BlockSpec((CHUNK, GROUP_INNER), lambda c, g: (c, g)),
        out_shape=jax.ShapeDtypeStruct((seq, D_INNER), BF16),
        scratch_shapes=[pltpu.VMEM((SSM_GROUPS, SSM_STATE, GROUP_INNER), F32),
                        pltpu.VMEM((SSM_GROUPS, SUBLANES, GROUP_CONV), F32),
                        pltpu.VMEM((SUBLANES + CHUNK, GROUP_CONV), F32),
                        pltpu.VMEM((CHUNK, LANES), F32), pltpu.VMEM((CHUNK, LANES), F32),
                        pltpu.VMEM((LANES, CHUNK), F32), pltpu.VMEM((LANES, CHUNK), F32)],
        compiler_params=_params("arbitrary", "arbitrary"), name="ssd_mixer",
    )(proj, proj, proj, proj, dt_raw, conv_w_g, conv_b_g, dt_bias, a_log, d_skip, norm_w)


def _attn_kernel(sink_ref, q_ref, kvp_ref, kvc_ref, o_ref):
    i = pl.program_id(0)
    qi = lax.broadcasted_iota(jnp.int32, (BLOCK, 2 * BLOCK), 0)
    sj = lax.broadcasted_iota(jnp.int32, (BLOCK, 2 * BLOCK), 1)
    dist_i = BLOCK + qi - sj
    valid = (dist_i >= 0) & (dist_i < WINDOW) & ((sj >= BLOCK) | (i > 0))
    dist = dist_i.astype(F32)
    scale = HEAD_DIM ** -0.5
    kv = jnp.concatenate([kvp_ref[...], kvc_ref[...]], axis=0)
    for kh in range(N_KV_HEADS):
        k = kv[:, kh * HEAD_DIM:(kh + 1) * HEAD_DIM]
        v = kv[:, KV_DIM + kh * HEAD_DIM:KV_DIM + (kh + 1) * HEAD_DIM]
        for r in range(Q_PER_KV):
            hd = kh * Q_PER_KV + r
            slope = 2.0 ** (-8.0 * (hd + 1) / N_Q_HEADS)
            sink = sink_ref[hd]
            qh = q_ref[:, hd * HEAD_DIM:(hd + 1) * HEAD_DIM]
            s = lax.dot_general(qh, k, (((1,), (1,)), ((), ())), preferred_element_type=F32)
            logits = jnp.where(valid, s * scale - slope * dist, -jnp.inf)
            m = jnp.maximum(jnp.max(logits, axis=-1, keepdims=True), sink)
            p = jnp.exp(logits - m)
            denom = jnp.sum(p, axis=-1, keepdims=True) + jnp.exp(sink - m)
            out = jnp.dot(p.astype(BF16), v, preferred_element_type=F32) / denom
            o_ref[:, hd * HEAD_DIM:(hd + 1) * HEAD_DIM] = out.astype(o_ref.dtype)


def swa_attention(q, kv, sinks):
    seq = q.shape[0]
    nb = seq // BLOCK
    return pl.pallas_call(
        _attn_kernel, grid=(nb,),
        in_specs=[pl.BlockSpec(memory_space=pltpu.SMEM),
                  pl.BlockSpec((BLOCK, N_Q_HEADS * HEAD_DIM), lambda i: (i, 0)),
                  pl.BlockSpec((BLOCK, 2 * KV_DIM), lambda i: (jnp.maximum(i - 1, 0), 0)),
                  pl.BlockSpec((BLOCK, 2 * KV_DIM), lambda i: (i, 0))],
        out_specs=pl.BlockSpec((BLOCK, N_Q_HEADS * HEAD_DIM), lambda i: (i, 0)),
        out_shape=jax.ShapeDtypeStruct((seq, N_Q_HEADS * HEAD_DIM), BF16),
        compiler_params=_params("parallel"), name="swa_attention",
    )(sinks, q, kv, kv)


def _pad_lanes(v, width=LANES):
    v = v.reshape(1, -1).astype(F32)
    return jnp.pad(v, ((0, 0), (0, width - v.shape[1])))


def kernel(x, norm_w, ssm_w_in, ssm_conv_w, ssm_conv_b, ssm_dt_bias, ssm_A_log, ssm_D, ssm_norm_w, ssm_w_out,
           kv_norm_w, w_kv, attn_w_q, attn_sinks, attn_w_o, ffn_w_gate, ffn_w_up, ffn_w_down):
    batch, seq, d = x.shape
    assert batch == 1 and d == D_MODEL and seq % CHUNK == 0
    h = x.reshape(seq, d)
    row = lambda v: v.reshape(1, -1).astype(F32)

    g = norm_w[0]
    w_in = ssm_w_in[0]
    n_main = 2 * D_INNER + 2 * GN
    w_main = w_in[:, :n_main].astype(BF16)
    w_dt = jnp.pad(w_in[:, n_main:], ((0, 0), (0, LANES - SSM_HEADS))).astype(BF16)
    proj, dt_raw = norm_matmul(h, row(g[0]), w_main, w2=w_dt, name="in_proj")

    def per_group(t):
        xs = t[:, :D_INNER].reshape(-1, SSM_GROUPS, GROUP_INNER)
        bs = t[:, D_INNER:D_INNER + GN].reshape(-1, SSM_GROUPS, SSM_STATE)
        cs = t[:, D_INNER + GN:].reshape(-1, SSM_GROUPS, SSM_STATE)
        return jnp.transpose(jnp.concatenate([xs, bs, cs], axis=-1), (1, 0, 2)).astype(F32)

    y = ssd_mixer(proj, dt_raw, per_group(ssm_conv_w[0]), per_group(ssm_conv_b[0].reshape(1, -1)),
                  _pad_lanes(ssm_dt_bias[0]), _pad_lanes(ssm_A_log[0]),
                  row(jnp.repeat(ssm_D[0], SSM_HEAD_DIM)), row(ssm_norm_w[0]))
    h = matmul_norm_residual(y, ssm_w_out[0].astype(BF16), h, row(g[1]), name="ssm_out_proj")
    h = ffn(h, row(g[2]), ffn_w_gate[0].astype(BF16), ffn_w_up[0].astype(BF16), ffn_w_down[0].astype(BF16),
            row(g[3]), name="ffn0")

    g = norm_w[1]
    kv = norm_matmul(h, row(kv_norm_w), w_kv.astype(BF16), name="kv_proj")
    q = norm_matmul(h, row(g[0]), attn_w_q[0].astype(BF16), name="q_proj")
    a = swa_attention(q, kv, attn_sinks[0].astype(F32))
    h = matmul_norm_residual(a, attn_w_o[0].astype(BF16), h, row(g[1]), name="attn_out_proj")
    h = ffn(h, row(g[2]), ffn_w_gate[1].astype(BF16), ffn_w_up[1].astype(BF16), ffn_w_down[1].astype(BF16),
            row(g[3]), name="ffn1")
    return h.reshape(batch, seq, d)
```

```python
import math

import jax
import jax.numpy as jnp
from jax import lax
from jax.experimental import pallas as pl
from jax.experimental.pallas import tpu as pltpu

D_MODEL = 2048
D_INNER = 4096
SSM_HEAD_DIM = 64
SSM_HEADS = 64
SSM_GROUPS = 8
HEADS_PER_GROUP = SSM_HEADS // SSM_GROUPS
SSM_STATE = 128
CONV_WIDTH = 4
GN = SSM_GROUPS * SSM_STATE
GROUP_INNER = D_INNER // SSM_GROUPS
GROUP_CONV = GROUP_INNER + 2 * SSM_STATE
SSD_SUB = 128
SSD_ROWS = 512
N_Q_HEADS = 32
N_KV_HEADS = 4
Q_PER_KV = N_Q_HEADS // N_KV_HEADS
HEAD_DIM = 64
WINDOW = 128
BLOCK = 128
KV_DIM = N_KV_HEADS * HEAD_DIM
D_FF = 5632
EPS = 1e-6
LOG2E = math.log2(math.e)

LANES = 128
SUBLANES = 8
VMEM_LIMIT_BYTES = 56 * 1024 * 1024

BF16 = jnp.bfloat16
F32 = jnp.float32


def _rms_scale(x):
    return lax.rsqrt(jnp.mean(x * x, axis=-1, keepdims=True) + EPS)


def _silu(x):
    return x * jax.nn.sigmoid(x)


def _params(*semantics):
    return pltpu.CompilerParams(dimension_semantics=semantics, vmem_limit_bytes=VMEM_LIMIT_BYTES)


def _norm_matmul_kernel(x_ref, g_ref, w_ref, o_ref, xn_ref):
    @pl.when(pl.program_id(1) == 0)
    def _():
        x = x_ref[...]
        xn_ref[...] = (x * _rms_scale(x) * g_ref[...]).astype(xn_ref.dtype)

    o_ref[...] = jnp.dot(xn_ref[...], w_ref[...], preferred_element_type=F32).astype(o_ref.dtype)


def _norm_matmul2_kernel(x_ref, g_ref, w_ref, w2_ref, o_ref, o2_ref, xn_ref):
    @pl.when(pl.program_id(1) == 0)
    def _():
        x = x_ref[...]
        xn = (x * _rms_scale(x) * g_ref[...]).astype(xn_ref.dtype)
        xn_ref[...] = xn
        o2_ref[...] = jnp.dot(xn, w2_ref[...], preferred_element_type=F32)

    o_ref[...] = jnp.dot(xn_ref[...], w_ref[...], preferred_element_type=F32).astype(o_ref.dtype)


def _norm_matmul_t_kernel(x_ref, g_ref, wt_ref, o_ref, xn_ref):
    @pl.when(pl.program_id(1) == 0)
    def _():
        x = x_ref[...]
        xn_ref[...] = (x * _rms_scale(x) * g_ref[...]).astype(xn_ref.dtype)

    o_ref[...] = lax.dot_general(wt_ref[...], xn_ref[...], (((1,), (1,)), ((), ())),
                                 preferred_element_type=F32).astype(o_ref.dtype)


def norm_matmul_t(x, g, wt, *, tm=1024, tn=1024, name):
    m, k = x.shape
    n = wt.shape[0]
    tm, tn = min(tm, m), min(tn, n)
    return pl.pallas_call(
        _norm_matmul_t_kernel, grid=(m // tm, n // tn),
        in_specs=[pl.BlockSpec((tm, k), lambda i, j: (i, 0)),
                  pl.BlockSpec((1, k), lambda i, j: (0, 0)),
                  pl.BlockSpec((tn, k), lambda i, j: (j, 0))],
        out_specs=pl.BlockSpec((tn, tm), lambda i, j: (j, i)),
        out_shape=jax.ShapeDtypeStruct((n, m), BF16), scratch_shapes=[pltpu.VMEM((tm, k), BF16)],
        compiler_params=_params("parallel", "arbitrary"), name=name)(x, g, wt)


def norm_matmul(x, g, w, *, w2=None, tm=1024, tn=1024, name):
    m, k = x.shape
    n = w.shape[1]
    tm, tn = min(tm, m), min(tn, n)
    grid = (m // tm, n // tn)
    x_spec = pl.BlockSpec((tm, k), lambda i, j: (i, 0))
    g_spec = pl.BlockSpec((1, k), lambda i, j: (0, 0))
    w_spec = pl.BlockSpec((k, tn), lambda i, j: (0, j))
    o_spec = pl.BlockSpec((tm, tn), lambda i, j: (i, j))
    scratch = [pltpu.VMEM((tm, k), BF16)]
    if w2 is None:
        return pl.pallas_call(
            _norm_matmul_kernel, grid=grid, in_specs=[x_spec, g_spec, w_spec], out_specs=o_spec,
            out_shape=jax.ShapeDtypeStruct((m, n), BF16), scratch_shapes=scratch,
            compiler_params=_params("parallel", "arbitrary"), name=name)(x, g, w)
    n2 = w2.shape[1]
    return pl.pallas_call(
        _norm_matmul2_kernel, grid=grid,
        in_specs=[x_spec, g_spec, w_spec, pl.BlockSpec((k, n2), lambda i, j: (0, 0))],
        out_specs=[o_spec, pl.BlockSpec((tm, n2), lambda i, j: (i, 0))],
        out_shape=[jax.ShapeDtypeStruct((m, n), BF16), jax.ShapeDtypeStruct((m, n2), F32)],
        scratch_shapes=scratch, compiler_params=_params("parallel", "arbitrary"), name=name)(x, g, w, w2)


def _matmul_norm_residual_kernel(y_ref, w_ref, h_ref, g_ref, o_ref, acc_ref):
    k = pl.program_id(1)
    d = jnp.dot(y_ref[...], w_ref[...], preferred_element_type=F32)

    @pl.when(k == 0)
    def _():
        acc_ref[...] = d

    @pl.when(k > 0)
    def _():
        acc_ref[...] += d

    @pl.when(k == pl.num_programs(1) - 1)
    def _():
        a = acc_ref[...]
        o_ref[...] = h_ref[...] + a * _rms_scale(a) * g_ref[...]


def matmul_norm_residual(y, w, h, g, *, tm=512, tk=1024, name):
    m, k = y.shape
    n = w.shape[1]
    tm, tk = min(tm, m), min(tk, k)
    return pl.pallas_call(
        _matmul_norm_residual_kernel, grid=(m // tm, k // tk),
        in_specs=[pl.BlockSpec((tm, tk), lambda i, kk: (i, kk)),
                  pl.BlockSpec((tk, n), lambda i, kk: (kk, 0)),
                  pl.BlockSpec((tm, n), lambda i, kk: (i, 0)),
                  pl.BlockSpec((1, n), lambda i, kk: (0, 0))],
        out_specs=pl.BlockSpec((tm, n), lambda i, kk: (i, 0)),
        out_shape=jax.ShapeDtypeStruct((m, n), F32),
        scratch_shapes=[pltpu.VMEM((tm, n), F32)],
        compiler_params=_params("parallel", "arbitrary"), name=name)(y, w, h, g)


def _ffn_kernel(h_ref, gpre_ref, wg_ref, wu_ref, wd_ref, gpost_ref, o_ref, xn_ref, acc_ref):
    f = pl.program_id(1)

    @pl.when(f == 0)
    def _():
        x = h_ref[...]
        xn_ref[...] = (x * _rms_scale(x) * gpre_ref[...]).astype(xn_ref.dtype)

    xn = xn_ref[...]
    gate = jnp.dot(xn, wg_ref[...], preferred_element_type=F32)
    up = jnp.dot(xn, wu_ref[...], preferred_element_type=F32)
    act = (_silu(gate) * up).astype(BF16)
    d = jnp.dot(act, wd_ref[...], preferred_element_type=F32)

    @pl.when(f == 0)
    def _():
        acc_ref[...] = d

    @pl.when(f > 0)
    def _():
        acc_ref[...] += d

    @pl.when(f == pl.num_programs(1) - 1)
    def _():
        a = acc_ref[...]
        o_ref[...] = h_ref[...] + a * _rms_scale(a) * gpost_ref[...]


def ffn(h, g_pre, w_gate, w_up, w_down, g_post, *, tm=512, tf=512, name):
    m, d = h.shape
    dff = w_gate.shape[1]
    tm = min(tm, m)
    return pl.pallas_call(
        _ffn_kernel, grid=(m // tm, dff // tf),
        in_specs=[pl.BlockSpec((tm, d), lambda i, f: (i, 0)),
                  pl.BlockSpec((1, d), lambda i, f: (0, 0)),
                  pl.BlockSpec((d, tf), lambda i, f: (0, f)),
                  pl.BlockSpec((d, tf), lambda i, f: (0, f)),
                  pl.BlockSpec((tf, d), lambda i, f: (f, 0)),
                  pl.BlockSpec((1, d), lambda i, f: (0, 0))],
        out_specs=pl.BlockSpec((tm, d), lambda i, f: (i, 0)),
        out_shape=jax.ShapeDtypeStruct((m, d), F32),
        scratch_shapes=[pltpu.VMEM((tm, d), BF16), pltpu.VMEM((tm, d), F32)],
        compiler_params=_params("parallel", "arbitrary"), name=name)(h, g_pre, w_gate, w_up, w_down, g_post)


def _expand_heads(v, rows):
    lane = lax.broadcasted_iota(jnp.int32, (rows, LANES), 1)
    first = lane < SSM_HEAD_DIM
    pairs = [jnp.where(first, v[:, 2 * p:2 * p + 1], v[:, 2 * p + 1:2 * p + 2]) for p in range(HEADS_PER_GROUP // 2)]
    return jnp.concatenate(pairs, axis=1)


def _split_bf16(v):
    hi = v.astype(BF16)
    lo = (v - hi.astype(F32)).astype(BF16)
    return jnp.concatenate([hi, lo], axis=1)


def _ssd_kernel(z_ref, xs_ref, b_ref, c_ref, dt_ref, cw_ref, cb_ref, dtb_ref, alog_ref, dskip_ref, nw_ref,
                y_ref, state_ref, tail_ref, pad_ref, act_ref, ccol_ref, crowT_ref, e1_ref, wl_ref, elast_ref):
    c = pl.program_id(0)
    g = pl.program_id(1)
    rows = z_ref.shape[0]
    n_sub = rows // SSD_SUB
    sub_row = lax.broadcasted_iota(jnp.int32, (SSD_SUB, SSD_SUB), 0)
    sub_col = lax.broadcasted_iota(jnp.int32, (SSD_SUB, SSD_SUB), 1)
    causal = sub_row >= sub_col

    @pl.when(c == 0)
    def _():
        state_ref[g] = jnp.zeros((SSM_STATE, GROUP_INNER), F32)
        tail_ref[g] = jnp.zeros((SUBLANES, GROUP_CONV), F32)

    @pl.when(g == 0)
    def _():
        dtv = jax.nn.softplus(dt_ref[...] + dtb_ref[...])
        dta = dtv * (-jnp.exp(alog_ref[...]))
        log_dt = jnp.log(dtv)
        tri = causal.astype(F32)
        for s in range(n_sub):
            sl = slice(s * SSD_SUB, (s + 1) * SSD_SUB)
            cum = jnp.dot(tri, dta[sl], precision=lax.Precision.HIGHEST,
                          preferred_element_type=F32)
            clast = cum[SSD_SUB - 1:SSD_SUB, :]
            ccol_ref[sl, :] = cum * LOG2E
            crowT_ref[:, sl] = ((cum - log_dt[sl]) * LOG2E).T
            e1_ref[sl, :] = _split_bf16(jnp.exp(cum))
            wl_ref[sl, :] = _split_bf16(jnp.exp(clast - cum) * dtv[sl])
            elast_ref[s] = jnp.exp(clast)

    xcur = jnp.concatenate([xs_ref[...].astype(F32), b_ref[...].astype(F32), c_ref[...].astype(F32)], axis=1)
    cw = cw_ref[...]
    bias = cb_ref[...]
    pad_ref[0:SUBLANES, :] = tail_ref[g]
    pad_ref[SUBLANES:2 * SUBLANES, :] = xcur[0:SUBLANES]
    tail_ref[g] = xcur[rows - SUBLANES:rows]
    head = bias
    body = bias + cw[CONV_WIDTH - 1:CONV_WIDTH, :] * xcur
    for k in range(CONV_WIDTH):
        lo = SUBLANES - (CONV_WIDTH - 1) + k
        head = head + cw[k:k + 1, :] * pad_ref[lo:lo + SUBLANES, :]
        if k < CONV_WIDTH - 1:
            body = body + cw[k:k + 1, :] * pltpu.roll(xcur, CONV_WIDTH - 1 - k, 0)
    act_ref[...] = _silu(jnp.concatenate([head, body[SUBLANES:]], axis=0))

    shift = lax.rem(LANES - HEADS_PER_GROUP * g, LANES)
    r0 = pl.multiple_of(g * HEADS_PER_GROUP, HEADS_PER_GROUP)
    lane16 = sub_col.astype(F32).astype(BF16)
    low_half = lane16 < SSM_HEAD_DIM
    zero16 = jnp.zeros((SSD_SUB, LANES), BF16)
    sel_row = lax.broadcasted_iota(jnp.int32, (2 * LANES, GROUP_INNER), 0) & (LANES - 1)
    sel_head = lax.broadcasted_iota(jnp.int32, (2 * LANES, GROUP_INNER), 1) // SSM_HEAD_DIM
    expand = (sel_row == sel_head + HEADS_PER_GROUP * g).astype(F32).astype(BF16)

    def sub_chunk(s, carry):
        off = pl.multiple_of(s * SSD_SUB, SSD_SUB)
        rs = pl.ds(off, SSD_SUB)
        act = act_ref[rs, :]
        xg = act[:, 0:GROUP_INNER]
        xb = xg.astype(BF16)
        bb = act[:, GROUP_INNER:GROUP_INNER + SSM_STATE].astype(BF16)
        cc = act[:, GROUP_INNER + SSM_STATE:GROUP_CONV].astype(BF16)
        ccol = pltpu.roll(ccol_ref[rs, :], shift, 1)
        crow = crowT_ref[pl.ds(r0, HEADS_PER_GROUP), rs]
        cbm = lax.dot_general(cc, bb, (((1,), (1,)), ((), ())), preferred_element_type=F32).astype(BF16)

        ys = []
        for p in range(HEADS_PER_GROUP // 2):
            ms = []
            for r in (2 * p, 2 * p + 1):
                seg = ccol[:, r:r + 1] - crow[r:r + 1, :]
                ms.append(cbm * jnp.exp2(jnp.where(causal, seg, -jnp.inf)).astype(BF16))
            xp = xb[:, p * LANES:(p + 1) * LANES]
            bd = jnp.concatenate([jnp.where(low_half, xp, zero16), jnp.where(low_half, zero16, xp)], axis=0)
            ys.append(jnp.dot(jnp.concatenate(ms, axis=1), bd, preferred_element_type=F32))
        y = jnp.concatenate(ys, axis=1)

        st = state_ref[g]
        e1 = jnp.dot(e1_ref[rs, :], expand, preferred_element_type=F32)
        y = y + jnp.dot(cc, st.astype(BF16), preferred_element_type=F32) * e1
        wl = jnp.dot(wl_ref[rs, :], expand, preferred_element_type=F32)
        xw = (xg * wl).astype(BF16)
        ds = lax.dot_general(bb, xw, (((0,), (0,)), ((), ())), preferred_element_type=F32)
        el = pltpu.roll(elast_ref[s], shift, 1)
        state_ref[g] = st * _expand_heads(el, 1) + ds

        y = y + dskip_ref[...] * xg
        y = y * _silu(z_ref[rs, :].astype(F32))
        y_ref[rs, :] = (y * _rms_scale(y) * nw_ref[...]).astype(y_ref.dtype)
        return carry

    lax.fori_loop(0, n_sub, sub_chunk, 0, unroll=True)


def ssd_mixer(proj, dt_raw, conv_w_g, conv_b_g, dt_bias, a_log, d_skip, norm_w):
    seq = proj.shape[0]
    rows = min(SSD_ROWS, seq)
    n_sub = rows // SSD_SUB
    xcol0 = D_INNER // GROUP_INNER
    bcol0 = 2 * D_INNER // SSM_STATE
    ccol0 = bcol0 + GN // SSM_STATE
    return pl.pallas_call(
        _ssd_kernel, grid=(seq // rows, SSM_GROUPS),
        in_specs=[pl.BlockSpec((rows, GROUP_INNER), lambda c, g: (c, g)),
                  pl.BlockSpec((rows, GROUP_INNER), lambda c, g: (c, xcol0 + g)),
                  pl.BlockSpec((rows, SSM_STATE), lambda c, g: (c, bcol0 + g)),
                  pl.BlockSpec((rows, SSM_STATE), lambda c, g: (c, ccol0 + g)),
                  pl.BlockSpec((rows, LANES), lambda c, g: (c, 0)),
                  pl.BlockSpec((None, CONV_WIDTH, GROUP_CONV), lambda c, g: (g, 0, 0)),
                  pl.BlockSpec((None, 1, GROUP_CONV), lambda c, g: (g, 0, 0)),
                  pl.BlockSpec((1, LANES), lambda c, g: (0, 0)),
                  pl.BlockSpec((1, LANES), lambda c, g: (0, 0)),
                  pl.BlockSpec((1, GROUP_INNER), lambda c, g: (0, g)),
                  pl.BlockSpec((1, GROUP_INNER), lambda c, g: (0, g))],
        out_specs=pl.BlockSpec((rows, GROUP_INNER), lambda c, g: (c, g)),
        out_shape=jax.ShapeDtypeStruct((seq, D_INNER), BF16),
        scratch_shapes=[pltpu.VMEM((SSM_GROUPS, SSM_STATE, GROUP_INNER), F32),
                        pltpu.VMEM((SSM_GROUPS, SUBLANES, GROUP_CONV), F32),
                        pltpu.VMEM((2 * SUBLANES, GROUP_CONV), F32),
                        pltpu.VMEM((rows, GROUP_CONV), F32),
                        pltpu.VMEM((rows, LANES), F32),
                        pltpu.VMEM((LANES, rows), F32),
                        pltpu.VMEM((rows, 2 * LANES), BF16),
                        pltpu.VMEM((rows, 2 * LANES), BF16),
                        pltpu.VMEM((n_sub, 1, LANES), F32)],
        compiler_params=_params("arbitrary", "arbitrary"), name="ssd_mixer",
    )(proj, proj, proj, proj, dt_raw, conv_w_g, conv_b_g, dt_bias, a_log, d_skip, norm_w)


def _attn_kernel(sink_ref, qt_ref, kvp_ref, kvc_ref, o_ref, bias_ref):
    i = pl.program_id(0)
    group_lanes = Q_PER_KV * BLOCK

    @pl.when(i == 0)
    def _():
        sj = lax.broadcasted_iota(jnp.int32, (2 * BLOCK, group_lanes), 0)
        lane = lax.broadcasted_iota(jnp.int32, (2 * BLOCK, group_lanes), 1)
        dist_i = BLOCK + (lane & (BLOCK - 1)) - sj
        valid = (dist_i >= 0) & (dist_i < WINDOW)
        dist = dist_i.astype(F32)
        head_in_group = (lane // BLOCK).astype(F32)
        for kh in range(N_KV_HEADS):
            slope = jnp.exp2((head_in_group + (kh * Q_PER_KV + 1)) * (-8.0 / N_Q_HEADS))
            b = jnp.where(valid, -(slope * dist), -jnp.inf)
            bias_ref[1, kh] = b
            bias_ref[0, kh] = jnp.where(sj >= BLOCK, b, -jnp.inf)

    has_prev = jnp.minimum(i, 1)
    scale = HEAD_DIM ** -0.5
    kv = jnp.concatenate([kvp_ref[...], kvc_ref[...]], axis=0)
    for kh in range(N_KV_HEADS):
        k = kv[:, kh * HEAD_DIM:(kh + 1) * HEAD_DIM] * scale
        v = kv[:, KV_DIM + kh * HEAD_DIM:KV_DIM + (kh + 1) * HEAD_DIM]
        h0 = kh * Q_PER_KV
        qg = jnp.concatenate([qt_ref[(h0 + r) * HEAD_DIM:(h0 + r + 1) * HEAD_DIM, :] for r in range(Q_PER_KV)],
                             axis=1)
        sink = sink_ref[kh]
        logits = jnp.dot(k, qg, preferred_element_type=F32) + bias_ref[has_prev, kh]
        m = jnp.maximum(jnp.max(logits, axis=0, keepdims=True), sink)
        p = jnp.exp(logits - m)
        denom = jnp.sum(p, axis=0, keepdims=True) + jnp.exp(sink - m)
        ot = lax.dot_general(v, p.astype(BF16), (((0,), (0,)), ((), ())), preferred_element_type=F32) / denom
        for r in range(0, Q_PER_KV, 2):
            pair = jnp.concatenate([ot[:, r * BLOCK:(r + 1) * BLOCK], ot[:, (r + 1) * BLOCK:(r + 2) * BLOCK]], axis=0)
            o_ref[:, (h0 + r) * HEAD_DIM:(h0 + r + 2) * HEAD_DIM] = pair.T.astype(o_ref.dtype)


def swa_attention(qt, kv, sinks):
    seq = kv.shape[0]
    nb = seq // BLOCK
    sink_rows = jnp.repeat(sinks.astype(F32), BLOCK).reshape(N_KV_HEADS, 1, Q_PER_KV * BLOCK)
    return pl.pallas_call(
        _attn_kernel, grid=(nb,),
        in_specs=[pl.BlockSpec((N_KV_HEADS, 1, Q_PER_KV * BLOCK), lambda i: (0, 0, 0)),
                  pl.BlockSpec((N_Q_HEADS * HEAD_DIM, BLOCK), lambda i: (0, i)),
                  pl.BlockSpec((BLOCK, 2 * KV_DIM), lambda i: (jnp.maximum(i - 1, 0), 0)),
                  pl.BlockSpec((BLOCK, 2 * KV_DIM), lambda i: (i, 0))],
        out_specs=pl.BlockSpec((BLOCK, N_Q_HEADS * HEAD_DIM), lambda i: (i, 0)),
        out_shape=jax.ShapeDtypeStruct((seq, N_Q_HEADS * HEAD_DIM), BF16),
        scratch_shapes=[pltpu.VMEM((2, N_KV_HEADS, 2 * BLOCK, Q_PER_KV * BLOCK), F32)],
        compiler_params=_params("arbitrary"), name="swa_attention",
    )(sink_rows, qt, kv, kv)


def _pad_lanes(v, width=LANES):
    v = v.reshape(1, -1).astype(F32)
    return jnp.pad(v, ((0, 0), (0, width - v.shape[1])))


def kernel(x, norm_w, ssm_w_in, ssm_conv_w, ssm_conv_b, ssm_dt_bias, ssm_A_log, ssm_D, ssm_norm_w, ssm_w_out,
           kv_norm_w, w_kv, attn_w_q, attn_sinks, attn_w_o, ffn_w_gate, ffn_w_up, ffn_w_down):
    batch, seq, d = x.shape
    assert batch == 1 and d == D_MODEL and seq % SSD_ROWS == 0
    h = x.reshape(seq, d)
    row = lambda v: v.reshape(1, -1).astype(F32)

    g = norm_w[0]
    w_in = ssm_w_in[0]
    n_main = 2 * D_INNER + 2 * GN
    w_main = w_in[:, :n_main].astype(BF16)
    w_dt = jnp.pad(w_in[:, n_main:], ((0, 0), (0, LANES - SSM_HEADS))).astype(BF16)
    proj, dt_raw = norm_matmul(h, row(g[0]), w_main, w2=w_dt, name="in_proj")

    def per_group(t):
        xs = t[:, :D_INNER].reshape(-1, SSM_GROUPS, GROUP_INNER)
        bs = t[:, D_INNER:D_INNER + GN].reshape(-1, SSM_GROUPS, SSM_STATE)
        cs = t[:, D_INNER + GN:].reshape(-1, SSM_GROUPS, SSM_STATE)
        return jnp.transpose(jnp.concatenate([xs, bs, cs], axis=-1), (1, 0, 2)).astype(F32)

    y = ssd_mixer(proj, dt_raw, per_group(ssm_conv_w[0]), per_group(ssm_conv_b[0].reshape(1, -1)),
                  _pad_lanes(ssm_dt_bias[0]), _pad_lanes(ssm_A_log[0]),
                  row(jnp.repeat(ssm_D[0], SSM_HEAD_DIM)), row(ssm_norm_w[0]))
    h = matmul_norm_residual(y, ssm_w_out[0].astype(BF16), h, row(g[1]), name="ssm_out_proj")
    h = ffn(h, row(g[2]), ffn_w_gate[0].astype(BF16), ffn_w_up[0].astype(BF16), ffn_w_down[0].astype(BF16),
            row(g[3]), name="ffn0")

    g = norm_w[1]
    kv = norm_matmul(h, row(kv_norm_w), w_kv.astype(BF16), name="kv_proj")
    qt = norm_matmul_t(h, row(g[0]), attn_w_q[0].T.astype(BF16), name="q_proj")
    a = swa_attention(qt, kv, attn_sinks[0])
    h = matmul_norm_residual(a, attn_w_o[0].astype(BF16), h, row(g[1]), name="attn_out_proj")
    h = ffn(h, row(g[2]), ffn_w_gate[1].astype(BF16), ffn_w_up[1].astype(BF16), ffn_w_down[1].astype(BF16),
            row(g[3]), name="ffn1")
    return h.reshape(batch, seq, d)
```

```python
import math

import jax
import jax.numpy as jnp
from jax import lax
from jax.experimental import pallas as pl
from jax.experimental.pallas import tpu as pltpu

D_MODEL = 2048
D_INNER = 4096
SSM_HEAD_DIM = 64
SSM_HEADS = 64
SSM_GROUPS = 8
HEADS_PER_GROUP = SSM_HEADS // SSM_GROUPS
SSM_STATE = 128
CONV_WIDTH = 4
GN = SSM_GROUPS * SSM_STATE
GROUP_INNER = D_INNER // SSM_GROUPS
GROUP_CONV = GROUP_INNER + 2 * SSM_STATE
SSD_SUB = 128
SSD_ROWS = 512
N_Q_HEADS = 32
N_KV_HEADS = 4
Q_PER_KV = N_Q_HEADS // N_KV_HEADS
HEAD_DIM = 64
WINDOW = 128
BLOCK = 128
KV_DIM = N_KV_HEADS * HEAD_DIM
D_FF = 5632
EPS = 1e-6
LOG2E = math.log2(math.e)

LANES = 128
SUBLANES = 8
VMEM_LIMIT_BYTES = 63 * 1024 * 1024

BF16 = jnp.bfloat16
F32 = jnp.float32


def _rms_scale(x):
    return lax.rsqrt(jnp.mean(x * x, axis=-1, keepdims=True) + EPS)


def _silu(x):
    return x * jax.nn.sigmoid(x)


def _params(*semantics):
    return pltpu.CompilerParams(dimension_semantics=semantics, vmem_limit_bytes=VMEM_LIMIT_BYTES)


def _norm_matmul_kernel(x_ref, g_ref, w_ref, o_ref, xn_ref):
    @pl.when(pl.program_id(1) == 0)
    def _():
        x = x_ref[...]
        xn_ref[...] = (x * _rms_scale(x) * g_ref[...]).astype(xn_ref.dtype)

    o_ref[...] = jnp.dot(xn_ref[...], w_ref[...], preferred_element_type=F32).astype(o_ref.dtype)


def _norm_matmul2_kernel(x_ref, g_ref, w_ref, w2_ref, o_ref, o2_ref, xn_ref):
    @pl.when(pl.program_id(1) == 0)
    def _():
        x = x_ref[...]
        xn = (x * _rms_scale(x) * g_ref[...]).astype(xn_ref.dtype)
        xn_ref[...] = xn
        o2_ref[...] = jnp.dot(xn, w2_ref[...], preferred_element_type=F32)

    o_ref[...] = jnp.dot(xn_ref[...], w_ref[...], preferred_element_type=F32).astype(o_ref.dtype)


def _norm_matmul_t_kernel(x_ref, g_ref, w_ref, o_ref, xn_ref):
    @pl.when(pl.program_id(1) == 0)
    def _():
        x = x_ref[...]
        xn_ref[...] = (x * _rms_scale(x) * g_ref[...]).astype(xn_ref.dtype)

    o_ref[...] = lax.dot_general(w_ref[...], xn_ref[...], (((0,), (1,)), ((), ())),
                                 preferred_element_type=F32).astype(o_ref.dtype)


def norm_matmul_t(x, g, w, *, tm=1024, tn=1024, name):
    m, k = x.shape
    n = w.shape[1]
    tm, tn = min(tm, m), min(tn, n)
    return pl.pallas_call(
        _norm_matmul_t_kernel, grid=(m // tm, n // tn),
        in_specs=[pl.BlockSpec((tm, k), lambda i, j: (i, 0)),
                  pl.BlockSpec((1, k), lambda i, j: (0, 0)),
                  pl.BlockSpec((k, tn), lambda i, j: (0, j))],
        out_specs=pl.BlockSpec((tn, tm), lambda i, j: (j, i)),
        out_shape=jax.ShapeDtypeStruct((n, m), BF16), scratch_shapes=[pltpu.VMEM((tm, k), BF16)],
        compiler_params=_params("parallel", "arbitrary"), name=name)(x, g, w)


def norm_matmul(x, g, w, *, w2=None, tm=1024, tn=1024, name):
    m, k = x.shape
    n = w.shape[1]
    tm, tn = min(tm, m), min(tn, n)
    grid = (m // tm, n // tn)
    x_spec = pl.BlockSpec((tm, k), lambda i, j: (i, 0))
    g_spec = pl.BlockSpec((1, k), lambda i, j: (0, 0))
    w_spec = pl.BlockSpec((k, tn), lambda i, j: (0, j))
    o_spec = pl.BlockSpec((tm, tn), lambda i, j: (i, j))
    scratch = [pltpu.VMEM((tm, k), BF16)]
    if w2 is None:
        return pl.pallas_call(
            _norm_matmul_kernel, grid=grid, in_specs=[x_spec, g_spec, w_spec], out_specs=o_spec,
            out_shape=jax.ShapeDtypeStruct((m, n), BF16), scratch_shapes=scratch,
            compiler_params=_params("parallel", "arbitrary"), name=name)(x, g, w)
    n2 = w2.shape[1]
    return pl.pallas_call(
        _norm_matmul2_kernel, grid=grid,
        in_specs=[x_spec, g_spec, w_spec, pl.BlockSpec((k, n2), lambda i, j: (0, 0))],
        out_specs=[o_spec, pl.BlockSpec((tm, n2), lambda i, j: (i, 0))],
        out_shape=[jax.ShapeDtypeStruct((m, n), BF16), jax.ShapeDtypeStruct((m, n2), F32)],
        scratch_shapes=scratch, compiler_params=_params("parallel", "arbitrary"), name=name)(x, g, w, w2)


def _matmul_norm_residual_kernel(y_ref, w_ref, h_ref, g_ref, o_ref):
    k = pl.program_id(1)

    @pl.when(k == 0)
    def _():
        o_ref[...] = jnp.zeros_like(o_ref)

    o_ref[...] += jnp.dot(y_ref[...], w_ref[...], preferred_element_type=F32)

    @pl.when(k == pl.num_programs(1) - 1)
    def _():
        a = o_ref[...]
        o_ref[...] = h_ref[...] + a * _rms_scale(a) * g_ref[...]


def matmul_norm_residual(y, w, h, g, *, tm=1024, tk=1024, name):
    m, k = y.shape
    n = w.shape[1]
    tm, tk = min(tm, m), min(tk, k)
    return pl.pallas_call(
        _matmul_norm_residual_kernel, grid=(m // tm, k // tk),
        in_specs=[pl.BlockSpec((tm, tk), lambda i, kk: (i, kk)),
                  pl.BlockSpec((tk, n), lambda i, kk: (kk, 0)),
                  pl.BlockSpec((tm, n), lambda i, kk: (i, 0)),
                  pl.BlockSpec((1, n), lambda i, kk: (0, 0))],
        out_specs=pl.BlockSpec((tm, n), lambda i, kk: (i, 0)),
        out_shape=jax.ShapeDtypeStruct((m, n), F32),
        compiler_params=_params("parallel", "arbitrary"), name=name)(y, w, h, g)


def _ffn_kernel(h_ref, gpre_ref, wg_ref, wu_ref, wd_ref, gpost_ref, o_ref, xn_ref):
    f = pl.program_id(1)

    @pl.when(f == 0)
    def _():
        x = h_ref[...]
        xn_ref[...] = (x * _rms_scale(x) * gpre_ref[...]).astype(xn_ref.dtype)
        o_ref[...] = jnp.zeros_like(o_ref)

    xn = xn_ref[...]
    gate = jnp.dot(xn, wg_ref[...], preferred_element_type=F32)
    up = jnp.dot(xn, wu_ref[...], preferred_element_type=F32)
    act = (_silu(gate) * up).astype(BF16)
    o_ref[...] += jnp.dot(act, wd_ref[...], preferred_element_type=F32)

    @pl.when(f == pl.num_programs(1) - 1)
    def _():
        a = o_ref[...]
        o_ref[...] = h_ref[...] + a * _rms_scale(a) * gpost_ref[...]


def ffn(h, g_pre, w_gate, w_up, w_down, g_post, *, tm=1024, tf=512, name):
    m, d = h.shape
    dff = w_gate.shape[1]
    tm = min(tm, m)
    return pl.pallas_call(
        _ffn_kernel, grid=(m // tm, dff // tf),
        in_specs=[pl.BlockSpec((tm, d), lambda i, f: (i, 0)),
                  pl.BlockSpec((1, d), lambda i, f: (0, 0)),
                  pl.BlockSpec((d, tf), lambda i, f: (0, f)),
                  pl.BlockSpec((d, tf), lambda i, f: (0, f)),
                  pl.BlockSpec((tf, d), lambda i, f: (f, 0)),
                  pl.BlockSpec((1, d), lambda i, f: (0, 0))],
        out_specs=pl.BlockSpec((tm, d), lambda i, f: (i, 0)),
        out_shape=jax.ShapeDtypeStruct((m, d), F32),
        scratch_shapes=[pltpu.VMEM((tm, d), BF16)],
        compiler_params=_params("parallel", "arbitrary"), name=name)(h, g_pre, w_gate, w_up, w_down, g_post)


def _expand_heads(v, rows):
    lane = lax.broadcasted_iota(jnp.int32, (rows, LANES), 1)
    first = lane < SSM_HEAD_DIM
    pairs = [jnp.where(first, v[:, 2 * p:2 * p + 1], v[:, 2 * p + 1:2 * p + 2]) for p in range(HEADS_PER_GROUP // 2)]
    return jnp.concatenate(pairs, axis=1)


def _split_bf16(v):
    hi = v.astype(BF16)
    lo = (v - hi.astype(F32)).astype(BF16)
    return jnp.concatenate([hi, lo], axis=1)


def _ssd_kernel(z_ref, xs_ref, b_ref, c_ref, dt_ref, cw_ref, cb_ref, dtb_ref, alog_ref, dskip_ref, nw_ref,
                y_ref, state_ref, tail_ref, pad_ref, act_ref, ccol_ref, crowT_ref, e1_ref, wl_ref, elast_ref):
    c = pl.program_id(0)
    g = pl.program_id(1)
    rows = z_ref.shape[0]
    n_sub = rows // SSD_SUB
    sub_row = lax.broadcasted_iota(jnp.int32, (SSD_SUB, SSD_SUB), 0)
    sub_col = lax.broadcasted_iota(jnp.int32, (SSD_SUB, SSD_SUB), 1)
    causal = sub_row >= sub_col

    @pl.when(c == 0)
    def _():
        state_ref[g] = jnp.zeros((SSM_STATE, GROUP_INNER), F32)
        tail_ref[g] = jnp.zeros((SUBLANES, GROUP_CONV), F32)

    @pl.when(g == 0)
    def _():
        dtv = jax.nn.softplus(dt_ref[...] + dtb_ref[...])
        dta = dtv * (-jnp.exp(alog_ref[...]))
        log_dt = jnp.log(dtv)
        tri = causal.astype(F32)
        for s in range(n_sub):
            sl = slice(s * SSD_SUB, (s + 1) * SSD_SUB)
            cum = jnp.dot(tri, dta[sl], precision=lax.Precision.HIGHEST,
                          preferred_element_type=F32)
            clast = cum[SSD_SUB - 1:SSD_SUB, :]
            ccol_ref[sl, :] = cum * LOG2E
            crowT_ref[:, sl] = ((cum - log_dt[sl]) * LOG2E).T
            e1_ref[sl, :] = _split_bf16(jnp.exp(cum))
            wl_ref[sl, :] = _split_bf16(jnp.exp(clast - cum) * dtv[sl])
            elast_ref[s] = jnp.exp(clast)

    xcur = jnp.concatenate([xs_ref[...].astype(F32), b_ref[...].astype(F32), c_ref[...].astype(F32)], axis=1)
    cw = cw_ref[...]
    bias = cb_ref[...]
    pad_ref[0:SUBLANES, :] = tail_ref[g]
    pad_ref[SUBLANES:2 * SUBLANES, :] = xcur[0:SUBLANES]
    tail_ref[g] = xcur[rows - SUBLANES:rows]
    head = bias
    body = bias + cw[CONV_WIDTH - 1:CONV_WIDTH, :] * xcur
    for k in range(CONV_WIDTH):
        lo = SUBLANES - (CONV_WIDTH - 1) + k
        head = head + cw[k:k + 1, :] * pad_ref[lo:lo + SUBLANES, :]
        if k < CONV_WIDTH - 1:
            body = body + cw[k:k + 1, :] * pltpu.roll(xcur, CONV_WIDTH - 1 - k, 0)
    act_ref[...] = _silu(jnp.concatenate([head, body[SUBLANES:]], axis=0))

    shift = lax.rem(LANES - HEADS_PER_GROUP * g, LANES)
    r0 = pl.multiple_of(g * HEADS_PER_GROUP, HEADS_PER_GROUP)
    lane16 = sub_col.astype(F32).astype(BF16)
    low_half = lane16 < SSM_HEAD_DIM
    zero16 = jnp.zeros((SSD_SUB, LANES), BF16)
    sel_row = lax.broadcasted_iota(jnp.int32, (2 * LANES, GROUP_INNER), 0) & (LANES - 1)
    sel_head = lax.broadcasted_iota(jnp.int32, (2 * LANES, GROUP_INNER), 1) // SSM_HEAD_DIM
    expand = (sel_row == sel_head + HEADS_PER_GROUP * g).astype(F32).astype(BF16)

    def sub_chunk(s, carry):
        off = pl.multiple_of(s * SSD_SUB, SSD_SUB)
        rs = pl.ds(off, SSD_SUB)
        act = act_ref[rs, :]
        xg = act[:, 0:GROUP_INNER]
        xb = xg.astype(BF16)
        bb = act[:, GROUP_INNER:GROUP_INNER + SSM_STATE].astype(BF16)
        cc = act[:, GROUP_INNER + SSM_STATE:GROUP_CONV].astype(BF16)
        ccol = pltpu.roll(ccol_ref[rs, :], shift, 1)
        crow = crowT_ref[pl.ds(r0, HEADS_PER_GROUP), rs]
        cbm = lax.dot_general(cc, bb, (((1,), (1,)), ((), ())), preferred_element_type=F32).astype(BF16)

        ys = []
        for p in range(HEADS_PER_GROUP // 2):
            ms = []
            for r in (2 * p, 2 * p + 1):
                seg = ccol[:, r:r + 1] - crow[r:r + 1, :]
                ms.append(cbm * jnp.exp2(jnp.where(causal, seg, -jnp.inf)).astype(BF16))
            xp = xb[:, p * LANES:(p + 1) * LANES]
            bd = jnp.concatenate([jnp.where(low_half, xp, zero16), jnp.where(low_half, zero16, xp)], axis=0)
            ys.append(jnp.dot(jnp.concatenate(ms, axis=1), bd, preferred_element_type=F32))
        y = jnp.concatenate(ys, axis=1)

        st = state_ref[g]
        e1 = jnp.dot(e1_ref[rs, :], expand, preferred_element_type=F32)
        y = y + jnp.dot(cc, st.astype(BF16), preferred_element_type=F32) * e1
        wl = jnp.dot(wl_ref[rs, :], expand, preferred_element_type=F32)
        xw = (xg * wl).astype(BF16)
        ds = lax.dot_general(bb, xw, (((0,), (0,)), ((), ())), preferred_element_type=F32)
        el = pltpu.roll(elast_ref[s], shift, 1)
        state_ref[g] = st * _expand_heads(el, 1) + ds

        y = y + dskip_ref[...] * xg
        y = y * _silu(z_ref[rs, :].astype(F32))
        y_ref[rs, :] = (y * _rms_scale(y) * nw_ref[...]).astype(y_ref.dtype)
        return carry

    lax.fori_loop(0, n_sub, sub_chunk, 0, unroll=True)


def ssd_mixer(proj, dt_raw, conv_w_g, conv_b_g, dt_bias, a_log, d_skip, norm_w):
    seq = proj.shape[0]
    rows = min(SSD_ROWS, seq)
    n_sub = rows // SSD_SUB
    xcol0 = D_INNER // GROUP_INNER
    bcol0 = 2 * D_INNER // SSM_STATE
    ccol0 = bcol0 + GN // SSM_STATE
    return pl.pallas_call(
        _ssd_kernel, grid=(seq // rows, SSM_GROUPS),
        in_specs=[pl.BlockSpec((rows, GROUP_INNER), lambda c, g: (c, g)),
                  pl.BlockSpec((rows, GROUP_INNER), lambda c, g: (c, xcol0 + g)),
                  pl.BlockSpec((rows, SSM_STATE), lambda c, g: (c, bcol0 + g)),
                  pl.BlockSpec((rows, SSM_STATE), lambda c, g: (c, ccol0 + g)),
                  pl.BlockSpec((rows, LANES), lambda c, g: (c, 0)),
                  pl.BlockSpec((None, CONV_WIDTH, GROUP_CONV), lambda c, g: (g, 0, 0)),
                  pl.BlockSpec((None, 1, GROUP_CONV), lambda c, g: (g, 0, 0)),
                  pl.BlockSpec((1, LANES), lambda c, g: (0, 0)),
                  pl.BlockSpec((1, LANES), lambda c, g: (0, 0)),
                  pl.BlockSpec((1, GROUP_INNER), lambda c, g: (0, g)),
                  pl.BlockSpec((1, GROUP_INNER), lambda c, g: (0, g))],
        out_specs=pl.BlockSpec((rows, GROUP_INNER), lambda c, g: (c, g)),
        out_shape=jax.ShapeDtypeStruct((seq, D_INNER), BF16),
        scratch_shapes=[pltpu.VMEM((SSM_GROUPS, SSM_STATE, GROUP_INNER), F32),
                        pltpu.VMEM((SSM_GROUPS, SUBLANES, GROUP_CONV), F32),
                        pltpu.VMEM((2 * SUBLANES, GROUP_CONV), F32),
                        pltpu.VMEM((rows, GROUP_CONV), F32),
                        pltpu.VMEM((rows, LANES), F32),
                        pltpu.VMEM((LANES, rows), F32),
                        pltpu.VMEM((rows, 2 * LANES), BF16),
                        pltpu.VMEM((rows, 2 * LANES), BF16),
                        pltpu.VMEM((n_sub, 1, LANES), F32)],
        compiler_params=_params("arbitrary", "arbitrary"), name="ssd_mixer",
    )(proj, proj, proj, proj, dt_raw, conv_w_g, conv_b_g, dt_bias, a_log, d_skip, norm_w)


def _attn_kernel(sink_ref, qt_ref, kvp_ref, kvc_ref, o_ref, bias_ref):
    i = pl.program_id(0)
    group_lanes = Q_PER_KV * BLOCK

    @pl.when(i == 0)
    def _():
        sj = lax.broadcasted_iota(jnp.int32, (2 * BLOCK, group_lanes), 0)
        lane = lax.broadcasted_iota(jnp.int32, (2 * BLOCK, group_lanes), 1)
        dist_i = BLOCK + (lane & (BLOCK - 1)) - sj
        valid = (dist_i >= 0) & (dist_i < WINDOW)
        dist = dist_i.astype(F32)
        head_in_group = (lane // BLOCK).astype(F32)
        for kh in range(N_KV_HEADS):
            slope = jnp.exp2((head_in_group + (kh * Q_PER_KV + 1)) * (-8.0 / N_Q_HEADS))
            b = jnp.where(valid, -(slope * dist), -jnp.inf)
            bias_ref[1, kh] = b
            bias_ref[0, kh] = jnp.where(sj >= BLOCK, b, -jnp.inf)

    has_prev = jnp.minimum(i, 1)
    scale = HEAD_DIM ** -0.5
    kv = jnp.concatenate([kvp_ref[...], kvc_ref[...]], axis=0)
    for kh in range(N_KV_HEADS):
        k = kv[:, kh * HEAD_DIM:(kh + 1) * HEAD_DIM] * scale
        v = kv[:, KV_DIM + kh * HEAD_DIM:KV_DIM + (kh + 1) * HEAD_DIM]
        h0 = kh * Q_PER_KV
        qg = jnp.concatenate([qt_ref[(h0 + r) * HEAD_DIM:(h0 + r + 1) * HEAD_DIM, :] for r in range(Q_PER_KV)],
                             axis=1)
        sink = sink_ref[kh]
        logits = jnp.dot(k, qg, preferred_element_type=F32) + bias_ref[has_prev, kh]
        m = jnp.maximum(jnp.max(logits, axis=0, keepdims=True), sink)
        p = jnp.exp(logits - m)
        denom = jnp.sum(p, axis=0, keepdims=True) + jnp.exp(sink - m)
        ot = lax.dot_general(v, p.astype(BF16), (((0,), (0,)), ((), ())), preferred_element_type=F32) / denom
        for r in range(0, Q_PER_KV, 2):
            pair = jnp.concatenate([ot[:, r * BLOCK:(r + 1) * BLOCK], ot[:, (r + 1) * BLOCK:(r + 2) * BLOCK]], axis=0)
            o_ref[:, (h0 + r) * HEAD_DIM:(h0 + r + 2) * HEAD_DIM] = pair.T.astype(o_ref.dtype)


def swa_attention(qt, kv, sinks):
    seq = kv.shape[0]
    nb = seq // BLOCK
    sink_rows = jnp.repeat(sinks.astype(F32), BLOCK).reshape(N_KV_HEADS, 1, Q_PER_KV * BLOCK)
    return pl.pallas_call(
        _attn_kernel, grid=(nb,),
        in_specs=[pl.BlockSpec((N_KV_HEADS, 1, Q_PER_KV * BLOCK), lambda i: (0, 0, 0)),
                  pl.BlockSpec((N_Q_HEADS * HEAD_DIM, BLOCK), lambda i: (0, i)),
                  pl.BlockSpec((BLOCK, 2 * KV_DIM), lambda i: (jnp.maximum(i - 1, 0), 0)),
                  pl.BlockSpec((BLOCK, 2 * KV_DIM), lambda i: (i, 0))],
        out_specs=pl.BlockSpec((BLOCK, N_Q_HEADS * HEAD_DIM), lambda i: (i, 0)),
        out_shape=jax.ShapeDtypeStruct((seq, N_Q_HEADS * HEAD_DIM), BF16),
        scratch_shapes=[pltpu.VMEM((2, N_KV_HEADS, 2 * BLOCK, Q_PER_KV * BLOCK), F32)],
        compiler_params=_params("arbitrary"), name="swa_attention",
    )(sink_rows, qt, kv, kv)


def _pad_lanes(v, width=LANES):
    v = v.reshape(1, -1).astype(F32)
    return jnp.pad(v, ((0, 0), (0, width - v.shape[1])))


def kernel(x, norm_w, ssm_w_in, ssm_conv_w, ssm_conv_b, ssm_dt_bias, ssm_A_log, ssm_D, ssm_norm_w, ssm_w_out,
           kv_norm_w, w_kv, attn_w_q, attn_sinks, attn_w_o, ffn_w_gate, ffn_w_up, ffn_w_down):
    batch, seq, d = x.shape
    assert batch == 1 and d == D_MODEL and seq % SSD_ROWS == 0
    h = x.reshape(seq, d)
    row = lambda v: v.reshape(1, -1).astype(F32)

    g = norm_w[0]
    w_in = ssm_w_in[0]
    n_main = 2 * D_INNER + 2 * GN
    w_main = w_in[:, :n_main].astype(BF16)
    w_dt = jnp.pad(w_in[:, n_main:], ((0, 0), (0, LANES - SSM_HEADS))).astype(BF16)
    proj, dt_raw = norm_matmul(h, row(g[0]), w_main, w2=w_dt, name="in_proj")

    def per_group(t):
        xs = t[:, :D_INNER].reshape(-1, SSM_GROUPS, GROUP_INNER)
        bs = t[:, D_INNER:D_INNER + GN].reshape(-1, SSM_GROUPS, SSM_STATE)
        cs = t[:, D_INNER + GN:].reshape(-1, SSM_GROUPS, SSM_STATE)
        return jnp.transpose(jnp.concatenate([xs, bs, cs], axis=-1), (1, 0, 2)).astype(F32)

    y = ssd_mixer(proj, dt_raw, per_group(ssm_conv_w[0]), per_group(ssm_conv_b[0].reshape(1, -1)),
                  _pad_lanes(ssm_dt_bias[0]), _pad_lanes(ssm_A_log[0]),
                  row(jnp.repeat(ssm_D[0], SSM_HEAD_DIM)), row(ssm_norm_w[0]))
    h = matmul_norm_residual(y, ssm_w_out[0].astype(BF16), h, row(g[1]), name="ssm_out_proj")
    h = ffn(h, row(g[2]), ffn_w_gate[0].astype(BF16), ffn_w_up[0].astype(BF16), ffn_w_down[0].astype(BF16),
            row(g[3]), name="ffn0")

    g = norm_w[1]
    kv = norm_matmul(h, row(kv_norm_w), w_kv.astype(BF16), name="kv_proj")
    qt = norm_matmul_t(h, row(g[0]), attn_w_q[0].astype(BF16), name="q_proj")
    a = swa_attention(qt, kv, attn_sinks[0])
    h = matmul_norm_residual(a, attn_w_o[0].astype(BF16), h, row(g[1]), name="attn_out_proj")
    h = ffn(h, row(g[2]), ffn_w_gate[1].astype(BF16), ffn_w_up[1].astype(BF16), ffn_w_down[1].astype(BF16),
            row(g[3]), name="ffn1")
    return h.reshape(batch, seq, d)
```

```python
import math

import jax
import jax.numpy as jnp
from jax import lax
from jax.experimental import pallas as pl
from jax.experimental.pallas import tpu as pltpu

D_MODEL = 2048
D_INNER = 4096
SSM_HEAD_DIM = 64
SSM_HEADS = 64
SSM_GROUPS = 8
HEADS_PER_GROUP = SSM_HEADS // SSM_GROUPS
SSM_STATE = 128
CONV_WIDTH = 4
GN = SSM_GROUPS * SSM_STATE
GROUP_INNER = D_INNER // SSM_GROUPS
GROUP_CONV = GROUP_INNER + 2 * SSM_STATE
SSD_SUB = 128
SSD_ROWS = 512
N_Q_HEADS = 32
N_KV_HEADS = 4
Q_PER_KV = N_Q_HEADS // N_KV_HEADS
HEAD_DIM = 64
WINDOW = 128
BLOCK = 128
KV_DIM = N_KV_HEADS * HEAD_DIM
D_FF = 5632
EPS = 1e-6
LOG2E = math.log2(math.e)

LANES = 128
SUBLANES = 8
VMEM_LIMIT_BYTES = 63 * 1024 * 1024

BF16 = jnp.bfloat16
F32 = jnp.float32


def _rms_scale(x):
    return lax.rsqrt(jnp.mean(x * x, axis=-1, keepdims=True) + EPS)


def _silu(x):
    return x * jax.nn.sigmoid(x)


def _params(*semantics):
    return pltpu.CompilerParams(dimension_semantics=semantics, vmem_limit_bytes=VMEM_LIMIT_BYTES)


def _norm_matmul_kernel(x_ref, g_ref, w_ref, o_ref, xn_ref):
    @pl.when(pl.program_id(1) == 0)
    def _():
        x = x_ref[...]
        xn_ref[...] = (x * _rms_scale(x) * g_ref[...]).astype(xn_ref.dtype)

    o_ref[...] = jnp.dot(xn_ref[...], w_ref[...], preferred_element_type=F32).astype(o_ref.dtype)


def _norm_matmul2_kernel(x_ref, g_ref, w_ref, w2_ref, o_ref, o2_ref, xn_ref):
    @pl.when(pl.program_id(1) == 0)
    def _():
        x = x_ref[...]
        xn = (x * _rms_scale(x) * g_ref[...]).astype(xn_ref.dtype)
        xn_ref[...] = xn
        o2_ref[...] = jnp.dot(xn, w2_ref[...], preferred_element_type=F32)

    o_ref[...] = jnp.dot(xn_ref[...], w_ref[...], preferred_element_type=F32).astype(o_ref.dtype)


def _norm_matmul_t_kernel(x_ref, g_ref, w_ref, o_ref, xn_ref):
    @pl.when(pl.program_id(1) == 0)
    def _():
        x = x_ref[...]
        xn_ref[...] = (x * _rms_scale(x) * g_ref[...]).astype(xn_ref.dtype)

    o_ref[...] = lax.dot_general(w_ref[...], xn_ref[...], (((0,), (1,)), ((), ())),
                                 preferred_element_type=F32).astype(o_ref.dtype)


def norm_matmul_t(x, g, w, *, tm=1024, tn=1024, name):
    m, k = x.shape
    n = w.shape[1]
    tm, tn = min(tm, m), min(tn, n)
    return pl.pallas_call(
        _norm_matmul_t_kernel, grid=(m // tm, n // tn),
        in_specs=[pl.BlockSpec((tm, k), lambda i, j: (i, 0)),
                  pl.BlockSpec((1, k), lambda i, j: (0, 0)),
                  pl.BlockSpec((k, tn), lambda i, j: (0, j))],
        out_specs=pl.BlockSpec((tn, tm), lambda i, j: (j, i)),
        out_shape=jax.ShapeDtypeStruct((n, m), BF16), scratch_shapes=[pltpu.VMEM((tm, k), BF16)],
        compiler_params=_params("parallel", "arbitrary"), name=name)(x, g, w)


def norm_matmul(x, g, w, *, n=None, w2=None, tm=1024, tn=1024, name):
    m, k = x.shape
    n = w.shape[1] if n is None else n
    tm, tn = min(tm, m), min(tn, n)
    assert n % tn == 0
    grid = (m // tm, n // tn)
    x_spec = pl.BlockSpec((tm, k), lambda i, j: (i, 0))
    g_spec = pl.BlockSpec((1, k), lambda i, j: (0, 0))
    w_spec = pl.BlockSpec((k, tn), lambda i, j: (0, j))
    o_spec = pl.BlockSpec((tm, tn), lambda i, j: (i, j))
    scratch = [pltpu.VMEM((tm, k), BF16)]
    if w2 is None:
        return pl.pallas_call(
            _norm_matmul_kernel, grid=grid, in_specs=[x_spec, g_spec, w_spec], out_specs=o_spec,
            out_shape=jax.ShapeDtypeStruct((m, n), BF16), scratch_shapes=scratch,
            compiler_params=_params("parallel", "arbitrary"), name=name)(x, g, w)
    n2 = w2.shape[1]
    return pl.pallas_call(
        _norm_matmul2_kernel, grid=grid,
        in_specs=[x_spec, g_spec, w_spec, pl.BlockSpec((k, n2), lambda i, j: (0, 0))],
        out_specs=[o_spec, pl.BlockSpec((tm, n2), lambda i, j: (i, 0))],
        out_shape=[jax.ShapeDtypeStruct((m, n), BF16), jax.ShapeDtypeStruct((m, n2), F32)],
        scratch_shapes=scratch, compiler_params=_params("parallel", "arbitrary"), name=name)(x, g, w, w2)


def _matmul_norm_residual_kernel(y_ref, w_ref, h_ref, g_ref, o_ref):
    k = pl.program_id(1)

    @pl.when(k == 0)
    def _():
        o_ref[...] = jnp.zeros_like(o_ref)

    o_ref[...] += jnp.dot(y_ref[...], w_ref[...], preferred_element_type=F32)

    @pl.when(k == pl.num_programs(1) - 1)
    def _():
        a = o_ref[...]
        o_ref[...] = h_ref[...] + a * _rms_scale(a) * g_ref[...]


def matmul_norm_residual(y, w, h, g, *, tm=1024, tk=1024, name):
    m, k = y.shape
    n = w.shape[1]
    tm, tk = min(tm, m), min(tk, k)
    return pl.pallas_call(
        _matmul_norm_residual_kernel, grid=(m // tm, k // tk),
        in_specs=[pl.BlockSpec((tm, tk), lambda i, kk: (i, kk)),
                  pl.BlockSpec((tk, n), lambda i, kk: (kk, 0)),
                  pl.BlockSpec((tm, n), lambda i, kk: (i, 0)),
                  pl.BlockSpec((1, n), lambda i, kk: (0, 0))],
        out_specs=pl.BlockSpec((tm, n), lambda i, kk: (i, 0)),
        out_shape=jax.ShapeDtypeStruct((m, n), F32),
        compiler_params=_params("parallel", "arbitrary"), name=name)(y, w, h, g)


def _ffn_kernel(h_ref, gpre_ref, wg_ref, wu_ref, wd_ref, gpost_ref, o_ref, xn_ref):
    f = pl.program_id(1)

    @pl.when(f == 0)
    def _():
        x = h_ref[...]
        xn_ref[...] = (x * _rms_scale(x) * gpre_ref[...]).astype(xn_ref.dtype)
        o_ref[...] = jnp.zeros_like(o_ref)

    xn = xn_ref[...]
    gate = jnp.dot(xn, wg_ref[...], preferred_element_type=F32)
    up = jnp.dot(xn, wu_ref[...], preferred_element_type=F32)
    act = (_silu(gate) * up).astype(BF16)
    o_ref[...] += jnp.dot(act, wd_ref[...], preferred_element_type=F32)

    @pl.when(f == pl.num_programs(1) - 1)
    def _():
        a = o_ref[...]
        o_ref[...] = h_ref[...] + a * _rms_scale(a) * gpost_ref[...]


def ffn(h, g_pre, w_gate, w_up, w_down, g_post, layer, *, tm=1024, tf=512, name):
    m, d = h.shape
    dff = w_gate.shape[2]
    tm = min(tm, m)
    return pl.pallas_call(
        _ffn_kernel, grid=(m // tm, dff // tf),
        in_specs=[pl.BlockSpec((tm, d), lambda i, f: (i, 0)),
                  pl.BlockSpec((1, d), lambda i, f: (0, 0)),
                  pl.BlockSpec((None, d, tf), lambda i, f: (layer, 0, f)),
                  pl.BlockSpec((None, d, tf), lambda i, f: (layer, 0, f)),
                  pl.BlockSpec((None, tf, d), lambda i, f: (layer, f, 0)),
                  pl.BlockSpec((1, d), lambda i, f: (0, 0))],
        out_specs=pl.BlockSpec((tm, d), lambda i, f: (i, 0)),
        out_shape=jax.ShapeDtypeStruct((m, d), F32),
        scratch_shapes=[pltpu.VMEM((tm, d), BF16)],
        compiler_params=_params("parallel", "arbitrary"), name=name)(h, g_pre, w_gate, w_up, w_down, g_post)


def _expand_heads(v, rows):
    lane = lax.broadcasted_iota(jnp.int32, (rows, LANES), 1)
    first = lane < SSM_HEAD_DIM
    pairs = [jnp.where(first, v[:, 2 * p:2 * p + 1], v[:, 2 * p + 1:2 * p + 2]) for p in range(HEADS_PER_GROUP // 2)]
    return jnp.concatenate(pairs, axis=1)


def _split_bf16(v):
    hi = v.astype(BF16)
    lo = (v - hi.astype(F32)).astype(BF16)
    return jnp.concatenate([hi, lo], axis=1)


def _ssd_kernel(z_ref, xs_ref, b_ref, c_ref, dt_ref, cw_ref, cb_ref, dtb_ref, alog_ref, dskip_ref, nw_ref,
                y_ref, state_ref, tail_ref, pad_ref, act_ref, ccol_ref, crowT_ref, e1_ref, wl_ref, elast_ref):
    c = pl.program_id(0)
    g = pl.program_id(1)
    rows = z_ref.shape[0]
    n_sub = rows // SSD_SUB
    sub_row = lax.broadcasted_iota(jnp.int32, (SSD_SUB, SSD_SUB), 0)
    sub_col = lax.broadcasted_iota(jnp.int32, (SSD_SUB, SSD_SUB), 1)
    causal = sub_row >= sub_col

    @pl.when(c == 0)
    def _():
        state_ref[g] = jnp.zeros((SSM_STATE, GROUP_INNER), F32)
        tail_ref[g] = jnp.zeros((SUBLANES, GROUP_CONV), F32)

    @pl.when(g == 0)
    def _():
        dtv = jax.nn.softplus(dt_ref[...] + dtb_ref[...])
        dta = dtv * (-jnp.exp(alog_ref[...]))
        log_dt = jnp.log(dtv)
        tri = causal.astype(F32)
        for s in range(n_sub):
            sl = slice(s * SSD_SUB, (s + 1) * SSD_SUB)
            cum = jnp.dot(tri, dta[sl], precision=lax.Precision.HIGHEST,
                          preferred_element_type=F32)
            clast = cum[SSD_SUB - 1:SSD_SUB, :]
            ccol_ref[sl, :] = cum * LOG2E
            crowT_ref[:, sl] = ((cum - log_dt[sl]) * LOG2E).T
            e1_ref[sl, :] = _split_bf16(jnp.exp(cum))
            wl_ref[sl, :] = _split_bf16(jnp.exp(clast - cum) * dtv[sl])
            elast_ref[s] = jnp.exp(clast)

    xcur = jnp.concatenate([xs_ref[...].astype(F32), b_ref[...].astype(F32), c_ref[...].astype(F32)], axis=1)
    cw = cw_ref[...]
    bias = cb_ref[...]
    pad_ref[0:SUBLANES, :] = tail_ref[g]
    pad_ref[SUBLANES:2 * SUBLANES, :] = xcur[0:SUBLANES]
    tail_ref[g] = xcur[rows - SUBLANES:rows]
    head = bias
    body = bias + cw[CONV_WIDTH - 1:CONV_WIDTH, :] * xcur
    for k in range(CONV_WIDTH):
        lo = SUBLANES - (CONV_WIDTH - 1) + k
        head = head + cw[k:k + 1, :] * pad_ref[lo:lo + SUBLANES, :]
        if k < CONV_WIDTH - 1:
            body = body + cw[k:k + 1, :] * pltpu.roll(xcur, CONV_WIDTH - 1 - k, 0)
    act_ref[...] = _silu(jnp.concatenate([head, body[SUBLANES:]], axis=0))

    shift = lax.rem(LANES - HEADS_PER_GROUP * g, LANES)
    r0 = pl.multiple_of(g * HEADS_PER_GROUP, HEADS_PER_GROUP)
    lane16 = sub_col.astype(F32).astype(BF16)
    low_half = lane16 < SSM_HEAD_DIM
    zero16 = jnp.zeros((SSD_SUB, LANES), BF16)
    sel_row = lax.broadcasted_iota(jnp.int32, (2 * LANES, GROUP_INNER), 0) & (LANES - 1)
    sel_head = lax.broadcasted_iota(jnp.int32, (2 * LANES, GROUP_INNER), 1) // SSM_HEAD_DIM
    expand = (sel_row == sel_head + HEADS_PER_GROUP * g).astype(F32).astype(BF16)

    for s in range(n_sub):
        rs = slice(s * SSD_SUB, (s + 1) * SSD_SUB)
        act = act_ref[rs, :]
        xg = act[:, 0:GROUP_INNER]
        xb = xg.astype(BF16)
        bb = act[:, GROUP_INNER:GROUP_INNER + SSM_STATE].astype(BF16)
        cc = act[:, GROUP_INNER + SSM_STATE:GROUP_CONV].astype(BF16)
        ccol = pltpu.roll(ccol_ref[rs, :], shift, 1)
        crow = crowT_ref[pl.ds(r0, HEADS_PER_GROUP), rs]
        cbm = lax.dot_general(cc, bb, (((1,), (1,)), ((), ())), preferred_element_type=F32).astype(BF16)

        ys = []
        for p in range(HEADS_PER_GROUP // 2):
            ms = []
            for r in (2 * p, 2 * p + 1):
                seg = ccol[:, r:r + 1] - crow[r:r + 1, :]
                ms.append(cbm * jnp.exp2(jnp.where(causal, seg, -jnp.inf)).astype(BF16))
            xp = xb[:, p * LANES:(p + 1) * LANES]
            bd = jnp.concatenate([jnp.where(low_half, xp, zero16), jnp.where(low_half, zero16, xp)], axis=0)
            ys.append(jnp.dot(jnp.concatenate(ms, axis=1), bd, preferred_element_type=F32))
        y = jnp.concatenate(ys, axis=1)

        st = state_ref[g]
        e1 = jnp.dot(e1_ref[rs, :], expand, preferred_element_type=F32)
        y = y + jnp.dot(cc, st.astype(BF16), preferred_element_type=F32) * e1
        wl = jnp.dot(wl_ref[rs, :], expand, preferred_element_type=F32)
        xw = (xg * wl).astype(BF16)
        ds = lax.dot_general(bb, xw, (((0,), (0,)), ((), ())), preferred_element_type=F32)
        el = pltpu.roll(elast_ref[s], shift, 1)
        state_ref[g] = st * _expand_heads(el, 1) + ds

        y = y + dskip_ref[...] * xg
        y = y * _silu(z_ref[rs, :].astype(F32))
        y_ref[rs, :] = (y * _rms_scale(y) * nw_ref[...]).astype(y_ref.dtype)


def ssd_mixer(proj, dt_raw, conv_w_g, conv_b_g, dt_bias, a_log, d_skip, norm_w):
    seq = proj.shape[0]
    rows = min(SSD_ROWS, seq)
    n_sub = rows // SSD_SUB
    xcol0 = D_INNER // GROUP_INNER
    bcol0 = 2 * D_INNER // SSM_STATE
    ccol0 = bcol0 + GN // SSM_STATE
    return pl.pallas_call(
        _ssd_kernel, grid=(seq // rows, SSM_GROUPS),
        in_specs=[pl.BlockSpec((rows, GROUP_INNER), lambda c, g: (c, g)),
                  pl.BlockSpec((rows, GROUP_INNER), lambda c, g: (c, xcol0 + g)),
                  pl.BlockSpec((rows, SSM_STATE), lambda c, g: (c, bcol0 + g)),
                  pl.BlockSpec((rows, SSM_STATE), lambda c, g: (c, ccol0 + g)),
                  pl.BlockSpec((rows, LANES), lambda c, g: (c, 0)),
                  pl.BlockSpec((None, CONV_WIDTH, GROUP_CONV), lambda c, g: (g, 0, 0)),
                  pl.BlockSpec((None, 1, GROUP_CONV), lambda c, g: (g, 0, 0)),
                  pl.BlockSpec((1, LANES), lambda c, g: (0, 0)),
                  pl.BlockSpec((1, LANES), lambda c, g: (0, 0)),
                  pl.BlockSpec((1, GROUP_INNER), lambda c, g: (0, g)),
                  pl.BlockSpec((1, GROUP_INNER), lambda c, g: (0, g))],
        out_specs=pl.BlockSpec((rows, GROUP_INNER), lambda c, g: (c, g)),
        out_shape=jax.ShapeDtypeStruct((seq, D_INNER), BF16),
        scratch_shapes=[pltpu.VMEM((SSM_GROUPS, SSM_STATE, GROUP_INNER), F32),
                        pltpu.VMEM((SSM_GROUPS, SUBLANES, GROUP_CONV), F32),
                        pltpu.VMEM((2 * SUBLANES, GROUP_CONV), F32),
                        pltpu.VMEM((rows, GROUP_CONV), F32),
                        pltpu.VMEM((rows, LANES), F32),
                        pltpu.VMEM((LANES, rows), F32),
                        pltpu.VMEM((rows, 2 * LANES), BF16),
                        pltpu.VMEM((rows, 2 * LANES), BF16),
                        pltpu.VMEM((n_sub, 1, LANES), F32)],
        compiler_params=_params("arbitrary", "arbitrary"), name="ssd_mixer",
    )(proj, proj, proj, proj, dt_raw, conv_w_g, conv_b_g, dt_bias, a_log, d_skip, norm_w)


def _attn_kernel(sink_ref, qt_ref, kvp_ref, kvc_ref, o_ref, bias_ref):
    i = pl.program_id(0)
    group_lanes = Q_PER_KV * BLOCK

    @pl.when(i == 0)
    def _():
        sj = lax.broadcasted_iota(jnp.int32, (2 * BLOCK, group_lanes), 0)
        lane = lax.broadcasted_iota(jnp.int32, (2 * BLOCK, group_lanes), 1)
        dist_i = BLOCK + (lane & (BLOCK - 1)) - sj
        valid = (dist_i >= 0) & (dist_i < WINDOW)
        dist = dist_i.astype(F32)
        head_in_group = (lane // BLOCK).astype(F32)
        for kh in range(N_KV_HEADS):
            slope = jnp.exp2((head_in_group + (kh * Q_PER_KV + 1)) * (-8.0 / N_Q_HEADS))
            b = jnp.where(valid, -(slope * dist), -jnp.inf)
            bias_ref[1, kh] = b
            bias_ref[0, kh] = jnp.where(sj >= BLOCK, b, -jnp.inf)

    has_prev = jnp.minimum(i, 1)
    scale = HEAD_DIM ** -0.5
    kv = jnp.concatenate([kvp_ref[...], kvc_ref[...]], axis=0)
    for kh in range(N_KV_HEADS):
        k = kv[:, kh * HEAD_DIM:(kh + 1) * HEAD_DIM] * scale
        v = kv[:, KV_DIM + kh * HEAD_DIM:KV_DIM + (kh + 1) * HEAD_DIM]
        h0 = kh * Q_PER_KV
        qg = jnp.concatenate([qt_ref[(h0 + r) * HEAD_DIM:(h0 + r + 1) * HEAD_DIM, :] for r in range(Q_PER_KV)],
                             axis=1)
        sink = sink_ref[kh]
        logits = jnp.dot(k, qg, preferred_element_type=F32) + bias_ref[has_prev, kh]
        m = jnp.maximum(jnp.max(logits, axis=0, keepdims=True), sink)
        p = jnp.exp(logits - m)
        denom = jnp.sum(p, axis=0, keepdims=True) + jnp.exp(sink - m)
        ot = lax.dot_general(v, p.astype(BF16), (((0,), (0,)), ((), ())), preferred_element_type=F32) / denom
        for r in range(0, Q_PER_KV, 2):
            pair = jnp.concatenate([ot[:, r * BLOCK:(r + 1) * BLOCK], ot[:, (r + 1) * BLOCK:(r + 2) * BLOCK]], axis=0)
            o_ref[:, (h0 + r) * HEAD_DIM:(h0 + r + 2) * HEAD_DIM] = pair.T.astype(o_ref.dtype)


def swa_attention(qt, kv, sinks):
    seq = kv.shape[0]
    nb = seq // BLOCK
    sink_rows = jnp.repeat(sinks.astype(F32), BLOCK).reshape(N_KV_HEADS, 1, Q_PER_KV * BLOCK)
    return pl.pallas_call(
        _attn_kernel, grid=(nb,),
        in_specs=[pl.BlockSpec((N_KV_HEADS, 1, Q_PER_KV * BLOCK), lambda i: (0, 0, 0)),
                  pl.BlockSpec((N_Q_HEADS * HEAD_DIM, BLOCK), lambda i: (0, i)),
                  pl.BlockSpec((BLOCK, 2 * KV_DIM), lambda i: (jnp.maximum(i - 1, 0), 0)),
                  pl.BlockSpec((BLOCK, 2 * KV_DIM), lambda i: (i, 0))],
        out_specs=pl.BlockSpec((BLOCK, N_Q_HEADS * HEAD_DIM), lambda i: (i, 0)),
        out_shape=jax.ShapeDtypeStruct((seq, N_Q_HEADS * HEAD_DIM), BF16),
        scratch_shapes=[pltpu.VMEM((2, N_KV_HEADS, 2 * BLOCK, Q_PER_KV * BLOCK), F32)],
        compiler_params=_params("arbitrary"), name="swa_attention",
    )(sink_rows, qt, kv, kv)


def _pad_lanes(v, width=LANES):
    v = v.reshape(1, -1).astype(F32)
    return jnp.pad(v, ((0, 0), (0, width - v.shape[1])))


def kernel(x, norm_w, ssm_w_in, ssm_conv_w, ssm_conv_b, ssm_dt_bias, ssm_A_log, ssm_D, ssm_norm_w, ssm_w_out,
           kv_norm_w, w_kv, attn_w_q, attn_sinks, attn_w_o, ffn_w_gate, ffn_w_up, ffn_w_down):
    batch, seq, d = x.shape
    assert batch == 1 and d == D_MODEL and seq % SSD_ROWS == 0
    h = x.reshape(seq, d)
    row = lambda v: v.reshape(1, -1).astype(F32)

    w_gate, w_up, w_down = ffn_w_gate.astype(BF16), ffn_w_up.astype(BF16), ffn_w_down.astype(BF16)

    g = norm_w[0]
    w_in = ssm_w_in[0].astype(BF16)
    n_main = 2 * D_INNER + 2 * GN
    w_dt = jnp.pad(w_in[:, n_main:], ((0, 0), (0, LANES - SSM_HEADS)))
    proj, dt_raw = norm_matmul(h, row(g[0]), w_in, n=n_main, w2=w_dt, name="in_proj")

    def per_group(t):
        xs = t[:, :D_INNER].reshape(-1, SSM_GROUPS, GROUP_INNER)
        bs = t[:, D_INNER:D_INNER + GN].reshape(-1, SSM_GROUPS, SSM_STATE)
        cs = t[:, D_INNER + GN:].reshape(-1, SSM_GROUPS, SSM_STATE)
        return jnp.transpose(jnp.concatenate([xs, bs, cs], axis=-1), (1, 0, 2)).astype(F32)

    y = ssd_mixer(proj, dt_raw, per_group(ssm_conv_w[0]), per_group(ssm_conv_b[0].reshape(1, -1)),
                  _pad_lanes(ssm_dt_bias[0]), _pad_lanes(ssm_A_log[0]),
                  row(jnp.repeat(ssm_D[0], SSM_HEAD_DIM)), row(ssm_norm_w[0]))
    h = matmul_norm_residual(y, ssm_w_out[0].astype(BF16), h, row(g[1]), name="ssm_out_proj")
    h = ffn(h, row(g[2]), w_gate, w_up, w_down, row(g[3]), 0, name="ffn0")

    g = norm_w[1]
    kv = norm_matmul(h, row(kv_norm_w), w_kv.astype(BF16), name="kv_proj")
    qt = norm_matmul_t(h, row(g[0]), attn_w_q[0].astype(BF16), name="q_proj")
    a = swa_attention(qt, kv, attn_sinks[0])
    h = matmul_norm_residual(a, attn_w_o[0].astype(BF16), h, row(g[1]), name="attn_out_proj")
    h = ffn(h, row(g[2]), w_gate, w_up, w_down, row(g[3]), 1, name="ffn1")
    return h.reshape(batch, seq, d)
```

```python
import math

import jax
import jax.numpy as jnp
from jax import lax
from jax.experimental import pallas as pl
from jax.experimental.pallas import tpu as pltpu

D_MODEL = 2048
D_INNER = 4096
SSM_HEAD_DIM = 64
SSM_HEADS = 64
SSM_GROUPS = 8
HEADS_PER_GROUP = SSM_HEADS // SSM_GROUPS
SSM_STATE = 128
CONV_WIDTH = 4
GN = SSM_GROUPS * SSM_STATE
GROUP_INNER = D_INNER // SSM_GROUPS
GROUP_CONV = GROUP_INNER + 2 * SSM_STATE
SSD_SUB = 128
SSD_ROWS = 512
N_Q_HEADS = 32
N_KV_HEADS = 4
Q_PER_KV = N_Q_HEADS // N_KV_HEADS
HEAD_DIM = 64
WINDOW = 128
BLOCK = 128
KV_DIM = N_KV_HEADS * HEAD_DIM
D_FF = 5632
EPS = 1e-6
LOG2E = math.log2(math.e)

LANES = 128
SUBLANES = 8
VMEM_LIMIT_BYTES = 63 * 1024 * 1024

BF16 = jnp.bfloat16
F32 = jnp.float32


def _rms_scale(x):
    return lax.rsqrt(jnp.mean(x * x, axis=-1, keepdims=True) + EPS)


def _silu(x):
    return x * jax.nn.sigmoid(x)


def _params(*semantics):
    return pltpu.CompilerParams(dimension_semantics=semantics, vmem_limit_bytes=VMEM_LIMIT_BYTES)


def _normed_rows(x_ref, g_ref, xn_ref, first):
    if not first:
        return xn_ref[...]
    x = x_ref[...]
    xn = (x * _rms_scale(x) * g_ref[...]).astype(xn_ref.dtype)
    xn_ref[...] = xn
    return xn


def _first_and_rest(step):
    j = pl.program_id(1)
    pl.when(j == 0)(lambda: step(True))
    pl.when(j > 0)(lambda: step(False))


def _first_middle_last(step):
    j = pl.program_id(1)
    last = pl.num_programs(1) - 1
    pl.when(j == 0)(lambda: step(True, False))
    pl.when(jnp.logical_and(j > 0, j < last))(lambda: step(False, False))
    pl.when(j == last)(lambda: step(False, True))


def _norm_matmul_kernel(x_ref, g_ref, w_ref, o_ref, xn_ref):
    def step(first):
        xn = _normed_rows(x_ref, g_ref, xn_ref, first)
        o_ref[...] = jnp.dot(xn, w_ref[...], preferred_element_type=F32).astype(o_ref.dtype)

    _first_and_rest(step)


def _norm_matmul2_kernel(x_ref, g_ref, w_ref, w2_ref, o_ref, o2_ref, xn_ref):
    def step(first):
        xn = _normed_rows(x_ref, g_ref, xn_ref, first)
        if first:
            o2_ref[...] = jnp.dot(xn, w2_ref[...], preferred_element_type=F32)
        o_ref[...] = jnp.dot(xn, w_ref[...], preferred_element_type=F32).astype(o_ref.dtype)

    _first_and_rest(step)


def _norm_matmul_t_kernel(x_ref, g_ref, w_ref, o_ref, xn_ref):
    def step(first):
        xn = _normed_rows(x_ref, g_ref, xn_ref, first)
        o_ref[...] = lax.dot_general(w_ref[...], xn, (((0,), (1,)), ((), ())),
                                     preferred_element_type=F32).astype(o_ref.dtype)

    _first_and_rest(step)


def norm_matmul_t(x, g, w, *, tm=1024, tn=1024, name):
    m, k = x.shape
    n = w.shape[1]
    tm, tn = min(tm, m), min(tn, n)
    return pl.pallas_call(
        _norm_matmul_t_kernel, grid=(m // tm, n // tn),
        in_specs=[pl.BlockSpec((tm, k), lambda i, j: (i, 0)),
                  pl.BlockSpec((1, k), lambda i, j: (0, 0)),
                  pl.BlockSpec((k, tn), lambda i, j: (0, j))],
        out_specs=pl.BlockSpec((tn, tm), lambda i, j: (j, i)),
        out_shape=jax.ShapeDtypeStruct((n, m), BF16), scratch_shapes=[pltpu.VMEM((tm, k), BF16)],
        compiler_params=_params("parallel", "arbitrary"), name=name)(x, g, w)


def norm_matmul(x, g, w, *, n=None, w2=None, tm=1024, tn=1024, name):
    m, k = x.shape
    n = w.shape[1] if n is None else n
    tm, tn = min(tm, m), min(tn, n)
    assert n % tn == 0
    grid = (m // tm, n // tn)
    x_spec = pl.BlockSpec((tm, k), lambda i, j: (i, 0))
    g_spec = pl.BlockSpec((1, k), lambda i, j: (0, 0))
    w_spec = pl.BlockSpec((k, tn), lambda i, j: (0, j))
    o_spec = pl.BlockSpec((tm, tn), lambda i, j: (i, j))
    scratch = [pltpu.VMEM((tm, k), BF16)]
    if w2 is None:
        return pl.pallas_call(
            _norm_matmul_kernel, grid=grid, in_specs=[x_spec, g_spec, w_spec], out_specs=o_spec,
            out_shape=jax.ShapeDtypeStruct((m, n), BF16), scratch_shapes=scratch,
            compiler_params=_params("parallel", "arbitrary"), name=name)(x, g, w)
    n2 = w2.shape[1]
    return pl.pallas_call(
        _norm_matmul2_kernel, grid=grid,
        in_specs=[x_spec, g_spec, w_spec, pl.BlockSpec((k, n2), lambda i, j: (0, 0))],
        out_specs=[o_spec, pl.BlockSpec((tm, n2), lambda i, j: (i, 0))],
        out_shape=[jax.ShapeDtypeStruct((m, n), BF16), jax.ShapeDtypeStruct((m, n2), F32)],
        scratch_shapes=scratch, compiler_params=_params("parallel", "arbitrary"), name=name)(x, g, w, w2)


def _matmul_norm_residual_kernel(y_ref, w_ref, h_ref, g_ref, o_ref):
    def step(first, final):
        d = jnp.dot(y_ref[...], w_ref[...], preferred_element_type=F32)
        a = d if first else o_ref[...] + d
        o_ref[...] = h_ref[...] + a * _rms_scale(a) * g_ref[...] if final else a

    _first_middle_last(step)


def matmul_norm_residual(y, w, h, g, *, tm=1024, tk=1024, name):
    m, k = y.shape
    n = w.shape[1]
    tm, tk = min(tm, m), min(tk, k)
    assert k // tk >= 2
    return pl.pallas_call(
        _matmul_norm_residual_kernel, grid=(m // tm, k // tk),
        in_specs=[pl.BlockSpec((tm, tk), lambda i, kk: (i, kk)),
                  pl.BlockSpec((tk, n), lambda i, kk: (kk, 0)),
                  pl.BlockSpec((tm, n), lambda i, kk: (i, 0)),
                  pl.BlockSpec((1, n), lambda i, kk: (0, 0))],
        out_specs=pl.BlockSpec((tm, n), lambda i, kk: (i, 0)),
        out_shape=jax.ShapeDtypeStruct((m, n), F32),
        compiler_params=_params("parallel", "arbitrary"), name=name)(y, w, h, g)


def _ffn_kernel(h_ref, gpre_ref, wg_ref, wu_ref, wd_ref, gpost_ref, o_ref, xn_ref):
    def step(first, final):
        xn = _normed_rows(h_ref, gpre_ref, xn_ref, first)
        gate = jnp.dot(xn, wg_ref[...], preferred_element_type=F32)
        up = jnp.dot(xn, wu_ref[...], preferred_element_type=F32)
        act = (_silu(gate) * up).astype(BF16)
        d = jnp.dot(act, wd_ref[...], preferred_element_type=F32)
        a = d if first else o_ref[...] + d
        o_ref[...] = h_ref[...] + a * _rms_scale(a) * gpost_ref[...] if final else a

    _first_middle_last(step)


def ffn(h, g_pre, w_gate, w_up, w_down, g_post, layer, *, tm=1024, tf=512, name):
    m, d = h.shape
    dff = w_gate.shape[2]
    tm = min(tm, m)
    assert dff // tf >= 2
    return pl.pallas_call(
        _ffn_kernel, grid=(m // tm, dff // tf),
        in_specs=[pl.BlockSpec((tm, d), lambda i, f: (i, 0)),
                  pl.BlockSpec((1, d), lambda i, f: (0, 0)),
                  pl.BlockSpec((None, d, tf), lambda i, f: (layer, 0, f)),
                  pl.BlockSpec((None, d, tf), lambda i, f: (layer, 0, f)),
                  pl.BlockSpec((None, tf, d), lambda i, f: (layer, f, 0)),
                  pl.BlockSpec((1, d), lambda i, f: (0, 0))],
        out_specs=pl.BlockSpec((tm, d), lambda i, f: (i, 0)),
        out_shape=jax.ShapeDtypeStruct((m, d), F32),
        scratch_shapes=[pltpu.VMEM((tm, d), BF16)],
        compiler_params=_params("parallel", "arbitrary"), name=name)(h, g_pre, w_gate, w_up, w_down, g_post)


def _expand_heads(v, rows):
    lane = lax.broadcasted_iota(jnp.int32, (rows, LANES), 1)
    first = lane < SSM_HEAD_DIM
    pairs = [jnp.where(first, v[:, 2 * p:2 * p + 1], v[:, 2 * p + 1:2 * p + 2]) for p in range(HEADS_PER_GROUP // 2)]
    return jnp.concatenate(pairs, axis=1)


def _split_bf16(v):
    hi = v.astype(BF16)
    lo = (v - hi.astype(F32)).astype(BF16)
    return jnp.concatenate([hi, lo], axis=1)


def _ssd_kernel(z_ref, xs_ref, b_ref, c_ref, dt_ref, cw_ref, cb_ref, dtb_ref, alog_ref, dskip_ref, nw_ref,
                y_ref, state_ref, tail_ref, pad_ref, act_ref, ccol_ref, crowT_ref, e1_ref, wl_ref, elast_ref):
    c = pl.program_id(0)
    g = pl.program_id(1)
    rows = z_ref.shape[0]
    n_sub = rows // SSD_SUB
    sub_row = lax.broadcasted_iota(jnp.int32, (SSD_SUB, SSD_SUB), 0)
    sub_col = lax.broadcasted_iota(jnp.int32, (SSD_SUB, SSD_SUB), 1)
    causal = sub_row >= sub_col

    @pl.when(c == 0)
    def _():
        state_ref[g] = jnp.zeros((SSM_STATE, GROUP_INNER), F32)
        tail_ref[g] = jnp.zeros((SUBLANES, GROUP_CONV), F32)

    @pl.when(g == 0)
    def _():
        dtv = jax.nn.softplus(dt_ref[...] + dtb_ref[...])
        dta = dtv * (-jnp.exp(alog_ref[...]))
        log_dt = jnp.log(dtv)
        tri = causal.astype(F32)
        for s in range(n_sub):
            sl = slice(s * SSD_SUB, (s + 1) * SSD_SUB)
            cum = jnp.dot(tri, dta[sl], precision=lax.Precision.HIGHEST,
                          preferred_element_type=F32)
            clast = cum[SSD_SUB - 1:SSD_SUB, :]
            ccol_ref[sl, :] = cum * LOG2E
            crowT_ref[:, sl] = ((cum - log_dt[sl]) * LOG2E).T
            e1_ref[sl, :] = _split_bf16(jnp.exp(cum))
            wl_ref[sl, :] = _split_bf16(jnp.exp(clast - cum) * dtv[sl])
            elast_ref[s] = jnp.exp(clast)

    xcur = jnp.concatenate([xs_ref[...].astype(F32), b_ref[...].astype(F32), c_ref[...].astype(F32)], axis=1)
    cw = cw_ref[...]
    bias = cb_ref[...]
    pad_ref[0:SUBLANES, :] = tail_ref[g]
    pad_ref[SUBLANES:2 * SUBLANES, :] = xcur[0:SUBLANES]
    tail_ref[g] = xcur[rows - SUBLANES:rows]
    head = bias
    body = bias + cw[CONV_WIDTH - 1:CONV_WIDTH, :] * xcur
    for k in range(CONV_WIDTH):
        lo = SUBLANES - (CONV_WIDTH - 1) + k
        head = head + cw[k:k + 1, :] * pad_ref[lo:lo + SUBLANES, :]
        if k < CONV_WIDTH - 1:
            body = body + cw[k:k + 1, :] * pltpu.roll(xcur, CONV_WIDTH - 1 - k, 0)
    act_ref[...] = _silu(jnp.concatenate([head, body[SUBLANES:]], axis=0))

    shift = lax.rem(LANES - HEADS_PER_GROUP * g, LANES)
    r0 = pl.multiple_of(g * HEADS_PER_GROUP, HEADS_PER_GROUP)
    lane16 = sub_col.astype(F32).astype(BF16)
    low_half = lane16 < SSM_HEAD_DIM
    zero16 = jnp.zeros((SSD_SUB, LANES), BF16)
    sel_row = lax.broadcasted_iota(jnp.int32, (2 * LANES, GROUP_INNER), 0) & (LANES - 1)
    sel_head = lax.broadcasted_iota(jnp.int32, (2 * LANES, GROUP_INNER), 1) // SSM_HEAD_DIM
    expand = (sel_row == sel_head + HEADS_PER_GROUP * g).astype(F32).astype(BF16)

    for s in range(n_sub):
        rs = slice(s * SSD_SUB, (s + 1) * SSD_SUB)
        act = act_ref[rs, :]
        xg = act[:, 0:GROUP_INNER]
        xb = xg.astype(BF16)
        bb = act[:, GROUP_INNER:GROUP_INNER + SSM_STATE].astype(BF16)
        cc = act[:, GROUP_INNER + SSM_STATE:GROUP_CONV].astype(BF16)
        ccol = pltpu.roll(ccol_ref[rs, :], shift, 1)
        crow = crowT_ref[pl.ds(r0, HEADS_PER_GROUP), rs]
        cbm = lax.dot_general(cc, bb, (((1,), (1,)), ((), ())), preferred_element_type=F32).astype(BF16)

        ys = []
        for p in range(HEADS_PER_GROUP // 2):
            ms = []
            for r in (2 * p, 2 * p + 1):
                seg = ccol[:, r:r + 1] - crow[r:r + 1, :]
                ms.append(cbm * jnp.exp2(jnp.where(causal, seg, -jnp.inf)).astype(BF16))
            xp = xb[:, p * LANES:(p + 1) * LANES]
            bd = jnp.concatenate([jnp.where(low_half, xp, zero16), jnp.where(low_half, zero16, xp)], axis=0)
            ys.append(jnp.dot(jnp.concatenate(ms, axis=1), bd, preferred_element_type=F32))
        y = jnp.concatenate(ys, axis=1)

        st = state_ref[g]
        e1 = jnp.dot(e1_ref[rs, :], expand, preferred_element_type=F32)
        y = y + jnp.dot(cc, st.astype(BF16), preferred_element_type=F32) * e1
        wl = jnp.dot(wl_ref[rs, :], expand, preferred_element_type=F32)
        xw = (xg * wl).astype(BF16)
        ds = lax.dot_general(bb, xw, (((0,), (0,)), ((), ())), preferred_element_type=F32)
        el = pltpu.roll(elast_ref[s], shift, 1)
        state_ref[g] = st * _expand_heads(el, 1) + ds

        y = y + dskip_ref[...] * xg
        y = y * _silu(z_ref[rs, :].astype(F32))
        y_ref[rs, :] = (y * _rms_scale(y) * nw_ref[...]).astype(y_ref.dtype)


def ssd_mixer(proj, dt_raw, conv_w_g, conv_b_g, dt_bias, a_log, d_skip, norm_w):
    seq = proj.shape[0]
    rows = min(SSD_ROWS, seq)
    n_sub = rows // SSD_SUB
    xcol0 = D_INNER // GROUP_INNER
    bcol0 = 2 * D_INNER // SSM_STATE
    ccol0 = bcol0 + GN // SSM_STATE
    return pl.pallas_call(
        _ssd_kernel, grid=(seq // rows, SSM_GROUPS),
        in_specs=[pl.BlockSpec((rows, GROUP_INNER), lambda c, g: (c, g)),
                  pl.BlockSpec((rows, GROUP_INNER), lambda c, g: (c, xcol0 + g)),
                  pl.BlockSpec((rows, SSM_STATE), lambda c, g: (c, bcol0 + g)),
                  pl.BlockSpec((rows, SSM_STATE), lambda c, g: (c, ccol0 + g)),
                  pl.BlockSpec((rows, LANES), lambda c, g: (c, 0)),
                  pl.BlockSpec((None, CONV_WIDTH, GROUP_CONV), lambda c, g: (g, 0, 0)),
                  pl.BlockSpec((None, 1, GROUP_CONV), lambda c, g: (g, 0, 0)),
                  pl.BlockSpec((1, LANES), lambda c, g: (0, 0)),
                  pl.BlockSpec((1, LANES), lambda c, g: (0, 0)),
                  pl.BlockSpec((1, GROUP_INNER), lambda c, g: (0, g)),
                  pl.BlockSpec((1, GROUP_INNER), lambda c, g: (0, g))],
        out_specs=pl.BlockSpec((rows, GROUP_INNER), lambda c, g: (c, g)),
        out_shape=jax.ShapeDtypeStruct((seq, D_INNER), BF16),
        scratch_shapes=[pltpu.VMEM((SSM_GROUPS, SSM_STATE, GROUP_INNER), F32),
                        pltpu.VMEM((SSM_GROUPS, SUBLANES, GROUP_CONV), F32),
                        pltpu.VMEM((2 * SUBLANES, GROUP_CONV), F32),
                        pltpu.VMEM((rows, GROUP_CONV), F32),
                        pltpu.VMEM((rows, LANES), F32),
                        pltpu.VMEM((LANES, rows), F32),
                        pltpu.VMEM((rows, 2 * LANES), BF16),
                        pltpu.VMEM((rows, 2 * LANES), BF16),
                        pltpu.VMEM((n_sub, 1, LANES), F32)],
        compiler_params=_params("arbitrary", "arbitrary"), name="ssd_mixer",
    )(proj, proj, proj, proj, dt_raw, conv_w_g, conv_b_g, dt_bias, a_log, d_skip, norm_w)


def _attn_kernel(sink_ref, qt_ref, kvp_ref, kvc_ref, o_ref, bias_ref):
    i = pl.program_id(0)
    group_lanes = Q_PER_KV * BLOCK

    @pl.when(i == 0)
    def _():
        sj = lax.broadcasted_iota(jnp.int32, (2 * BLOCK, group_lanes), 0)
        lane = lax.broadcasted_iota(jnp.int32, (2 * BLOCK, group_lanes), 1)
        dist_i = BLOCK + (lane & (BLOCK - 1)) - sj
        valid = (dist_i >= 0) & (dist_i < WINDOW)
        dist = dist_i.astype(F32)
        head_in_group = (lane // BLOCK).astype(F32)
        for kh in range(N_KV_HEADS):
            slope = jnp.exp2((head_in_group + (kh * Q_PER_KV + 1)) * (-8.0 / N_Q_HEADS))
            b = jnp.where(valid, -(slope * dist), -jnp.inf)
            bias_ref[1, kh] = b
            bias_ref[0, kh] = jnp.where(sj >= BLOCK, b, -jnp.inf)

    has_prev = jnp.minimum(i, 1)
    scale = HEAD_DIM ** -0.5
    kv = jnp.concatenate([kvp_ref[...], kvc_ref[...]], axis=0)
    for kh in range(N_KV_HEADS):
        k = kv[:, kh * HEAD_DIM:(kh + 1) * HEAD_DIM] * scale
        v = kv[:, KV_DIM + kh * HEAD_DIM:KV_DIM + (kh + 1) * HEAD_DIM]
        h0 = kh * Q_PER_KV
        qg = jnp.concatenate([qt_ref[(h0 + r) * HEAD_DIM:(h0 + r + 1) * HEAD_DIM, :] for r in range(Q_PER_KV)],
                             axis=1)
        sink = sink_ref[kh]
        logits = jnp.dot(k, qg, preferred_element_type=F32) + bias_ref[has_prev, kh]
        m = jnp.maximum(jnp.max(logits, axis=0, keepdims=True), sink)
        p = jnp.exp(logits - m)
        denom = jnp.sum(p, axis=0, keepdims=True) + jnp.exp(sink - m)
        ot = lax.dot_general(v, p.astype(BF16), (((0,), (0,)), ((), ())), preferred_element_type=F32) / denom
        for r in range(0, Q_PER_KV, 2):
            pair = jnp.concatenate([ot[:, r * BLOCK:(r + 1) * BLOCK], ot[:, (r + 1) * BLOCK:(r + 2) * BLOCK]], axis=0)
            o_ref[:, (h0 + r) * HEAD_DIM:(h0 + r + 2) * HEAD_DIM] = pair.T.astype(o_ref.dtype)


def swa_attention(qt, kv, sinks):
    seq = kv.shape[0]
    nb = seq // BLOCK
    sink_rows = jnp.repeat(sinks.astype(F32), BLOCK).reshape(N_KV_HEADS, 1, Q_PER_KV * BLOCK)
    return pl.pallas_call(
        _attn_kernel, grid=(nb,),
        in_specs=[pl.BlockSpec((N_KV_HEADS, 1, Q_PER_KV * BLOCK), lambda i: (0, 0, 0)),
                  pl.BlockSpec((N_Q_HEADS * HEAD_DIM, BLOCK), lambda i: (0, i)),
                  pl.BlockSpec((BLOCK, 2 * KV_DIM), lambda i: (jnp.maximum(i - 1, 0), 0)),
                  pl.BlockSpec((BLOCK, 2 * KV_DIM), lambda i: (i, 0))],
        out_specs=pl.BlockSpec((BLOCK, N_Q_HEADS * HEAD_DIM), lambda i: (i, 0)),
        out_shape=jax.ShapeDtypeStruct((seq, N_Q_HEADS * HEAD_DIM), BF16),
        scratch_shapes=[pltpu.VMEM((2, N_KV_HEADS, 2 * BLOCK, Q_PER_KV * BLOCK), F32)],
        compiler_params=_params("arbitrary"), name="swa_attention",
    )(sink_rows, qt, kv, kv)


def _pad_lanes(v, width=LANES):
    v = v.reshape(1, -1).astype(F32)
    return jnp.pad(v, ((0, 0), (0, width - v.shape[1])))


def kernel(x, norm_w, ssm_w_in, ssm_conv_w, ssm_conv_b, ssm_dt_bias, ssm_A_log, ssm_D, ssm_norm_w, ssm_w_out,
           kv_norm_w, w_kv, attn_w_q, attn_sinks, attn_w_o, ffn_w_gate, ffn_w_up, ffn_w_down):
    batch, seq, d = x.shape
    assert batch == 1 and d == D_MODEL and seq % SSD_ROWS == 0
    h = x.reshape(seq, d)
    row = lambda v: v.reshape(1, -1).astype(F32)

    w_gate, w_up, w_down = ffn_w_gate.astype(BF16), ffn_w_up.astype(BF16), ffn_w_down.astype(BF16)

    g = norm_w[0]
    w_in = ssm_w_in[0].astype(BF16)
    n_main = 2 * D_INNER + 2 * GN
    w_dt = jnp.pad(w_in[:, n_main:], ((0, 0), (0, LANES - SSM_HEADS)))
    proj, dt_raw = norm_matmul(h, row(g[0]), w_in, n=n_main, w2=w_dt, name="in_proj")

    def per_group(t):
        xs = t[:, :D_INNER].reshape(-1, SSM_GROUPS, GROUP_INNER)
        bs = t[:, D_INNER:D_INNER + GN].reshape(-1, SSM_GROUPS, SSM_STATE)
        cs = t[:, D_INNER + GN:].reshape(-1, SSM_GROUPS, SSM_STATE)
        return jnp.transpose(jnp.concatenate([xs, bs, cs], axis=-1), (1, 0, 2)).astype(F32)

    y = ssd_mixer(proj, dt_raw, per_group(ssm_conv_w[0]), per_group(ssm_conv_b[0].reshape(1, -1)),
                  _pad_lanes(ssm_dt_bias[0]), _pad_lanes(ssm_A_log[0]),
                  row(jnp.repeat(ssm_D[0], SSM_HEAD_DIM)), row(ssm_norm_w[0]))
    h = matmul_norm_residual(y, ssm_w_out[0].astype(BF16), h, row(g[1]), name="ssm_out_proj")
    h = ffn(h, row(g[2]), w_gate, w_up, w_down, row(g[3]), 0, name="ffn0")

    g = norm_w[1]
    kv = norm_matmul(h, row(kv_norm_w), w_kv.astype(BF16), name="kv_proj")
    qt = norm_matmul_t(h, row(g[0]), attn_w_q[0].astype(BF16), name="q_proj")
    a = swa_attention(qt, kv, attn_sinks[0])
    h = matmul_norm_residual(a, attn_w_o[0].astype(BF16), h, row(g[1]), name="attn_out_proj")
    h = ffn(h, row(g[2]), w_gate, w_up, w_down, row(g[3]), 1, name="ffn1")
    return h.reshape(batch, seq, d)
```

```python
import math

import jax
import jax.numpy as jnp
from jax import lax
from jax.experimental import pallas as pl
from jax.experimental.pallas import tpu as pltpu

D_MODEL = 2048
D_INNER = 4096
SSM_HEAD_DIM = 64
SSM_HEADS = 64
SSM_GROUPS = 8
HEADS_PER_GROUP = SSM_HEADS // SSM_GROUPS
SSM_STATE = 128
CONV_WIDTH = 4
GN = SSM_GROUPS * SSM_STATE
GROUP_INNER = D_INNER // SSM_GROUPS
GROUP_CONV = GROUP_INNER + 2 * SSM_STATE
SSD_SUB = 128
SSD_ROWS = 1024
N_Q_HEADS = 32
N_KV_HEADS = 4
Q_PER_KV = N_Q_HEADS // N_KV_HEADS
HEAD_DIM = 64
WINDOW = 128
BLOCK = 128
KV_DIM = N_KV_HEADS * HEAD_DIM
D_FF = 5632
EPS = 1e-6
LOG2E = math.log2(math.e)

LANES = 128
SUBLANES = 8
VMEM_LIMIT_BYTES = 63 * 1024 * 1024

BF16 = jnp.bfloat16
F32 = jnp.float32


def _rms_scale(x):
    return lax.rsqrt(jnp.mean(x * x, axis=-1, keepdims=True) + EPS)


def _silu(x):
    return x * jax.nn.sigmoid(x)


def _params(*semantics):
    return pltpu.CompilerParams(dimension_semantics=semantics, vmem_limit_bytes=VMEM_LIMIT_BYTES)


def _normed_rows(x_ref, g_ref, xn_ref, first):
    if not first:
        return xn_ref[...]
    x = x_ref[...]
    xn = (x * _rms_scale(x) * g_ref[...]).astype(xn_ref.dtype)
    xn_ref[...] = xn
    return xn


def _first_and_rest(step):
    j = pl.program_id(1)
    pl.when(j == 0)(lambda: step(True))
    pl.when(j > 0)(lambda: step(False))


def _first_middle_last(step):
    j = pl.program_id(1)
    last = pl.num_programs(1) - 1
    pl.when(j == 0)(lambda: step(True, False))
    pl.when(jnp.logical_and(j > 0, j < last))(lambda: step(False, False))
    pl.when(j == last)(lambda: step(False, True))


def _norm_matmul_kernel(x_ref, g_ref, w_ref, o_ref, xn_ref):
    def step(first):
        xn = _normed_rows(x_ref, g_ref, xn_ref, first)
        o_ref[...] = jnp.dot(xn, w_ref[...], preferred_element_type=F32).astype(o_ref.dtype)

    _first_and_rest(step)


def _norm_matmul2_kernel(x_ref, g_ref, w_ref, w2_ref, o_ref, o2_ref, xn_ref):
    def step(first):
        xn = _normed_rows(x_ref, g_ref, xn_ref, first)
        if first:
            o2_ref[...] = jnp.dot(xn, w2_ref[...], preferred_element_type=F32)
        o_ref[...] = jnp.dot(xn, w_ref[...], preferred_element_type=F32).astype(o_ref.dtype)

    _first_and_rest(step)


def _norm_matmul_t_kernel(x_ref, g_ref, w_ref, o_ref, xn_ref):
    def step(first):
        xn = _normed_rows(x_ref, g_ref, xn_ref, first)
        o_ref[...] = lax.dot_general(w_ref[...], xn, (((0,), (1,)), ((), ())),
                                     preferred_element_type=F32).astype(o_ref.dtype)

    _first_and_rest(step)


def norm_matmul_t(x, g, w, *, tm=1024, tn=1024, name):
    m, k = x.shape
    n = w.shape[1]
    tm, tn = min(tm, m), min(tn, n)
    return pl.pallas_call(
        _norm_matmul_t_kernel, grid=(m // tm, n // tn),
        in_specs=[pl.BlockSpec((tm, k), lambda i, j: (i, 0)),
                  pl.BlockSpec((1, k), lambda i, j: (0, 0)),
                  pl.BlockSpec((k, tn), lambda i, j: (0, j))],
        out_specs=pl.BlockSpec((tn, tm), lambda i, j: (j, i)),
        out_shape=jax.ShapeDtypeStruct((n, m), BF16), scratch_shapes=[pltpu.VMEM((tm, k), BF16)],
        compiler_params=_params("parallel", "arbitrary"), name=name)(x, g, w)


def norm_matmul(x, g, w, *, n=None, w2=None, tm=1024, tn=1024, name):
    m, k = x.shape
    n = w.shape[1] if n is None else n
    tm, tn = min(tm, m), min(tn, n)
    assert n % tn == 0
    grid = (m // tm, n // tn)
    x_spec = pl.BlockSpec((tm, k), lambda i, j: (i, 0))
    g_spec = pl.BlockSpec((1, k), lambda i, j: (0, 0))
    w_spec = pl.BlockSpec((k, tn), lambda i, j: (0, j))
    o_spec = pl.BlockSpec((tm, tn), lambda i, j: (i, j))
    scratch = [pltpu.VMEM((tm, k), BF16)]
    if w2 is None:
        return pl.pallas_call(
            _norm_matmul_kernel, grid=grid, in_specs=[x_spec, g_spec, w_spec], out_specs=o_spec,
            out_shape=jax.ShapeDtypeStruct((m, n), BF16), scratch_shapes=scratch,
            compiler_params=_params("parallel", "arbitrary"), name=name)(x, g, w)
    n2 = w2.shape[1]
    return pl.pallas_call(
        _norm_matmul2_kernel, grid=grid,
        in_specs=[x_spec, g_spec, w_spec, pl.BlockSpec((k, n2), lambda i, j: (0, 0))],
        out_specs=[o_spec, pl.BlockSpec((tm, n2), lambda i, j: (i, 0))],
        out_shape=[jax.ShapeDtypeStruct((m, n), BF16), jax.ShapeDtypeStruct((m, n2), F32)],
        scratch_shapes=scratch, compiler_params=_params("parallel", "arbitrary"), name=name)(x, g, w, w2)


def _matmul_norm_residual_kernel(y_ref, w_ref, h_ref, g_ref, o_ref):
    def step(first, final):
        d = jnp.dot(y_ref[...], w_ref[...], preferred_element_type=F32)
        a = d if first else o_ref[...] + d
        o_ref[...] = h_ref[...] + a * _rms_scale(a) * g_ref[...] if final else a

    _first_middle_last(step)


def matmul_norm_residual(y, w, h, g, *, tm=1024, tk=1024, name):
    m, k = y.shape
    n = w.shape[1]
    tm, tk = min(tm, m), min(tk, k)
    assert k // tk >= 2
    return pl.pallas_call(
        _matmul_norm_residual_kernel, grid=(m // tm, k // tk),
        in_specs=[pl.BlockSpec((tm, tk), lambda i, kk: (i, kk)),
                  pl.BlockSpec((tk, n), lambda i, kk: (kk, 0)),
                  pl.BlockSpec((tm, n), lambda i, kk: (i, 0)),
                  pl.BlockSpec((1, n), lambda i, kk: (0, 0))],
        out_specs=pl.BlockSpec((tm, n), lambda i, kk: (i, 0)),
        out_shape=jax.ShapeDtypeStruct((m, n), F32),
        compiler_params=_params("parallel", "arbitrary"), name=name)(y, w, h, g)


def _ffn_kernel(h_ref, gpre_ref, wg_ref, wu_ref, wd_ref, gpost_ref, o_ref, xn_ref):
    def step(first, final):
        xn = _normed_rows(h_ref, gpre_ref, xn_ref, first)
        gate = jnp.dot(xn, wg_ref[...], preferred_element_type=F32)
        up = jnp.dot(xn, wu_ref[...], preferred_element_type=F32)
        act = (_silu(gate) * up).astype(BF16)
        d = jnp.dot(act, wd_ref[...], preferred_element_type=F32)
        a = d if first else o_ref[...] + d
        o_ref[...] = h_ref[...] + a * _rms_scale(a) * gpost_ref[...] if final else a

    _first_middle_last(step)


def ffn(h, g_pre, w_gate, w_up, w_down, g_post, layer, *, tm=1024, tf=512, name):
    m, d = h.shape
    dff = w_gate.shape[2]
    tm = min(tm, m)
    assert dff // tf >= 2
    return pl.pallas_call(
        _ffn_kernel, grid=(m // tm, dff // tf),
        in_specs=[pl.BlockSpec((tm, d), lambda i, f: (i, 0)),
                  pl.BlockSpec((1, d), lambda i, f: (0, 0)),
                  pl.BlockSpec((None, d, tf), lambda i, f: (layer, 0, f)),
                  pl.BlockSpec((None, d, tf), lambda i, f: (layer, 0, f)),
                  pl.BlockSpec((None, tf, d), lambda i, f: (layer, f, 0)),
                  pl.BlockSpec((1, d), lambda i, f: (0, 0))],
        out_specs=pl.BlockSpec((tm, d), lambda i, f: (i, 0)),
        out_shape=jax.ShapeDtypeStruct((m, d), F32),
        scratch_shapes=[pltpu.VMEM((tm, d), BF16)],
        compiler_params=_params("parallel", "arbitrary"), name=name)(h, g_pre, w_gate, w_up, w_down, g_post)


def _expand_heads(v, rows):
    lane = lax.broadcasted_iota(jnp.int32, (rows, LANES), 1)
    first = lane < SSM_HEAD_DIM
    pairs = [jnp.where(first, v[:, 2 * p:2 * p + 1], v[:, 2 * p + 1:2 * p + 2]) for p in range(HEADS_PER_GROUP // 2)]
    return jnp.concatenate(pairs, axis=1)


def _split_bf16(v):
    hi = v.astype(BF16)
    lo = (v - hi.astype(F32)).astype(BF16)
    return jnp.concatenate([hi, lo], axis=1)


def _ssd_kernel(z_ref, xs_ref, b_ref, c_ref, dt_ref, cw_ref, cb_ref, dtb_ref, alog_ref, dskip_ref, nw_ref,
                y_ref, state_ref, tail_ref, pad_ref, act_ref, ccol_ref, crowT_ref, e1_ref, wl_ref, elast_ref):
    c = pl.program_id(0)
    g = pl.program_id(1)
    rows = z_ref.shape[0]
    n_sub = rows // SSD_SUB
    sub_row = lax.broadcasted_iota(jnp.int32, (SSD_SUB, SSD_SUB), 0)
    sub_col = lax.broadcasted_iota(jnp.int32, (SSD_SUB, SSD_SUB), 1)
    causal = sub_row >= sub_col

    @pl.when(c == 0)
    def _():
        state_ref[g] = jnp.zeros((SSM_STATE, GROUP_INNER), F32)
        tail_ref[g] = jnp.zeros((SUBLANES, GROUP_CONV), F32)

    @pl.when(g == 0)
    def _():
        dtv = jax.nn.softplus(dt_ref[...] + dtb_ref[...])
        dta = dtv * (-jnp.exp(alog_ref[...]))
        log_dt = jnp.log(dtv)
        tri = causal.astype(F32)
        for s in range(n_sub):
            sl = slice(s * SSD_SUB, (s + 1) * SSD_SUB)
            cum = jnp.dot(tri, dta[sl], precision=lax.Precision.HIGHEST,
                          preferred_element_type=F32)
            clast = cum[SSD_SUB - 1:SSD_SUB, :]
            ccol_ref[sl, :] = cum * LOG2E
            crowT_ref[:, sl] = ((cum - log_dt[sl]) * LOG2E).T
            e1_ref[sl, :] = _split_bf16(jnp.exp(cum))
            wl_ref[sl, :] = _split_bf16(jnp.exp(clast - cum) * dtv[sl])
            elast_ref[s] = jnp.exp(clast)

    xcur = jnp.concatenate([xs_ref[...].astype(F32), b_ref[...].astype(F32), c_ref[...].astype(F32)], axis=1)
    cw = cw_ref[...]
    bias = cb_ref[...]
    pad_ref[0:SUBLANES, :] = tail_ref[g]
    pad_ref[SUBLANES:2 * SUBLANES, :] = xcur[0:SUBLANES]
    tail_ref[g] = xcur[rows - SUBLANES:rows]
    head = bias
    body = bias + cw[CONV_WIDTH - 1:CONV_WIDTH, :] * xcur
    for k in range(CONV_WIDTH):
        lo = SUBLANES - (CONV_WIDTH - 1) + k
        head = head + cw[k:k + 1, :] * pad_ref[lo:lo + SUBLANES, :]
        if k < CONV_WIDTH - 1:
            body = body + cw[k:k + 1, :] * pltpu.roll(xcur, CONV_WIDTH - 1 - k, 0)
    act_ref[...] = _silu(jnp.concatenate([head, body[SUBLANES:]], axis=0))

    shift = lax.rem(LANES - HEADS_PER_GROUP * g, LANES)
    r0 = pl.multiple_of(g * HEADS_PER_GROUP, HEADS_PER_GROUP)
    lane16 = sub_col.astype(F32).astype(BF16)
    low_half = lane16 < SSM_HEAD_DIM
    zero16 = jnp.zeros((SSD_SUB, LANES), BF16)
    sel_row = lax.broadcasted_iota(jnp.int32, (2 * LANES, GROUP_INNER), 0) & (LANES - 1)
    sel_head = lax.broadcasted_iota(jnp.int32, (2 * LANES, GROUP_INNER), 1) // SSM_HEAD_DIM
    expand = (sel_row == sel_head + HEADS_PER_GROUP * g).astype(F32).astype(BF16)

    for s in range(n_sub):
        rs = slice(s * SSD_SUB, (s + 1) * SSD_SUB)
        act = act_ref[rs, :]
        xg = act[:, 0:GROUP_INNER]
        xb = xg.astype(BF16)
        bb = act[:, GROUP_INNER:GROUP_INNER + SSM_STATE].astype(BF16)
        cc = act[:, GROUP_INNER + SSM_STATE:GROUP_CONV].astype(BF16)
        ccol = pltpu.roll(ccol_ref[rs, :], shift, 1)
        crow = crowT_ref[pl.ds(r0, HEADS_PER_GROUP), rs]
        cbm = lax.dot_general(cc, bb, (((1,), (1,)), ((), ())), preferred_element_type=F32).astype(BF16)

        ys = []
        for p in range(HEADS_PER_GROUP // 2):
            ms = []
            for r in (2 * p, 2 * p + 1):
                seg = ccol[:, r:r + 1] - crow[r:r + 1, :]
                ms.append(cbm * jnp.exp2(jnp.where(causal, seg, -jnp.inf)).astype(BF16))
            xp = xb[:, p * LANES:(p + 1) * LANES]
            bd = jnp.concatenate([jnp.where(low_half, xp, zero16), jnp.where(low_half, zero16, xp)], axis=0)
            ys.append(jnp.dot(jnp.concatenate(ms, axis=1), bd, preferred_element_type=F32))
        y = jnp.concatenate(ys, axis=1)

        st = state_ref[g]
        e1 = jnp.dot(e1_ref[rs, :], expand, preferred_element_type=F32)
        y = y + jnp.dot(cc, st.astype(BF16), preferred_element_type=F32) * e1
        wl = jnp.dot(wl_ref[rs, :], expand, preferred_element_type=F32)
        xw = (xg * wl).astype(BF16)
        ds = lax.dot_general(bb, xw, (((0,), (0,)), ((), ())), preferred_element_type=F32)
        el = pltpu.roll(elast_ref[s], shift, 1)
        state_ref[g] = st * _expand_heads(el, 1) + ds

        y = y + dskip_ref[...] * xg
        y = y * _silu(z_ref[rs, :].astype(F32))
        y_ref[rs, :] = (y * _rms_scale(y) * nw_ref[...]).astype(y_ref.dtype)


def ssd_mixer(proj, dt_raw, conv_w_g, conv_b_g, dt_bias, a_log, d_skip, norm_w):
    seq = proj.shape[0]
    rows = min(SSD_ROWS, seq)
    n_sub = rows // SSD_SUB
    xcol0 = D_INNER // GROUP_INNER
    bcol0 = 2 * D_INNER // SSM_STATE
    ccol0 = bcol0 + GN // SSM_STATE
    return pl.pallas_call(
        _ssd_kernel, grid=(seq // rows, SSM_GROUPS),
        in_specs=[pl.BlockSpec((rows, GROUP_INNER), lambda c, g: (c, g)),
                  pl.BlockSpec((rows, GROUP_INNER), lambda c, g: (c, xcol0 + g)),
                  pl.BlockSpec((rows, SSM_STATE), lambda c, g: (c, bcol0 + g)),
                  pl.BlockSpec((rows, SSM_STATE), lambda c, g: (c, ccol0 + g)),
                  pl.BlockSpec((rows, LANES), lambda c, g: (c, 0)),
                  pl.BlockSpec((None, CONV_WIDTH, GROUP_CONV), lambda c, g: (g, 0, 0)),
                  pl.BlockSpec((None, 1, GROUP_CONV), lambda c, g: (g, 0, 0)),
                  pl.BlockSpec((1, LANES), lambda c, g: (0, 0)),
                  pl.BlockSpec((1, LANES), lambda c, g: (0, 0)),
                  pl.BlockSpec((1, GROUP_INNER), lambda c, g: (0, g)),
                  pl.BlockSpec((1, GROUP_INNER), lambda c, g: (0, g))],
        out_specs=pl.BlockSpec((rows, GROUP_INNER), lambda c, g: (c, g)),
        out_shape=jax.ShapeDtypeStruct((seq, D_INNER), BF16),
        scratch_shapes=[pltpu.VMEM((SSM_GROUPS, SSM_STATE, GROUP_INNER), F32),
                        pltpu.VMEM((SSM_GROUPS, SUBLANES, GROUP_CONV), F32),
                        pltpu.VMEM((2 * SUBLANES, GROUP_CONV), F32),
                        pltpu.VMEM((rows, GROUP_CONV), F32),
                        pltpu.VMEM((rows, LANES), F32),
                        pltpu.VMEM((LANES, rows), F32),
                        pltpu.VMEM((rows, 2 * LANES), BF16),
                        pltpu.VMEM((rows, 2 * LANES), BF16),
                        pltpu.VMEM((n_sub, 1, LANES), F32)],
        compiler_params=_params("arbitrary", "arbitrary"), name="ssd_mixer",
    )(proj, proj, proj, proj, dt_raw, conv_w_g, conv_b_g, dt_bias, a_log, d_skip, norm_w)


def _attn_kernel(sink_ref, qt_ref, kvp_ref, kvc_ref, o_ref, bias_ref):
    i = pl.program_id(0)
    group_lanes = Q_PER_KV * BLOCK

    @pl.when(i == 0)
    def _():
        sj = lax.broadcasted_iota(jnp.int32, (2 * BLOCK, group_lanes), 0)
        lane = lax.broadcasted_iota(jnp.int32, (2 * BLOCK, group_lanes), 1)
        dist_i = BLOCK + (lane & (BLOCK - 1)) - sj
        valid = (dist_i >= 0) & (dist_i < WINDOW)
        dist = dist_i.astype(F32)
        head_in_group = (lane // BLOCK).astype(F32)
        for kh in range(N_KV_HEADS):
            slope = jnp.exp2((head_in_group + (kh * Q_PER_KV + 1)) * (-8.0 / N_Q_HEADS))
            b = jnp.where(valid, -(slope * dist), -jnp.inf)
            bias_ref[1, kh] = b
            bias_ref[0, kh] = jnp.where(sj >= BLOCK, b, -jnp.inf)

    has_prev = jnp.minimum(i, 1)
    scale = HEAD_DIM ** -0.5
    kv = jnp.concatenate([kvp_ref[...], kvc_ref[...]], axis=0)
    for kh in range(N_KV_HEADS):
        k = kv[:, kh * HEAD_DIM:(kh + 1) * HEAD_DIM] * scale
        v = kv[:, KV_DIM + kh * HEAD_DIM:KV_DIM + (kh + 1) * HEAD_DIM]
        h0 = kh * Q_PER_KV
        qg = jnp.concatenate([qt_ref[(h0 + r) * HEAD_DIM:(h0 + r + 1) * HEAD_DIM, :] for r in range(Q_PER_KV)],
                             axis=1)
        sink = sink_ref[kh]
        logits = jnp.dot(k, qg, preferred_element_type=F32) + bias_ref[has_prev, kh]
        m = jnp.maximum(jnp.max(logits, axis=0, keepdims=True), sink)
        p = jnp.exp(logits - m)
        denom = jnp.sum(p, axis=0, keepdims=True) + jnp.exp(sink - m)
        ot = lax.dot_general(v, p.astype(BF16), (((0,), (0,)), ((), ())), preferred_element_type=F32) / denom
        for r in range(0, Q_PER_KV, 2):
            pair = jnp.concatenate([ot[:, r * BLOCK:(r + 1) * BLOCK], ot[:, (r + 1) * BLOCK:(r + 2) * BLOCK]], axis=0)
            o_ref[:, (h0 + r) * HEAD_DIM:(h0 + r + 2) * HEAD_DIM] = pair.T.astype(o_ref.dtype)


def swa_attention(qt, kv, sinks):
    seq = kv.shape[0]
    nb = seq // BLOCK
    sink_rows = jnp.repeat(sinks.astype(F32), BLOCK).reshape(N_KV_HEADS, 1, Q_PER_KV * BLOCK)
    return pl.pallas_call(
        _attn_kernel, grid=(nb,),
        in_specs=[pl.BlockSpec((N_KV_HEADS, 1, Q_PER_KV * BLOCK), lambda i: (0, 0, 0)),
                  pl.BlockSpec((N_Q_HEADS * HEAD_DIM, BLOCK), lambda i: (0, i)),
                  pl.BlockSpec((BLOCK, 2 * KV_DIM), lambda i: (jnp.maximum(i - 1, 0), 0)),
                  pl.BlockSpec((BLOCK, 2 * KV_DIM), lambda i: (i, 0))],
        out_specs=pl.BlockSpec((BLOCK, N_Q_HEADS * HEAD_DIM), lambda i: (i, 0)),
        out_shape=jax.ShapeDtypeStruct((seq, N_Q_HEADS * HEAD_DIM), BF16),
        scratch_shapes=[pltpu.VMEM((2, N_KV_HEADS, 2 * BLOCK, Q_PER_KV * BLOCK), F32)],
        compiler_params=_params("arbitrary"), name="swa_attention",
    )(sink_rows, qt, kv, kv)


def _pad_lanes(v, width=LANES):
    v = v.reshape(1, -1).astype(F32)
    return jnp.pad(v, ((0, 0), (0, width - v.shape[1])))


def kernel(x, norm_w, ssm_w_in, ssm_conv_w, ssm_conv_b, ssm_dt_bias, ssm_A_log, ssm_D, ssm_norm_w, ssm_w_out,
           kv_norm_w, w_kv, attn_w_q, attn_sinks, attn_w_o, ffn_w_gate, ffn_w_up, ffn_w_down):
    batch, seq, d = x.shape
    assert batch == 1 and d == D_MODEL and seq % SSD_ROWS == 0
    h = x.reshape(seq, d)
    row = lambda v: v.reshape(1, -1).astype(F32)

    w_gate, w_up, w_down = ffn_w_gate.astype(BF16), ffn_w_up.astype(BF16), ffn_w_down.astype(BF16)

    g = norm_w[0]
    w_in = ssm_w_in[0].astype(BF16)
    n_main = 2 * D_INNER + 2 * GN
    w_dt = jnp.pad(w_in[:, n_main:], ((0, 0), (0, LANES - SSM_HEADS)))
    proj, dt_raw = norm_matmul(h, row(g[0]), w_in, n=n_main, w2=w_dt, tn=2048, name="in_proj")

    def per_group(t):
        xs = t[:, :D_INNER].reshape(-1, SSM_GROUPS, GROUP_INNER)
        bs = t[:, D_INNER:D_INNER + GN].reshape(-1, SSM_GROUPS, SSM_STATE)
        cs = t[:, D_INNER + GN:].reshape(-1, SSM_GROUPS, SSM_STATE)
        return jnp.transpose(jnp.concatenate([xs, bs, cs], axis=-1), (1, 0, 2)).astype(F32)

    y = ssd_mixer(proj, dt_raw, per_group(ssm_conv_w[0]), per_group(ssm_conv_b[0].reshape(1, -1)),
                  _pad_lanes(ssm_dt_bias[0]), _pad_lanes(ssm_A_log[0]),
                  row(jnp.repeat(ssm_D[0], SSM_HEAD_DIM)), row(ssm_norm_w[0]))
    h = matmul_norm_residual(y, ssm_w_out[0].astype(BF16), h, row(g[1]), name="ssm_out_proj")
    h = ffn(h, row(g[2]), w_gate, w_up, w_down, row(g[3]), 0, name="ffn0")

    g = norm_w[1]
    kv = norm_matmul(h, row(kv_norm_w), w_kv.astype(BF16), name="kv_proj")
    qt = norm_matmul_t(h, row(g[0]), attn_w_q[0].astype(BF16), name="q_proj")
    a = swa_attention(qt, kv, attn_sinks[0])
    h = matmul_norm_residual(a, attn_w_o[0].astype(BF16), h, row(g[1]), name="attn_out_proj")
    h = ffn(h, row(g[2]), w_gate, w_up, w_down, row(g[3]), 1, name="ffn1")
    return h.reshape(batch, seq, d)
```

```python
import math

import jax
import jax.numpy as jnp
from jax import lax
from jax.experimental import pallas as pl
from jax.experimental.pallas import tpu as pltpu

D_MODEL = 2048
D_INNER = 4096
SSM_HEAD_DIM = 64
SSM_HEADS = 64
SSM_GROUPS = 8
HEADS_PER_GROUP = SSM_HEADS // SSM_GROUPS
SSM_STATE = 128
CONV_WIDTH = 4
GN = SSM_GROUPS * SSM_STATE
GROUP_INNER = D_INNER // SSM_GROUPS
GROUP_CONV = GROUP_INNER + 2 * SSM_STATE
SSD_SUB = 128
SSD_ROWS = 1024
N_Q_HEADS = 32
N_KV_HEADS = 4
Q_PER_KV = N_Q_HEADS // N_KV_HEADS
HEAD_DIM = 64
WINDOW = 128
BLOCK = 128
KV_DIM = N_KV_HEADS * HEAD_DIM
D_FF = 5632
EPS = 1e-6
LOG2E = math.log2(math.e)

LANES = 128
SUBLANES = 8
VMEM_LIMIT_BYTES = 63 * 1024 * 1024

BF16 = jnp.bfloat16
F32 = jnp.float32


def _rms_scale(x):
    return lax.rsqrt(jnp.mean(x * x, axis=-1, keepdims=True) + EPS)


def _silu(x):
    return x * jax.nn.sigmoid(x)


def _params(*semantics):
    return pltpu.CompilerParams(dimension_semantics=semantics, vmem_limit_bytes=VMEM_LIMIT_BYTES)


def _normed_rows(x_ref, g_ref, xn_ref, first):
    if not first:
        return xn_ref[...]
    x = x_ref[...]
    xn = (x * _rms_scale(x) * g_ref[...]).astype(xn_ref.dtype)
    xn_ref[...] = xn
    return xn


def _first_and_rest(step):
    j = pl.program_id(1)
    pl.when(j == 0)(lambda: step(True))
    pl.when(j > 0)(lambda: step(False))


def _first_middle_last(step):
    j = pl.program_id(1)
    last = pl.num_programs(1) - 1
    pl.when(j == 0)(lambda: step(True, False))
    pl.when(jnp.logical_and(j > 0, j < last))(lambda: step(False, False))
    pl.when(j == last)(lambda: step(False, True))


def _norm_matmul_kernel(x_ref, g_ref, w_ref, o_ref, xn_ref):
    def step(first):
        xn = _normed_rows(x_ref, g_ref, xn_ref, first)
        o_ref[...] = jnp.dot(xn, w_ref[...], preferred_element_type=F32).astype(o_ref.dtype)

    _first_and_rest(step)


def _norm_matmul2_kernel(x_ref, g_ref, wt_ref, wt2_ref, o_ref, o2_ref, xn_ref):
    nt = (((1,), (1,)), ((), ()))

    def step(first):
        xn = _normed_rows(x_ref, g_ref, xn_ref, first)
        if first:
            o2_ref[...] = lax.dot_general(xn, wt2_ref[...], nt, preferred_element_type=F32)
        o_ref[...] = lax.dot_general(xn, wt_ref[...], nt, preferred_element_type=F32).astype(o_ref.dtype)

    _first_and_rest(step)


def _norm_matmul_t_kernel(x_ref, g_ref, w_ref, o_ref, xn_ref):
    def step(first):
        xn = _normed_rows(x_ref, g_ref, xn_ref, first)
        o_ref[...] = lax.dot_general(w_ref[...], xn, (((0,), (1,)), ((), ())),
                                     preferred_element_type=F32).astype(o_ref.dtype)

    _first_and_rest(step)


def norm_matmul_t(x, g, w, *, tm=1024, tn=1024, name):
    m, k = x.shape
    n = w.shape[1]
    tm, tn = min(tm, m), min(tn, n)
    return pl.pallas_call(
        _norm_matmul_t_kernel, grid=(m // tm, n // tn),
        in_specs=[pl.BlockSpec((tm, k), lambda i, j: (i, 0)),
                  pl.BlockSpec((1, k), lambda i, j: (0, 0)),
                  pl.BlockSpec((k, tn), lambda i, j: (0, j))],
        out_specs=pl.BlockSpec((tn, tm), lambda i, j: (j, i)),
        out_shape=jax.ShapeDtypeStruct((n, m), BF16), scratch_shapes=[pltpu.VMEM((tm, k), BF16)],
        compiler_params=_params("parallel", "arbitrary"), name=name)(x, g, w)


def norm_matmul(x, g, w, *, tm=1024, tn=1024, name):
    m, k = x.shape
    n = w.shape[1]
    tm, tn = min(tm, m), min(tn, n)
    return pl.pallas_call(
        _norm_matmul_kernel, grid=(m // tm, n // tn),
        in_specs=[pl.BlockSpec((tm, k), lambda i, j: (i, 0)),
                  pl.BlockSpec((1, k), lambda i, j: (0, 0)),
                  pl.BlockSpec((k, tn), lambda i, j: (0, j))],
        out_specs=pl.BlockSpec((tm, tn), lambda i, j: (i, j)),
        out_shape=jax.ShapeDtypeStruct((m, n), BF16), scratch_shapes=[pltpu.VMEM((tm, k), BF16)],
        compiler_params=_params("parallel", "arbitrary"), name=name)(x, g, w)


def norm_matmul2_nt(x, g, wt, wt2, *, n, tm=1024, tn=1024, name):
    m, k = x.shape
    n2 = wt2.shape[0]
    tm, tn = min(tm, m), min(tn, n)
    assert n % tn == 0
    return pl.pallas_call(
        _norm_matmul2_kernel, grid=(m // tm, n // tn),
        in_specs=[pl.BlockSpec((tm, k), lambda i, j: (i, 0)),
                  pl.BlockSpec((1, k), lambda i, j: (0, 0)),
                  pl.BlockSpec((tn, k), lambda i, j: (j, 0)),
                  pl.BlockSpec((n2, k), lambda i, j: (0, 0))],
        out_specs=[pl.BlockSpec((tm, tn), lambda i, j: (i, j)), pl.BlockSpec((tm, n2), lambda i, j: (i, 0))],
        out_shape=[jax.ShapeDtypeStruct((m, n), BF16), jax.ShapeDtypeStruct((m, n2), F32)],
        scratch_shapes=[pltpu.VMEM((tm, k), BF16)],
        compiler_params=_params("parallel", "arbitrary"), name=name)(x, g, wt, wt2)


def _matmul_norm_residual_kernel(y_ref, w_ref, h_ref, g_ref, o_ref):
    def step(first, final):
        d = jnp.dot(y_ref[...], w_ref[...], preferred_element_type=F32)
        a = d if first else o_ref[...] + d
        o_ref[...] = h_ref[...] + a * _rms_scale(a) * g_ref[...] if final else a

    _first_middle_last(step)


def matmul_norm_residual(y, w, h, g, *, tm=1024, tk=1024, name):
    m, k = y.shape
    n = w.shape[1]
    tm, tk = min(tm, m), min(tk, k)
    assert k // tk >= 2
    return pl.pallas_call(
        _matmul_norm_residual_kernel, grid=(m // tm, k // tk),
        in_specs=[pl.BlockSpec((tm, tk), lambda i, kk: (i, kk)),
                  pl.BlockSpec((tk, n), lambda i, kk: (kk, 0)),
                  pl.BlockSpec((tm, n), lambda i, kk: (i, 0)),
                  pl.BlockSpec((1, n), lambda i, kk: (0, 0))],
        out_specs=pl.BlockSpec((tm, n), lambda i, kk: (i, 0)),
        out_shape=jax.ShapeDtypeStruct((m, n), F32),
        compiler_params=_params("parallel", "arbitrary"), name=name)(y, w, h, g)


def _ffn_kernel(h_ref, gpre_ref, wg_ref, wu_ref, wd_ref, gpost_ref, o_ref, xn_ref):
    def step(first, final):
        xn = _normed_rows(h_ref, gpre_ref, xn_ref, first)
        gate = jnp.dot(xn, wg_ref[...], preferred_element_type=F32)
        up = jnp.dot(xn, wu_ref[...], preferred_element_type=F32)
        act = (_silu(gate) * up).astype(BF16)
        d = jnp.dot(act, wd_ref[...], preferred_element_type=F32)
        a = d if first else o_ref[...] + d
        o_ref[...] = h_ref[...] + a * _rms_scale(a) * gpost_ref[...] if final else a

    _first_middle_last(step)


def ffn(h, g_pre, w_gate, w_up, w_down, g_post, layer, *, tm=1024, tf=512, name):
    m, d = h.shape
    dff = w_gate.shape[2]
    tm = min(tm, m)
    assert dff // tf >= 2
    return pl.pallas_call(
        _ffn_kernel, grid=(m // tm, dff // tf),
        in_specs=[pl.BlockSpec((tm, d), lambda i, f: (i, 0)),
                  pl.BlockSpec((1, d), lambda i, f: (0, 0)),
                  pl.BlockSpec((None, d, tf), lambda i, f: (layer, 0, f)),
                  pl.BlockSpec((None, d, tf), lambda i, f: (layer, 0, f)),
                  pl.BlockSpec((None, tf, d), lambda i, f: (layer, f, 0)),
                  pl.BlockSpec((1, d), lambda i, f: (0, 0))],
        out_specs=pl.BlockSpec((tm, d), lambda i, f: (i, 0)),
        out_shape=jax.ShapeDtypeStruct((m, d), F32),
        scratch_shapes=[pltpu.VMEM((tm, d), BF16)],
        compiler_params=_params("parallel", "arbitrary"), name=name)(h, g_pre, w_gate, w_up, w_down, g_post)


def _expand_heads(v, rows):
    lane = lax.broadcasted_iota(jnp.int32, (rows, LANES), 1)
    first = lane < SSM_HEAD_DIM
    pairs = [jnp.where(first, v[:, 2 * p:2 * p + 1], v[:, 2 * p + 1:2 * p + 2]) for p in range(HEADS_PER_GROUP // 2)]
    return jnp.concatenate(pairs, axis=1)


def _split_bf16(v):
    hi = v.astype(BF16)
    lo = (v - hi.astype(F32)).astype(BF16)
    return jnp.concatenate([hi, lo], axis=1)


def _ssd_kernel(z_ref, xs_ref, b_ref, c_ref, dt_ref, cw_ref, cb_ref, dtb_ref, alog_ref, dskip_ref, nw_ref,
                y_ref, state_ref, tail_ref, pad_ref, act_ref, ccol_ref, crowT_ref, e1_ref, wl_ref, elast_ref):
    c = pl.program_id(0)
    g = pl.program_id(1)
    rows = z_ref.shape[0]
    n_sub = rows // SSD_SUB
    sub_row = lax.broadcasted_iota(jnp.int32, (SSD_SUB, SSD_SUB), 0)
    sub_col = lax.broadcasted_iota(jnp.int32, (SSD_SUB, SSD_SUB), 1)
    causal = sub_row >= sub_col

    @pl.when(c == 0)
    def _():
        state_ref[g] = jnp.zeros((SSM_STATE, GROUP_INNER), F32)
        tail_ref[g] = jnp.zeros((SUBLANES, GROUP_CONV), F32)

    @pl.when(g == 0)
    def _():
        dtv = jax.nn.softplus(dt_ref[...] + dtb_ref[...])
        dta = dtv * (-jnp.exp(alog_ref[...]))
        log_dt = jnp.log(dtv)
        tri = causal.astype(F32)
        for s in range(n_sub):
            sl = slice(s * SSD_SUB, (s + 1) * SSD_SUB)
            cum = jnp.dot(tri, dta[sl], precision=lax.Precision.HIGHEST,
                          preferred_element_type=F32)
            clast = cum[SSD_SUB - 1:SSD_SUB, :]
            ccol_ref[sl, :] = cum * LOG2E
            crowT_ref[:, sl] = ((cum - log_dt[sl]) * LOG2E).T
            e1_ref[sl, :] = _split_bf16(jnp.exp(cum))
            wl_ref[sl, :] = _split_bf16(jnp.exp(clast - cum) * dtv[sl])
            elast_ref[s] = jnp.exp(clast)

    xcur = jnp.concatenate([xs_ref[...].astype(F32), b_ref[...].astype(F32), c_ref[...].astype(F32)], axis=1)
    cw = cw_ref[...]
    bias = cb_ref[...]
    pad_ref[0:SUBLANES, :] = tail_ref[g]
    pad_ref[SUBLANES:2 * SUBLANES, :] = xcur[0:SUBLANES]
    tail_ref[g] = xcur[rows - SUBLANES:rows]
    head = bias
    body = bias + cw[CONV_WIDTH - 1:CONV_WIDTH, :] * xcur
    for k in range(CONV_WIDTH):
        lo = SUBLANES - (CONV_WIDTH - 1) + k
        head = head + cw[k:k + 1, :] * pad_ref[lo:lo + SUBLANES, :]
        if k < CONV_WIDTH - 1:
            body = body + cw[k:k + 1, :] * pltpu.roll(xcur, CONV_WIDTH - 1 - k, 0)
    act_ref[...] = _silu(jnp.concatenate([head, body[SUBLANES:]], axis=0))

    shift = lax.rem(LANES - HEADS_PER_GROUP * g, LANES)
    r0 = pl.multiple_of(g * HEADS_PER_GROUP, HEADS_PER_GROUP)
    lane16 = sub_col.astype(F32).astype(BF16)
    low_half = lane16 < SSM_HEAD_DIM
    zero16 = jnp.zeros((SSD_SUB, LANES), BF16)
    sel_row = lax.broadcasted_iota(jnp.int32, (2 * LANES, GROUP_INNER), 0) & (LANES - 1)
    sel_head = lax.broadcasted_iota(jnp.int32, (2 * LANES, GROUP_INNER), 1) // SSM_HEAD_DIM
    expand = (sel_row == sel_head + HEADS_PER_GROUP * g).astype(F32).astype(BF16)

    for s in range(n_sub):
        rs = slice(s * SSD_SUB, (s + 1) * SSD_SUB)
        act = act_ref[rs, :]
        xg = act[:, 0:GROUP_INNER]
        xb = xg.astype(BF16)
        bb = act[:, GROUP_INNER:GROUP_INNER + SSM_STATE].astype(BF16)
        cc = act[:, GROUP_INNER + SSM_STATE:GROUP_CONV].astype(BF16)
        ccol = pltpu.roll(ccol_ref[rs, :], shift, 1)
        crow = crowT_ref[pl.ds(r0, HEADS_PER_GROUP), rs]
        cbm = lax.dot_general(cc, bb, (((1,), (1,)), ((), ())), preferred_element_type=F32).astype(BF16)

        ys = []
        for p in range(HEADS_PER_GROUP // 2):
            ms = []
            for r in (2 * p, 2 * p + 1):
                seg = ccol[:, r:r + 1] - crow[r:r + 1, :]
                ms.append(cbm * jnp.exp2(jnp.where(causal, seg, -jnp.inf)).astype(BF16))
            xp = xb[:, p * LANES:(p + 1) * LANES]
            bd = jnp.concatenate([jnp.where(low_half, xp, zero16), jnp.where(low_half, zero16, xp)], axis=0)
            ys.append(jnp.dot(jnp.concatenate(ms, axis=1), bd, preferred_element_type=F32))
        y = jnp.concatenate(ys, axis=1)

        st = state_ref[g]
        e1 = jnp.dot(e1_ref[rs, :], expand, preferred_element_type=F32)
        y = y + jnp.dot(cc, st.astype(BF16), preferred_element_type=F32) * e1
        wl = jnp.dot(wl_ref[rs, :], expand, preferred_element_type=F32)
        xw = (xg * wl).astype(BF16)
        ds = lax.dot_general(bb, xw, (((0,), (0,)), ((), ())), preferred_element_type=F32)
        el = pltpu.roll(elast_ref[s], shift, 1)
        state_ref[g] = st * _expand_heads(el, 1) + ds

        y = y + dskip_ref[...] * xg
        y = y * _silu(z_ref[rs, :].astype(F32))
        y_ref[rs, :] = (y * _rms_scale(y) * nw_ref[...]).astype(y_ref.dtype)


def ssd_mixer(proj, dt_raw, conv_w_g, conv_b_g, dt_bias, a_log, d_skip, norm_w):
    seq = proj.shape[0]
    rows = min(SSD_ROWS, seq)
    n_sub = rows // SSD_SUB
    xcol0 = D_INNER // GROUP_INNER
    bcol0 = 2 * D_INNER // SSM_STATE
    ccol0 = bcol0 + GN // SSM_STATE
    return pl.pallas_call(
        _ssd_kernel, grid=(seq // rows, SSM_GROUPS),
        in_specs=[pl.BlockSpec((rows, GROUP_INNER), lambda c, g: (c, g)),
                  pl.BlockSpec((rows, GROUP_INNER), lambda c, g: (c, xcol0 + g)),
                  pl.BlockSpec((rows, SSM_STATE), lambda c, g: (c, bcol0 + g)),
                  pl.BlockSpec((rows, SSM_STATE), lambda c, g: (c, ccol0 + g)),
                  pl.BlockSpec((rows, LANES), lambda c, g: (c, 0)),
                  pl.BlockSpec((None, CONV_WIDTH, GROUP_CONV), lambda c, g: (g, 0, 0)),
                  pl.BlockSpec((None, 1, GROUP_CONV), lambda c, g: (g, 0, 0)),
                  pl.BlockSpec((1, LANES), lambda c, g: (0, 0)),
                  pl.BlockSpec((1, LANES), lambda c, g: (0, 0)),
                  pl.BlockSpec((1, GROUP_INNER), lambda c, g: (0, g)),
                  pl.BlockSpec((1, GROUP_INNER), lambda c, g: (0, g))],
        out_specs=pl.BlockSpec((rows, GROUP_INNER), lambda c, g: (c, g)),
        out_shape=jax.ShapeDtypeStruct((seq, D_INNER), BF16),
        scratch_shapes=[pltpu.VMEM((SSM_GROUPS, SSM_STATE, GROUP_INNER), F32),
                        pltpu.VMEM((SSM_GROUPS, SUBLANES, GROUP_CONV), F32),
                        pltpu.VMEM((2 * SUBLANES, GROUP_CONV), F32),
                        pltpu.VMEM((rows, GROUP_CONV), F32),
                        pltpu.VMEM((rows, LANES), F32),
                        pltpu.VMEM((LANES, rows), F32),
                        pltpu.VMEM((rows, 2 * LANES), BF16),
                        pltpu.VMEM((rows, 2 * LANES), BF16),
                        pltpu.VMEM((n_sub, 1, LANES), F32)],
        compiler_params=_params("arbitrary", "arbitrary"), name="ssd_mixer",
    )(proj, proj, proj, proj, dt_raw, conv_w_g, conv_b_g, dt_bias, a_log, d_skip, norm_w)


def _attn_kernel(sink_ref, qt_ref, kvp_ref, kvc_ref, o_ref, bias_ref):
    i = pl.program_id(0)
    group_lanes = Q_PER_KV * BLOCK

    @pl.when(i == 0)
    def _():
        sj = lax.broadcasted_iota(jnp.int32, (2 * BLOCK, group_lanes), 0)
        lane = lax.broadcasted_iota(jnp.int32, (2 * BLOCK, group_lanes), 1)
        dist_i = BLOCK + (lane & (BLOCK - 1)) - sj
        valid = (dist_i >= 0) & (dist_i < WINDOW)
        dist = dist_i.astype(F32)
        head_in_group = (lane // BLOCK).astype(F32)
        for kh in range(N_KV_HEADS):
            slope = jnp.exp2((head_in_group + (kh * Q_PER_KV + 1)) * (-8.0 / N_Q_HEADS))
            b = jnp.where(valid, -(slope * dist), -jnp.inf)
            bias_ref[1, kh] = b
            bias_ref[0, kh] = jnp.where(sj >= BLOCK, b, -jnp.inf)

    has_prev = jnp.minimum(i, 1)
    scale = HEAD_DIM ** -0.5
    kv = jnp.concatenate([kvp_ref[...], kvc_ref[...]], axis=0)
    for kh in range(N_KV_HEADS):
        k = kv[:, kh * HEAD_DIM:(kh + 1) * HEAD_DIM] * scale
        v = kv[:, KV_DIM + kh * HEAD_DIM:KV_DIM + (kh + 1) * HEAD_DIM]
        h0 = kh * Q_PER_KV
        qg = jnp.concatenate([qt_ref[(h0 + r) * HEAD_DIM:(h0 + r + 1) * HEAD_DIM, :] for r in range(Q_PER_KV)],
                             axis=1)
        sink = sink_ref[kh]
        logits = jnp.dot(k, qg, preferred_element_type=F32) + bias_ref[has_prev, kh]
        m = jnp.maximum(jnp.max(logits, axis=0, keepdims=True), sink)
        p = jnp.exp(logits - m)
        denom = jnp.sum(p, axis=0, keepdims=True) + jnp.exp(sink - m)
        ot = lax.dot_general(v, p.astype(BF16), (((0,), (0,)), ((), ())), preferred_element_type=F32) / denom
        for r in range(0, Q_PER_KV, 2):
            pair = jnp.concatenate([ot[:, r * BLOCK:(r + 1) * BLOCK], ot[:, (r + 1) * BLOCK:(r + 2) * BLOCK]], axis=0)
            o_ref[:, (h0 + r) * HEAD_DIM:(h0 + r + 2) * HEAD_DIM] = pair.T.astype(o_ref.dtype)


def swa_attention(qt, kv, sinks):
    seq = kv.shape[0]
    nb = seq // BLOCK
    sink_rows = jnp.repeat(sinks.astype(F32), BLOCK).reshape(N_KV_HEADS, 1, Q_PER_KV * BLOCK)
    return pl.pallas_call(
        _attn_kernel, grid=(nb,),
        in_specs=[pl.BlockSpec((N_KV_HEADS, 1, Q_PER_KV * BLOCK), lambda i: (0, 0, 0)),
                  pl.BlockSpec((N_Q_HEADS * HEAD_DIM, BLOCK), lambda i: (0, i)),
                  pl.BlockSpec((BLOCK, 2 * KV_DIM), lambda i: (jnp.maximum(i - 1, 0), 0)),
                  pl.BlockSpec((BLOCK, 2 * KV_DIM), lambda i: (i, 0))],
        out_specs=pl.BlockSpec((BLOCK, N_Q_HEADS * HEAD_DIM), lambda i: (i, 0)),
        out_shape=jax.ShapeDtypeStruct((seq, N_Q_HEADS * HEAD_DIM), BF16),
        scratch_shapes=[pltpu.VMEM((2, N_KV_HEADS, 2 * BLOCK, Q_PER_KV * BLOCK), F32)],
        compiler_params=_params("arbitrary"), name="swa_attention",
    )(sink_rows, qt, kv, kv)


def _pad_lanes(v, width=LANES):
    v = v.reshape(1, -1).astype(F32)
    return jnp.pad(v, ((0, 0), (0, width - v.shape[1])))


def kernel(x, norm_w, ssm_w_in, ssm_conv_w, ssm_conv_b, ssm_dt_bias, ssm_A_log, ssm_D, ssm_norm_w, ssm_w_out,
           kv_norm_w, w_kv, attn_w_q, attn_sinks, attn_w_o, ffn_w_gate, ffn_w_up, ffn_w_down):
    batch, seq, d = x.shape
    assert batch == 1 and d == D_MODEL and seq % SSD_ROWS == 0
    h = x.reshape(seq, d)
    row = lambda v: v.reshape(1, -1).astype(F32)

    w_gate, w_up, w_down = ffn_w_gate.astype(BF16), ffn_w_up.astype(BF16), ffn_w_down.astype(BF16)

    g = norm_w[0]
    w_in_t = ssm_w_in[0].T.astype(BF16)
    n_main = 2 * D_INNER + 2 * GN
    w_dt_t = jnp.pad(w_in_t[n_main:], ((0, LANES - SSM_HEADS), (0, 0)))
    proj, dt_raw = norm_matmul2_nt(h, row(g[0]), w_in_t, w_dt_t, n=n_main, tn=2048, name="in_proj")

    def per_group(t):
        xs = t[:, :D_INNER].reshape(-1, SSM_GROUPS, GROUP_INNER)
        bs = t[:, D_INNER:D_INNER + GN].reshape(-1, SSM_GROUPS, SSM_STATE)
        cs = t[:, D_INNER + GN:].reshape(-1, SSM_GROUPS, SSM_STATE)
        return jnp.transpose(jnp.concatenate([xs, bs, cs], axis=-1), (1, 0, 2)).astype(F32)

    y = ssd_mixer(proj, dt_raw, per_group(ssm_conv_w[0]), per_group(ssm_conv_b[0].reshape(1, -1)),
                  _pad_lanes(ssm_dt_bias[0]), _pad_lanes(ssm_A_log[0]),
                  row(jnp.repeat(ssm_D[0], SSM_HEAD_DIM)), row(ssm_norm_w[0]))
    h = matmul_norm_residual(y, ssm_w_out[0].astype(BF16), h, row(g[1]), name="ssm_out_proj")
    h = ffn(h, row(g[2]), w_gate, w_up, w_down, row(g[3]), 0, name="ffn0")

    g = norm_w[1]
    kv = norm_matmul(h, row(kv_norm_w), w_kv.astype(BF16), name="kv_proj")
    qt = norm_matmul_t(h, row(g[0]), attn_w_q[0].astype(BF16), name="q_proj")
    a = swa_attention(qt, kv, attn_sinks[0])
    h = matmul_norm_residual(a, attn_w_o[0].astype(BF16), h, row(g[1]), name="attn_out_proj")
    h = ffn(h, row(g[2]), w_gate, w_up, w_down, row(g[3]), 1, name="ffn1")
    return h.reshape(batch, seq, d)
```

```python
import functools
import math

import jax
import jax.numpy as jnp
from jax import lax
from jax.experimental import pallas as pl
from jax.experimental.pallas import tpu as pltpu

D_MODEL = 2048
D_INNER = 4096
SSM_HEAD_DIM = 64
SSM_HEADS = 64
SSM_GROUPS = 8
HEADS_PER_GROUP = SSM_HEADS // SSM_GROUPS
SSM_STATE = 128
CONV_WIDTH = 4
GN = SSM_GROUPS * SSM_STATE
GROUP_INNER = D_INNER // SSM_GROUPS
GROUP_CONV = GROUP_INNER + 2 * SSM_STATE
SSD_SUB = 128
SSD_ROWS = 1024
N_Q_HEADS = 32
N_KV_HEADS = 4
Q_PER_KV = N_Q_HEADS // N_KV_HEADS
HEAD_DIM = 64
WINDOW = 128
BLOCK = 128
KV_DIM = N_KV_HEADS * HEAD_DIM
D_FF = 5632
EPS = 1e-6
LOG2E = math.log2(math.e)

LANES = 128
SUBLANES = 8
VMEM_LIMIT_BYTES = 63 * 1024 * 1024

BF16 = jnp.bfloat16
F32 = jnp.float32


def _rms_scale(x):
    return lax.rsqrt(jnp.mean(x * x, axis=-1, keepdims=True) + EPS)


def _silu(x):
    return x * jax.nn.sigmoid(x)


def _params(*semantics):
    return pltpu.CompilerParams(dimension_semantics=semantics, vmem_limit_bytes=VMEM_LIMIT_BYTES)


def _normed_rows(x_ref, g_ref, xn_ref, first):
    if not first:
        return xn_ref[...]
    x = x_ref[...]
    xn = (x * _rms_scale(x) * g_ref[...]).astype(xn_ref.dtype)
    xn_ref[...] = xn
    return xn


def _first_and_rest(step):
    j = pl.program_id(1)
    pl.when(j == 0)(lambda: step(True))
    pl.when(j > 0)(lambda: step(False))


def _first_middle_last(step):
    j = pl.program_id(1)
    last = pl.num_programs(1) - 1
    pl.when(j == 0)(lambda: step(True, False))
    pl.when(jnp.logical_and(j > 0, j < last))(lambda: step(False, False))
    pl.when(j == last)(lambda: step(False, True))


def _norm_matmul_kernel(x_ref, g_ref, w_ref, o_ref, xn_ref):
    def step(first):
        xn = _normed_rows(x_ref, g_ref, xn_ref, first)
        o_ref[...] = jnp.dot(xn, w_ref[...], preferred_element_type=F32).astype(o_ref.dtype)

    _first_and_rest(step)


def _norm_matmul2_kernel(x_ref, g_ref, wt_ref, wt2_ref, o_ref, o2_ref, xn_ref):
    nt = (((1,), (1,)), ((), ()))

    def step(first):
        xn = _normed_rows(x_ref, g_ref, xn_ref, first)
        if first:
            o2_ref[...] = lax.dot_general(xn, wt2_ref[...], nt, preferred_element_type=F32)
        o_ref[...] = lax.dot_general(xn, wt_ref[...], nt, preferred_element_type=F32).astype(o_ref.dtype)

    _first_and_rest(step)


def _norm_matmul_t_kernel(x_ref, g_ref, w_ref, o_ref, xn_ref):
    def step(first):
        xn = _normed_rows(x_ref, g_ref, xn_ref, first)
        o_ref[...] = lax.dot_general(w_ref[...], xn, (((0,), (1,)), ((), ())),
                                     preferred_element_type=F32).astype(o_ref.dtype)

    _first_and_rest(step)


def norm_matmul_t(x, g, w, *, tm=1024, tn=1024, name):
    m, k = x.shape
    n = w.shape[1]
    tm, tn = min(tm, m), min(tn, n)
    return pl.pallas_call(
        _norm_matmul_t_kernel, grid=(m // tm, n // tn),
        in_specs=[pl.BlockSpec((tm, k), lambda i, j: (i, 0)),
                  pl.BlockSpec((1, k), lambda i, j: (0, 0)),
                  pl.BlockSpec((k, tn), lambda i, j: (0, j))],
        out_specs=pl.BlockSpec((tn, tm), lambda i, j: (j, i)),
        out_shape=jax.ShapeDtypeStruct((n, m), BF16), scratch_shapes=[pltpu.VMEM((tm, k), BF16)],
        compiler_params=_params("parallel", "arbitrary"), name=name)(x, g, w)


def norm_matmul(x, g, w, *, tm=1024, tn=1024, name):
    m, k = x.shape
    n = w.shape[1]
    tm, tn = min(tm, m), min(tn, n)
    return pl.pallas_call(
        _norm_matmul_kernel, grid=(m // tm, n // tn),
        in_specs=[pl.BlockSpec((tm, k), lambda i, j: (i, 0)),
                  pl.BlockSpec((1, k), lambda i, j: (0, 0)),
                  pl.BlockSpec((k, tn), lambda i, j: (0, j))],
        out_specs=pl.BlockSpec((tm, tn), lambda i, j: (i, j)),
        out_shape=jax.ShapeDtypeStruct((m, n), BF16), scratch_shapes=[pltpu.VMEM((tm, k), BF16)],
        compiler_params=_params("parallel", "arbitrary"), name=name)(x, g, w)


def norm_matmul2_nt(x, g, wt, wt2, *, n, tm=1024, tn=1024, name):
    m, k = x.shape
    n2 = wt2.shape[0]
    tm, tn = min(tm, m), min(tn, n)
    assert n % tn == 0
    return pl.pallas_call(
        _norm_matmul2_kernel, grid=(m // tm, n // tn),
        in_specs=[pl.BlockSpec((tm, k), lambda i, j: (i, 0)),
                  pl.BlockSpec((1, k), lambda i, j: (0, 0)),
                  pl.BlockSpec((tn, k), lambda i, j: (j, 0)),
                  pl.BlockSpec((n2, k), lambda i, j: (0, 0))],
        out_specs=[pl.BlockSpec((tm, tn), lambda i, j: (i, j)), pl.BlockSpec((tm, n2), lambda i, j: (i, 0))],
        out_shape=[jax.ShapeDtypeStruct((m, n), BF16), jax.ShapeDtypeStruct((m, n2), F32)],
        scratch_shapes=[pltpu.VMEM((tm, k), BF16)],
        compiler_params=_params("parallel", "arbitrary"), name=name)(x, g, wt, wt2)


def _matmul_norm_residual_kernel(y_ref, w_ref, h_ref, g_ref, o_ref):
    def step(first, final):
        d = jnp.dot(y_ref[...], w_ref[...], preferred_element_type=F32)
        a = d if first else o_ref[...] + d
        o_ref[...] = h_ref[...] + a * _rms_scale(a) * g_ref[...] if final else a

    _first_middle_last(step)


def matmul_norm_residual(y, w, h, g, *, tm=1024, tk=1024, name):
    m, k = y.shape
    n = w.shape[1]
    tm, tk = min(tm, m), min(tk, k)
    assert k // tk >= 2
    return pl.pallas_call(
        _matmul_norm_residual_kernel, grid=(m // tm, k // tk),
        in_specs=[pl.BlockSpec((tm, tk), lambda i, kk: (i, kk)),
                  pl.BlockSpec((tk, n), lambda i, kk: (kk, 0)),
                  pl.BlockSpec((tm, n), lambda i, kk: (i, 0)),
                  pl.BlockSpec((1, n), lambda i, kk: (0, 0))],
        out_specs=pl.BlockSpec((tm, n), lambda i, kk: (i, 0)),
        out_shape=jax.ShapeDtypeStruct((m, n), F32),
        compiler_params=_params("parallel", "arbitrary"), name=name)(y, w, h, g)


def _ffn_kernel(h_ref, gpre_ref, wg_hbm, wu_hbm, wd_hbm, gpost_ref, o_ref, xn_ref, wg_buf, wu_buf, wd_buf, sems,
                *, layer, tf, n_chunks):
    i = pl.program_id(0)
    n_blocks = pl.num_programs(0)

    def chunk_copies(f, slot):
        cols = pl.ds(pl.multiple_of(f * tf, tf), tf)
        return (pltpu.make_async_copy(wg_hbm.at[layer, :, cols], wg_buf.at[slot], sems.at[0, slot]),
                pltpu.make_async_copy(wu_hbm.at[layer, :, cols], wu_buf.at[slot], sems.at[1, slot]),
                pltpu.make_async_copy(wd_hbm.at[layer, cols, :], wd_buf.at[slot], sems.at[2, slot]))

    @pl.when(i == 0)
    def _():
        for cp in chunk_copies(0, 0):
            cp.start()

    def step(f, first, final):
        slot = lax.rem(i * n_chunks + f, 2)
        for cp in chunk_copies(f, slot):
            cp.wait()
        if final:
            @pl.when(i + 1 < n_blocks)
            def _():
                for cp in chunk_copies(0, 1 - slot):
                    cp.start()
        else:
            for cp in chunk_copies(f + 1, 1 - slot):
                cp.start()
        xn = _normed_rows(h_ref, gpre_ref, xn_ref, first)
        gate = jnp.dot(xn, wg_buf[slot], preferred_element_type=F32)
        up = jnp.dot(xn, wu_buf[slot], preferred_element_type=F32)
        act = (_silu(gate) * up).astype(BF16)
        d = jnp.dot(act, wd_buf[slot], preferred_element_type=F32)
        a = d if first else o_ref[...] + d
        o_ref[...] = h_ref[...] + a * _rms_scale(a) * gpost_ref[...] if final else a

    step(0, True, False)
    lax.fori_loop(1, n_chunks - 1, lambda f, c: (step(f, False, False), c)[1], 0)
    step(n_chunks - 1, False, True)


def ffn(h, g_pre, w_gate, w_up, w_down, g_post, layer, *, tm=1024, tf=512, name):
    m, d = h.shape
    dff = w_gate.shape[2]
    tm = min(tm, m)
    n_chunks = dff // tf
    assert n_chunks >= 3
    return pl.pallas_call(
        functools.partial(_ffn_kernel, layer=layer, tf=tf, n_chunks=n_chunks), grid=(m // tm,),
        in_specs=[pl.BlockSpec((tm, d), lambda i: (i, 0)),
                  pl.BlockSpec((1, d), lambda i: (0, 0)),
                  pl.BlockSpec(memory_space=pl.ANY),
                  pl.BlockSpec(memory_space=pl.ANY),
                  pl.BlockSpec(memory_space=pl.ANY),
                  pl.BlockSpec((1, d), lambda i: (0, 0))],
        out_specs=pl.BlockSpec((tm, d), lambda i: (i, 0)),
        out_shape=jax.ShapeDtypeStruct((m, d), F32),
        scratch_shapes=[pltpu.VMEM((tm, d), BF16),
                        pltpu.VMEM((2, d, tf), BF16), pltpu.VMEM((2, d, tf), BF16), pltpu.VMEM((2, tf, d), BF16),
                        pltpu.SemaphoreType.DMA((3, 2))],
        compiler_params=_params("arbitrary"), name=name)(h, g_pre, w_gate, w_up, w_down, g_post)


def _expand_heads(v, rows):
    lane = lax.broadcasted_iota(jnp.int32, (rows, LANES), 1)
    first = lane < SSM_HEAD_DIM
    pairs = [jnp.where(first, v[:, 2 * p:2 * p + 1], v[:, 2 * p + 1:2 * p + 2]) for p in range(HEADS_PER_GROUP // 2)]
    return jnp.concatenate(pairs, axis=1)


def _split_bf16(v):
    hi = v.astype(BF16)
    lo = (v - hi.astype(F32)).astype(BF16)
    return jnp.concatenate([hi, lo], axis=1)


def _ssd_kernel(z_ref, xs_ref, b_ref, c_ref, dt_ref, cw_ref, cb_ref, dtb_ref, alog_ref, dskip_ref, nw_ref,
                y_ref, state_ref, tail_ref, pad_ref, act_ref, ccol_ref, crowT_ref, e1_ref, wl_ref, elast_ref):
    c = pl.program_id(0)
    g = pl.program_id(1)
    rows = z_ref.shape[0]
    n_sub = rows // SSD_SUB
    sub_row = lax.broadcasted_iota(jnp.int32, (SSD_SUB, SSD_SUB), 0)
    sub_col = lax.broadcasted_iota(jnp.int32, (SSD_SUB, SSD_SUB), 1)
    causal = sub_row >= sub_col

    @pl.when(c == 0)
    def _():
        state_ref[g] = jnp.zeros((SSM_STATE, GROUP_INNER), F32)
        tail_ref[g] = jnp.zeros((SUBLANES, GROUP_CONV), F32)

    @pl.when(g == 0)
    def _():
        dtv = jax.nn.softplus(dt_ref[...] + dtb_ref[...])
        dta = dtv * (-jnp.exp(alog_ref[...]))
        log_dt = jnp.log(dtv)
        tri = causal.astype(F32)
        for s in range(n_sub):
            sl = slice(s * SSD_SUB, (s + 1) * SSD_SUB)
            cum = jnp.dot(tri, dta[sl], precision=lax.Precision.HIGHEST,
                          preferred_element_type=F32)
            clast = cum[SSD_SUB - 1:SSD_SUB, :]
            ccol_ref[sl, :] = cum * LOG2E
            crowT_ref[:, sl] = ((cum - log_dt[sl]) * LOG2E).T
            e1_ref[sl, :] = _split_bf16(jnp.exp(cum))
            wl_ref[sl, :] = _split_bf16(jnp.exp(clast - cum) * dtv[sl])
            elast_ref[s] = jnp.exp(clast)

    xcur = jnp.concatenate([xs_ref[...].astype(F32), b_ref[...].astype(F32), c_ref[...].astype(F32)], axis=1)
    cw = cw_ref[...]
    bias = cb_ref[...]
    pad_ref[0:SUBLANES, :] = tail_ref[g]
    pad_ref[SUBLANES:2 * SUBLANES, :] = xcur[0:SUBLANES]
    tail_ref[g] = xcur[rows - SUBLANES:rows]
    head = bias
    body = bias + cw[CONV_WIDTH - 1:CONV_WIDTH, :] * xcur
    for k in range(CONV_WIDTH):
        lo = SUBLANES - (CONV_WIDTH - 1) + k
        head = head + cw[k:k + 1, :] * pad_ref[lo:lo + SUBLANES, :]
        if k < CONV_WIDTH - 1:
            body = body + cw[k:k + 1, :] * pltpu.roll(xcur, CONV_WIDTH - 1 - k, 0)
    act_ref[...] = _silu(jnp.concatenate([head, body[SUBLANES:]], axis=0))

    shift = lax.rem(LANES - HEADS_PER_GROUP * g, LANES)
    r0 = pl.multiple_of(g * HEADS_PER_GROUP, HEADS_PER_GROUP)
    lane16 = sub_col.astype(F32).astype(BF16)
    low_half = lane16 < SSM_HEAD_DIM
    zero16 = jnp.zeros((SSD_SUB, LANES), BF16)
    sel_row = lax.broadcasted_iota(jnp.int32, (2 * LANES, GROUP_INNER), 0) & (LANES - 1)
    sel_head = lax.broadcasted_iota(jnp.int32, (2 * LANES, GROUP_INNER), 1) // SSM_HEAD_DIM
    expand = (sel_row == sel_head + HEADS_PER_GROUP * g).astype(F32).astype(BF16)

    for s in range(n_sub):
        rs = slice(s * SSD_SUB, (s + 1) * SSD_SUB)
        act = act_ref[rs, :]
        xg = act[:, 0:GROUP_INNER]
        xb = xg.astype(BF16)
        bb = act[:, GROUP_INNER:GROUP_INNER + SSM_STATE].astype(BF16)
        cc = act[:, GROUP_INNER + SSM_STATE:GROUP_CONV].astype(BF16)
        ccol = pltpu.roll(ccol_ref[rs, :], shift, 1)
        crow = crowT_ref[pl.ds(r0, HEADS_PER_GROUP), rs]
        cbm = lax.dot_general(cc, bb, (((1,), (1,)), ((), ())), preferred_element_type=F32).astype(BF16)

        ys = []
        for p in range(HEADS_PER_GROUP // 2):
            ms = []
            for r in (2 * p, 2 * p + 1):
                seg = ccol[:, r:r + 1] - crow[r:r + 1, :]
                ms.append(cbm * jnp.exp2(jnp.where(causal, seg, -jnp.inf)).astype(BF16))
            xp = xb[:, p * LANES:(p + 1) * LANES]
            bd = jnp.concatenate([jnp.where(low_half, xp, zero16), jnp.where(low_half, zero16, xp)], axis=0)
            ys.append(jnp.dot(jnp.concatenate(ms, axis=1), bd, preferred_element_type=F32))
        y = jnp.concatenate(ys, axis=1)

        st = state_ref[g]
        e1 = jnp.dot(e1_ref[rs, :], expand, preferred_element_type=F32)
        y = y + jnp.dot(cc, st.astype(BF16), preferred_element_type=F32) * e1
        wl = jnp.dot(wl_ref[rs, :], expand, preferred_element_type=F32)
        xw = (xg * wl).astype(BF16)
        ds = lax.dot_general(bb, xw, (((0,), (0,)), ((), ())), preferred_element_type=F32)
        el = pltpu.roll(elast_ref[s], shift, 1)
        state_ref[g] = st * _expand_heads(el, 1) + ds

        y = y + dskip_ref[...] * xg
        y = y * _silu(z_ref[rs, :].astype(F32))
        y_ref[rs, :] = (y * _rms_scale(y) * nw_ref[...]).astype(y_ref.dtype)


def ssd_mixer(proj, dt_raw, conv_w_g, conv_b_g, dt_bias, a_log, d_skip, norm_w):
    seq = proj.shape[0]
    rows = min(SSD_ROWS, seq)
    n_sub = rows // SSD_SUB
    xcol0 = D_INNER // GROUP_INNER
    bcol0 = 2 * D_INNER // SSM_STATE
    ccol0 = bcol0 + GN // SSM_STATE
    return pl.pallas_call(
        _ssd_kernel, grid=(seq // rows, SSM_GROUPS),
        in_specs=[pl.BlockSpec((rows, GROUP_INNER), lambda c, g: (c, g)),
                  pl.BlockSpec((rows, GROUP_INNER), lambda c, g: (c, xcol0 + g)),
                  pl.BlockSpec((rows, SSM_STATE), lambda c, g: (c, bcol0 + g)),
                  pl.BlockSpec((rows, SSM_STATE), lambda c, g: (c, ccol0 + g)),
                  pl.BlockSpec((rows, LANES), lambda c, g: (c, 0)),
                  pl.BlockSpec((None, CONV_WIDTH, GROUP_CONV), lambda c, g: (g, 0, 0)),
                  pl.BlockSpec((None, 1, GROUP_CONV), lambda c, g: (g, 0, 0)),
                  pl.BlockSpec((1, LANES), lambda c, g: (0, 0)),
                  pl.BlockSpec((1, LANES), lambda c, g: (0, 0)),
                  pl.BlockSpec((1, GROUP_INNER), lambda c, g: (0, g)),
                  pl.BlockSpec((1, GROUP_INNER), lambda c, g: (0, g))],
        out_specs=pl.BlockSpec((rows, GROUP_INNER), lambda c, g: (c, g)),
        out_shape=jax.ShapeDtypeStruct((seq, D_INNER), BF16),
        scratch_shapes=[pltpu.VMEM((SSM_GROUPS, SSM_STATE, GROUP_INNER), F32),
                        pltpu.VMEM((SSM_GROUPS, SUBLANES, GROUP_CONV), F32),
                        pltpu.VMEM((2 * SUBLANES, GROUP_CONV), F32),
                        pltpu.VMEM((rows, GROUP_CONV), F32),
                        pltpu.VMEM((rows, LANES), F32),
                        pltpu.VMEM((LANES, rows), F32),
                        pltpu.VMEM((rows, 2 * LANES), BF16),
                        pltpu.VMEM((rows, 2 * LANES), BF16),
                        pltpu.VMEM((n_sub, 1, LANES), F32)],
        compiler_params=_params("arbitrary", "arbitrary"), name="ssd_mixer",
    )(proj, proj, proj, proj, dt_raw, conv_w_g, conv_b_g, dt_bias, a_log, d_skip, norm_w)


def _attn_kernel(sink_ref, qt_ref, kvp_ref, kvc_ref, o_ref, bias_ref):
    i = pl.program_id(0)
    group_lanes = Q_PER_KV * BLOCK

    @pl.when(i == 0)
    def _():
        sj = lax.broadcasted_iota(jnp.int32, (2 * BLOCK, group_lanes), 0)
        lane = lax.broadcasted_iota(jnp.int32, (2 * BLOCK, group_lanes), 1)
        dist_i = BLOCK + (lane & (BLOCK - 1)) - sj
        valid = (dist_i >= 0) & (dist_i < WINDOW)
        dist = dist_i.astype(F32)
        head_in_group = (lane // BLOCK).astype(F32)
        for kh in range(N_KV_HEADS):
            slope = jnp.exp2((head_in_group + (kh * Q_PER_KV + 1)) * (-8.0 / N_Q_HEADS))
            b = jnp.where(valid, -(slope * dist), -jnp.inf)
            bias_ref[1, kh] = b
            bias_ref[0, kh] = jnp.where(sj >= BLOCK, b, -jnp.inf)

    has_prev = jnp.minimum(i, 1)
    scale = HEAD_DIM ** -0.5
    kv = jnp.concatenate([kvp_ref[...], kvc_ref[...]], axis=0)
    for kh in range(N_KV_HEADS):
        k = kv[:, kh * HEAD_DIM:(kh + 1) * HEAD_DIM] * scale
        v = kv[:, KV_DIM + kh * HEAD_DIM:KV_DIM + (kh + 1) * HEAD_DIM]
        h0 = kh * Q_PER_KV
        qg = jnp.concatenate([qt_ref[(h0 + r) * HEAD_DIM:(h0 + r + 1) * HEAD_DIM, :] for r in range(Q_PER_KV)],
                             axis=1)
        sink = sink_ref[kh]
        logits = jnp.dot(k, qg, preferred_element_type=F32) + bias_ref[has_prev, kh]
        m = jnp.maximum(jnp.max(logits, axis=0, keepdims=True), sink)
        p = jnp.exp(logits - m)
        denom = jnp.sum(p, axis=0, keepdims=True) + jnp.exp(sink - m)
        ot = lax.dot_general(v, p.astype(BF16), (((0,), (0,)), ((), ())), preferred_element_type=F32) / denom
        for r in range(0, Q_PER_KV, 2):
            pair = jnp.concatenate([ot[:, r * BLOCK:(r + 1) * BLOCK], ot[:, (r + 1) * BLOCK:(r + 2) * BLOCK]], axis=0)
            o_ref[:, (h0 + r) * HEAD_DIM:(h0 + r + 2) * HEAD_DIM] = pair.T.astype(o_ref.dtype)


def swa_attention(qt, kv, sinks):
    seq = kv.shape[0]
    nb = seq // BLOCK
    sink_rows = jnp.repeat(sinks.astype(F32), BLOCK).reshape(N_KV_HEADS, 1, Q_PER_KV * BLOCK)
    return pl.pallas_call(
        _attn_kernel, grid=(nb,),
        in_specs=[pl.BlockSpec((N_KV_HEADS, 1, Q_PER_KV * BLOCK), lambda i: (0, 0, 0)),
                  pl.BlockSpec((N_Q_HEADS * HEAD_DIM, BLOCK), lambda i: (0, i)),
                  pl.BlockSpec((BLOCK, 2 * KV_DIM), lambda i: (jnp.maximum(i - 1, 0), 0)),
                  pl.BlockSpec((BLOCK, 2 * KV_DIM), lambda i: (i, 0))],
        out_specs=pl.BlockSpec((BLOCK, N_Q_HEADS * HEAD_DIM), lambda i: (i, 0)),
        out_shape=jax.ShapeDtypeStruct((seq, N_Q_HEADS * HEAD_DIM), BF16),
        scratch_shapes=[pltpu.VMEM((2, N_KV_HEADS, 2 * BLOCK, Q_PER_KV * BLOCK), F32)],
        compiler_params=_params("arbitrary"), name="swa_attention",
    )(sink_rows, qt, kv, kv)


def _pad_lanes(v, width=LANES):
    v = v.reshape(1, -1).astype(F32)
    return jnp.pad(v, ((0, 0), (0, width - v.shape[1])))


def kernel(x, norm_w, ssm_w_in, ssm_conv_w, ssm_conv_b, ssm_dt_bias, ssm_A_log, ssm_D, ssm_norm_w, ssm_w_out,
           kv_norm_w, w_kv, attn_w_q, attn_sinks, attn_w_o, ffn_w_gate, ffn_w_up, ffn_w_down):
    batch, seq, d = x.shape
    assert batch == 1 and d == D_MODEL and seq % SSD_ROWS == 0
    h = x.reshape(seq, d)
    row = lambda v: v.reshape(1, -1).astype(F32)

    w_gate, w_up, w_down = ffn_w_gate.astype(BF16), ffn_w_up.astype(BF16), ffn_w_down.astype(BF16)

    g = norm_w[0]
    w_in_t = ssm_w_in[0].T.astype(BF16)
    n_main = 2 * D_INNER + 2 * GN
    w_dt_t = jnp.pad(w_in_t[n_main:], ((0, LANES - SSM_HEADS), (0, 0)))
    proj, dt_raw = norm_matmul2_nt(h, row(g[0]), w_in_t, w_dt_t, n=n_main, tn=2048, name="in_proj")

    def per_group(t):
        xs = t[:, :D_INNER].reshape(-1, SSM_GROUPS, GROUP_INNER)
        bs = t[:, D_INNER:D_INNER + GN].reshape(-1, SSM_GROUPS, SSM_STATE)
        cs = t[:, D_INNER + GN:].reshape(-1, SSM_GROUPS, SSM_STATE)
        return jnp.transpose(jnp.concatenate([xs, bs, cs], axis=-1), (1, 0, 2)).astype(F32)

    y = ssd_mixer(proj, dt_raw, per_group(ssm_conv_w[0]), per_group(ssm_conv_b[0].reshape(1, -1)),
                  _pad_lanes(ssm_dt_bias[0]), _pad_lanes(ssm_A_log[0]),
                  row(jnp.repeat(ssm_D[0], SSM_HEAD_DIM)), row(ssm_norm_w[0]))
    h = matmul_norm_residual(y, ssm_w_out[0].astype(BF16), h, row(g[1]), name="ssm_out_proj")
    h = ffn(h, row(g[2]), w_gate, w_up, w_down, row(g[3]), 0, name="ffn0")

    g = norm_w[1]
    kv = norm_matmul(h, row(kv_norm_w), w_kv.astype(BF16), name="kv_proj")
    qt = norm_matmul_t(h, row(g[0]), attn_w_q[0].astype(BF16), name="q_proj")
    a = swa_attention(qt, kv, attn_sinks[0])
    h = matmul_norm_residual(a, attn_w_o[0].astype(BF16), h, row(g[1]), name="attn_out_proj")
    h = ffn(h, row(g[2]), w_gate, w_up, w_down, row(g[3]), 1, name="ffn1")
    return h.reshape(batch, seq, d)
```

```python
import math

import jax
import jax.numpy as jnp
from jax import lax
from jax.experimental import pallas as pl
from jax.experimental.pallas import tpu as pltpu

D_MODEL = 2048
D_INNER = 4096
SSM_HEAD_DIM = 64
SSM_HEADS = 64
SSM_GROUPS = 8
HEADS_PER_GROUP = SSM_HEADS // SSM_GROUPS
SSM_STATE = 128
CONV_WIDTH = 4
GN = SSM_GROUPS * SSM_STATE
GROUP_INNER = D_INNER // SSM_GROUPS
GROUP_CONV = GROUP_INNER + 2 * SSM_STATE
SSD_SUB = 128
SSD_ROWS = 1024
N_Q_HEADS = 32
N_KV_HEADS = 4
Q_PER_KV = N_Q_HEADS // N_KV_HEADS
HEAD_DIM = 64
WINDOW = 128
BLOCK = 128
KV_DIM = N_KV_HEADS * HEAD_DIM
EPS = 1e-6
LOG2E = math.log2(math.e)

LANES = 128
SUBLANES = 8
VMEM_LIMIT_BYTES = 63 * 1024 * 1024

ROW_TILE = 1024
COL_TILE = 1024
IN_PROJ_COL_TILE = 2048
K_TILE = 1024
FFN_CHUNK = 512

BF16 = jnp.bfloat16
F32 = jnp.float32


def _rms_scale(x):
    return lax.rsqrt(jnp.mean(x * x, axis=-1, keepdims=True) + EPS)


def _silu(x):
    return x * jax.nn.sigmoid(x)


def _params(*semantics):
    return pltpu.CompilerParams(dimension_semantics=semantics, vmem_limit_bytes=VMEM_LIMIT_BYTES)


def _normed_rows(x_ref, g_ref, xn_ref, first):
    if not first:
        return xn_ref[...]
    x = x_ref[...]
    xn = (x * _rms_scale(x) * g_ref[...]).astype(xn_ref.dtype)
    xn_ref[...] = xn
    return xn


def _first_and_rest(step):
    j = pl.program_id(1)
    pl.when(j == 0)(lambda: step(True))
    pl.when(j > 0)(lambda: step(False))


def _first_middle_last(step):
    j = pl.program_id(1)
    last = pl.num_programs(1) - 1
    pl.when(j == 0)(lambda: step(True, False))
    pl.when(jnp.logical_and(j > 0, j < last))(lambda: step(False, False))
    pl.when(j == last)(lambda: step(False, True))


def _norm_matmul_kernel(x_ref, g_ref, w_ref, o_ref, xn_ref):
    def step(first):
        xn = _normed_rows(x_ref, g_ref, xn_ref, first)
        o_ref[...] = jnp.dot(xn, w_ref[...], preferred_element_type=F32).astype(o_ref.dtype)

    _first_and_rest(step)


def _norm_matmul2_kernel(x_ref, g_ref, wt_ref, wt2_ref, o_ref, o2_ref, xn_ref):
    nt = (((1,), (1,)), ((), ()))

    def step(first):
        xn = _normed_rows(x_ref, g_ref, xn_ref, first)
        if first:
            o2_ref[...] = lax.dot_general(xn, wt2_ref[...], nt, preferred_element_type=F32)
        o_ref[...] = lax.dot_general(xn, wt_ref[...], nt, preferred_element_type=F32).astype(o_ref.dtype)

    _first_and_rest(step)


def _norm_matmul_t_kernel(x_ref, g_ref, w_ref, o_ref, xn_ref):
    def step(first):
        xn = _normed_rows(x_ref, g_ref, xn_ref, first)
        o_ref[...] = lax.dot_general(w_ref[...], xn, (((0,), (1,)), ((), ())),
                                     preferred_element_type=F32).astype(o_ref.dtype)

    _first_and_rest(step)


def norm_matmul_t(x, g, w, *, name):
    m, k = x.shape
    n = w.shape[1]
    tm, tn = min(ROW_TILE, m), min(COL_TILE, n)
    return pl.pallas_call(
        _norm_matmul_t_kernel, grid=(m // tm, n // tn),
        in_specs=[pl.BlockSpec((tm, k), lambda i, j: (i, 0)),
                  pl.BlockSpec((1, k), lambda i, j: (0, 0)),
                  pl.BlockSpec((k, tn), lambda i, j: (0, j))],
        out_specs=pl.BlockSpec((tn, tm), lambda i, j: (j, i)),
        out_shape=jax.ShapeDtypeStruct((n, m), BF16), scratch_shapes=[pltpu.VMEM((tm, k), BF16)],
        compiler_params=_params("parallel", "arbitrary"), name=name)(x, g, w)


def norm_matmul(x, g, w, *, name):
    m, k = x.shape
    n = w.shape[1]
    tm, tn = min(ROW_TILE, m), min(COL_TILE, n)
    return pl.pallas_call(
        _norm_matmul_kernel, grid=(m // tm, n // tn),
        in_specs=[pl.BlockSpec((tm, k), lambda i, j: (i, 0)),
                  pl.BlockSpec((1, k), lambda i, j: (0, 0)),
                  pl.BlockSpec((k, tn), lambda i, j: (0, j))],
        out_specs=pl.BlockSpec((tm, tn), lambda i, j: (i, j)),
        out_shape=jax.ShapeDtypeStruct((m, n), BF16), scratch_shapes=[pltpu.VMEM((tm, k), BF16)],
        compiler_params=_params("parallel", "arbitrary"), name=name)(x, g, w)


def norm_matmul2_nt(x, g, wt, wt2, *, n, name):
    m, k = x.shape
    n2 = wt2.shape[0]
    tm, tn = min(ROW_TILE, m), min(IN_PROJ_COL_TILE, n)
    assert n % tn == 0
    return pl.pallas_call(
        _norm_matmul2_kernel, grid=(m // tm, n // tn),
        in_specs=[pl.BlockSpec((tm, k), lambda i, j: (i, 0)),
                  pl.BlockSpec((1, k), lambda i, j: (0, 0)),
                  pl.BlockSpec((tn, k), lambda i, j: (j, 0)),
                  pl.BlockSpec((n2, k), lambda i, j: (0, 0))],
        out_specs=[pl.BlockSpec((tm, tn), lambda i, j: (i, j)), pl.BlockSpec((tm, n2), lambda i, j: (i, 0))],
        out_shape=[jax.ShapeDtypeStruct((m, n), BF16), jax.ShapeDtypeStruct((m, n2), F32)],
        scratch_shapes=[pltpu.VMEM((tm, k), BF16)],
        compiler_params=_params("parallel", "arbitrary"), name=name)(x, g, wt, wt2)


def _matmul_norm_residual_kernel(y_ref, w_ref, h_ref, g_ref, o_ref):
    def step(first, final):
        d = jnp.dot(y_ref[...], w_ref[...], preferred_element_type=F32)
        a = d if first else o_ref[...] + d
        o_ref[...] = h_ref[...] + a * _rms_scale(a) * g_ref[...] if final else a

    _first_middle_last(step)


def matmul_norm_residual(y, w, h, g, *, name):
    m, k = y.shape
    n = w.shape[1]
    tm, tk = min(ROW_TILE, m), min(K_TILE, k)
    assert k // tk >= 2
    return pl.pallas_call(
        _matmul_norm_residual_kernel, grid=(m // tm, k // tk),
        in_specs=[pl.BlockSpec((tm, tk), lambda i, kk: (i, kk)),
                  pl.BlockSpec((tk, n), lambda i, kk: (kk, 0)),
                  pl.BlockSpec((tm, n), lambda i, kk: (i, 0)),
                  pl.BlockSpec((1, n), lambda i, kk: (0, 0))],
        out_specs=pl.BlockSpec((tm, n), lambda i, kk: (i, 0)),
        out_shape=jax.ShapeDtypeStruct((m, n), F32),
        compiler_params=_params("parallel", "arbitrary"), name=name)(y, w, h, g)


def _ffn_kernel(h_ref, gpre_ref, wg_ref, wu_ref, wd_ref, gpost_ref, o_ref, xn_ref):
    def step(first, final):
        xn = _normed_rows(h_ref, gpre_ref, xn_ref, first)
        gate = jnp.dot(xn, wg_ref[...], preferred_element_type=F32)
        up = jnp.dot(xn, wu_ref[...], preferred_element_type=F32)
        act = (_silu(gate) * up).astype(BF16)
        d = jnp.dot(act, wd_ref[...], preferred_element_type=F32)
        a = d if first else o_ref[...] + d
        o_ref[...] = h_ref[...] + a * _rms_scale(a) * gpost_ref[...] if final else a

    _first_middle_last(step)


def ffn(h, g_pre, w_gate, w_up, w_down, g_post, layer, *, name):
    m, d = h.shape
    dff = w_gate.shape[2]
    tm, tf = min(ROW_TILE, m), FFN_CHUNK
    assert dff // tf >= 2
    return pl.pallas_call(
        _ffn_kernel, grid=(m // tm, dff // tf),
        in_specs=[pl.BlockSpec((tm, d), lambda i, f: (i, 0)),
                  pl.BlockSpec((1, d), lambda i, f: (0, 0)),
                  pl.BlockSpec((None, d, tf), lambda i, f: (layer, 0, f)),
                  pl.BlockSpec((None, d, tf), lambda i, f: (layer, 0, f)),
                  pl.BlockSpec((None, tf, d), lambda i, f: (layer, f, 0)),
                  pl.BlockSpec((1, d), lambda i, f: (0, 0))],
        out_specs=pl.BlockSpec((tm, d), lambda i, f: (i, 0)),
        out_shape=jax.ShapeDtypeStruct((m, d), F32),
        scratch_shapes=[pltpu.VMEM((tm, d), BF16)],
        compiler_params=_params("parallel", "arbitrary"), name=name)(h, g_pre, w_gate, w_up, w_down, g_post)


def _expand_heads(v, rows):
    lane = lax.broadcasted_iota(jnp.int32, (rows, LANES), 1)
    first = lane < SSM_HEAD_DIM
    pairs = [jnp.where(first, v[:, 2 * p:2 * p + 1], v[:, 2 * p + 1:2 * p + 2]) for p in range(HEADS_PER_GROUP // 2)]
    return jnp.concatenate(pairs, axis=1)


def _split_bf16(v):
    hi = v.astype(BF16)
    lo = (v - hi.astype(F32)).astype(BF16)
    return jnp.concatenate([hi, lo], axis=1)


def _ssd_kernel(z_ref, xs_ref, b_ref, c_ref, dt_ref, cw_ref, cb_ref, dtb_ref, alog_ref, dskip_ref, nw_ref,
                y_ref, state_ref, tail_ref, pad_ref, act_ref, ccol_ref, crowT_ref, e1_ref, wl_ref, elast_ref):
    c = pl.program_id(0)
    g = pl.program_id(1)
    rows = z_ref.shape[0]
    n_sub = rows // SSD_SUB
    sub_row = lax.broadcasted_iota(jnp.int32, (SSD_SUB, SSD_SUB), 0)
    sub_col = lax.broadcasted_iota(jnp.int32, (SSD_SUB, SSD_SUB), 1)
    causal = sub_row >= sub_col

    @pl.when(c == 0)
    def _():
        state_ref[g] = jnp.zeros((SSM_STATE, GROUP_INNER), F32)
        tail_ref[g] = jnp.zeros((SUBLANES, GROUP_CONV), F32)

    @pl.when(g == 0)
    def _():
        dtv = jax.nn.softplus(dt_ref[...] + dtb_ref[...])
        dta = dtv * (-jnp.exp(alog_ref[...]))
        log_dt = jnp.log(dtv)
        tri = causal.astype(F32)
        for s in range(n_sub):
            sl = slice(s * SSD_SUB, (s + 1) * SSD_SUB)
            cum = jnp.dot(tri, dta[sl], precision=lax.Precision.HIGHEST,
                          preferred_element_type=F32)
            clast = cum[SSD_SUB - 1:SSD_SUB, :]
            ccol_ref[sl, :] = cum * LOG2E
            crowT_ref[:, sl] = ((cum - log_dt[sl]) * LOG2E).T
            e1_ref[sl, :] = _split_bf16(jnp.exp(cum))
            wl_ref[sl, :] = _split_bf16(jnp.exp(clast - cum) * dtv[sl])
            elast_ref[s] = jnp.exp(clast)

    xcur = jnp.concatenate([xs_ref[...].astype(F32), b_ref[...].astype(F32), c_ref[...].astype(F32)], axis=1)
    cw = cw_ref[...]
    bias = cb_ref[...]
    pad_ref[0:SUBLANES, :] = tail_ref[g]
    pad_ref[SUBLANES:2 * SUBLANES, :] = xcur[0:SUBLANES]
    tail_ref[g] = xcur[rows - SUBLANES:rows]
    head = bias
    body = bias + cw[CONV_WIDTH - 1:CONV_WIDTH, :] * xcur
    for k in range(CONV_WIDTH):
        lo = SUBLANES - (CONV_WIDTH - 1) + k
        head = head + cw[k:k + 1, :] * pad_ref[lo:lo + SUBLANES, :]
        if k < CONV_WIDTH - 1:
            body = body + cw[k:k + 1, :] * pltpu.roll(xcur, CONV_WIDTH - 1 - k, 0)
    act_ref[...] = _silu(jnp.concatenate([head, body[SUBLANES:]], axis=0))

    shift = lax.rem(LANES - HEADS_PER_GROUP * g, LANES)
    r0 = pl.multiple_of(g * HEADS_PER_GROUP, HEADS_PER_GROUP)
    lane16 = sub_col.astype(F32).astype(BF16)
    low_half = lane16 < SSM_HEAD_DIM
    zero16 = jnp.zeros((SSD_SUB, LANES), BF16)
    sel_row = lax.broadcasted_iota(jnp.int32, (2 * LANES, GROUP_INNER), 0) & (LANES - 1)
    sel_head = lax.broadcasted_iota(jnp.int32, (2 * LANES, GROUP_INNER), 1) // SSM_HEAD_DIM
    expand = (sel_row == sel_head + HEADS_PER_GROUP * g).astype(F32).astype(BF16)

    for s in range(n_sub):
        rs = slice(s * SSD_SUB, (s + 1) * SSD_SUB)
        act = act_ref[rs, :]
        xg = act[:, 0:GROUP_INNER]
        xb = xg.astype(BF16)
        bb = act[:, GROUP_INNER:GROUP_INNER + SSM_STATE].astype(BF16)
        cc = act[:, GROUP_INNER + SSM_STATE:GROUP_CONV].astype(BF16)
        ccol = pltpu.roll(ccol_ref[rs, :], shift, 1)
        crow = crowT_ref[pl.ds(r0, HEADS_PER_GROUP), rs]
        cbm = lax.dot_general(cc, bb, (((1,), (1,)), ((), ())), preferred_element_type=F32).astype(BF16)

        ys = []
        for p in range(HEADS_PER_GROUP // 2):
            ms = []
            for r in (2 * p, 2 * p + 1):
                seg = ccol[:, r:r + 1] - crow[r:r + 1, :]
                ms.append(cbm * jnp.exp2(jnp.where(causal, seg, -jnp.inf)).astype(BF16))
            xp = xb[:, p * LANES:(p + 1) * LANES]
            bd = jnp.concatenate([jnp.where(low_half, xp, zero16), jnp.where(low_half, zero16, xp)], axis=0)
            ys.append(jnp.dot(jnp.concatenate(ms, axis=1), bd, preferred_element_type=F32))
        y = jnp.concatenate(ys, axis=1)

        st = state_ref[g]
        e1 = jnp.dot(e1_ref[rs, :], expand, preferred_element_type=F32)
        y = y + jnp.dot(cc, st.astype(BF16), preferred_element_type=F32) * e1
        wl = jnp.dot(wl_ref[rs, :], expand, preferred_element_type=F32)
        xw = (xg * wl).astype(BF16)
        ds = lax.dot_general(bb, xw, (((0,), (0,)), ((), ())), preferred_element_type=F32)
        el = pltpu.roll(elast_ref[s], shift, 1)
        state_ref[g] = st * _expand_heads(el, 1) + ds

        y = y + dskip_ref[...] * xg
        y = y * _silu(z_ref[rs, :].astype(F32))
        y_ref[rs, :] = (y * _rms_scale(y) * nw_ref[...]).astype(y_ref.dtype)


def ssd_mixer(proj, dt_raw, conv_w_g, conv_b_g, dt_bias, a_log, d_skip, norm_w):
    seq = proj.shape[0]
    rows = min(SSD_ROWS, seq)
    n_sub = rows // SSD_SUB
    xcol0 = D_INNER // GROUP_INNER
    bcol0 = 2 * D_INNER // SSM_STATE
    ccol0 = bcol0 + GN // SSM_STATE
    return pl.pallas_call(
        _ssd_kernel, grid=(seq // rows, SSM_GROUPS),
        in_specs=[pl.BlockSpec((rows, GROUP_INNER), lambda c, g: (c, g)),
                  pl.BlockSpec((rows, GROUP_INNER), lambda c, g: (c, xcol0 + g)),
                  pl.BlockSpec((rows, SSM_STATE), lambda c, g: (c, bcol0 + g)),
                  pl.BlockSpec((rows, SSM_STATE), lambda c, g: (c, ccol0 + g)),
                  pl.BlockSpec((rows, LANES), lambda c, g: (c, 0)),
                  pl.BlockSpec((None, CONV_WIDTH, GROUP_CONV), lambda c, g: (g, 0, 0)),
                  pl.BlockSpec((None, 1, GROUP_CONV), lambda c, g: (g, 0, 0)),
                  pl.BlockSpec((1, LANES), lambda c, g: (0, 0)),
                  pl.BlockSpec((1, LANES), lambda c, g: (0, 0)),
                  pl.BlockSpec((1, GROUP_INNER), lambda c, g: (0, g)),
                  pl.BlockSpec((1, GROUP_INNER), lambda c, g: (0, g))],
        out_specs=pl.BlockSpec((rows, GROUP_INNER), lambda c, g: (c, g)),
        out_shape=jax.ShapeDtypeStruct((seq, D_INNER), BF16),
        scratch_shapes=[pltpu.VMEM((SSM_GROUPS, SSM_STATE, GROUP_INNER), F32),
                        pltpu.VMEM((SSM_GROUPS, SUBLANES, GROUP_CONV), F32),
                        pltpu.VMEM((2 * SUBLANES, GROUP_CONV), F32),
                        pltpu.VMEM((rows, GROUP_CONV), F32),
                        pltpu.VMEM((rows, LANES), F32),
                        pltpu.VMEM((LANES, rows), F32),
                        pltpu.VMEM((rows, 2 * LANES), BF16),
                        pltpu.VMEM((rows, 2 * LANES), BF16),
                        pltpu.VMEM((n_sub, 1, LANES), F32)],
        compiler_params=_params("arbitrary", "arbitrary"), name="ssd_mixer",
    )(proj, proj, proj, proj, dt_raw, conv_w_g, conv_b_g, dt_bias, a_log, d_skip, norm_w)


def _attn_kernel(sink_ref, qt_ref, kvp_ref, kvc_ref, o_ref, bias_ref):
    i = pl.program_id(0)
    group_lanes = Q_PER_KV * BLOCK

    @pl.when(i == 0)
    def _():
        sj = lax.broadcasted_iota(jnp.int32, (2 * BLOCK, group_lanes), 0)
        lane = lax.broadcasted_iota(jnp.int32, (2 * BLOCK, group_lanes), 1)
        dist_i = BLOCK + (lane & (BLOCK - 1)) - sj
        valid = (dist_i >= 0) & (dist_i < WINDOW)
        dist = dist_i.astype(F32)
        head_in_group = (lane // BLOCK).astype(F32)
        for kh in range(N_KV_HEADS):
            slope = jnp.exp2((head_in_group + (kh * Q_PER_KV + 1)) * (-8.0 / N_Q_HEADS))
            b = jnp.where(valid, -(slope * dist), -jnp.inf)
            bias_ref[1, kh] = b
            bias_ref[0, kh] = jnp.where(sj >= BLOCK, b, -jnp.inf)

    has_prev = jnp.minimum(i, 1)
    scale = HEAD_DIM ** -0.5
    kv = jnp.concatenate([kvp_ref[...], kvc_ref[...]], axis=0)
    for kh in range(N_KV_HEADS):
        k = kv[:, kh * HEAD_DIM:(kh + 1) * HEAD_DIM] * scale
        v = kv[:, KV_DIM + kh * HEAD_DIM:KV_DIM + (kh + 1) * HEAD_DIM]
        h0 = kh * Q_PER_KV
        qg = jnp.concatenate([qt_ref[(h0 + r) * HEAD_DIM:(h0 + r + 1) * HEAD_DIM, :] for r in range(Q_PER_KV)],
                             axis=1)
        sink = sink_ref[kh]
        logits = jnp.dot(k, qg, preferred_element_type=F32) + bias_ref[has_prev, kh]
        m = jnp.maximum(jnp.max(logits, axis=0, keepdims=True), sink)
        p = jnp.exp(logits - m)
        denom = jnp.sum(p, axis=0, keepdims=True) + jnp.exp(sink - m)
        ot = lax.dot_general(v, p.astype(BF16), (((0,), (0,)), ((), ())), preferred_element_type=F32) / denom
        for r in range(0, Q_PER_KV, 2):
            pair = jnp.concatenate([ot[:, r * BLOCK:(r + 1) * BLOCK], ot[:, (r + 1) * BLOCK:(r + 2) * BLOCK]], axis=0)
            o_ref[:, (h0 + r) * HEAD_DIM:(h0 + r + 2) * HEAD_DIM] = pair.T.astype(o_ref.dtype)


def swa_attention(qt, kv, sinks):
    seq = kv.shape[0]
    nb = seq // BLOCK
    sink_rows = jnp.repeat(sinks.astype(F32), BLOCK).reshape(N_KV_HEADS, 1, Q_PER_KV * BLOCK)
    return pl.pallas_call(
        _attn_kernel, grid=(nb,),
        in_specs=[pl.BlockSpec((N_KV_HEADS, 1, Q_PER_KV * BLOCK), lambda i: (0, 0, 0)),
                  pl.BlockSpec((N_Q_HEADS * HEAD_DIM, BLOCK), lambda i: (0, i)),
                  pl.BlockSpec((BLOCK, 2 * KV_DIM), lambda i: (jnp.maximum(i - 1, 0), 0)),
                  pl.BlockSpec((BLOCK, 2 * KV_DIM), lambda i: (i, 0))],
        out_specs=pl.BlockSpec((BLOCK, N_Q_HEADS * HEAD_DIM), lambda i: (i, 0)),
        out_shape=jax.ShapeDtypeStruct((seq, N_Q_HEADS * HEAD_DIM), BF16),
        scratch_shapes=[pltpu.VMEM((2, N_KV_HEADS, 2 * BLOCK, Q_PER_KV * BLOCK), F32)],
        compiler_params=_params("arbitrary"), name="swa_attention",
    )(sink_rows, qt, kv, kv)


def _pad_lanes(v, width=LANES):
    v = v.reshape(1, -1).astype(F32)
    return jnp.pad(v, ((0, 0), (0, width - v.shape[1])))


def kernel(x, norm_w, ssm_w_in, ssm_conv_w, ssm_conv_b, ssm_dt_bias, ssm_A_log, ssm_D, ssm_norm_w, ssm_w_out,
           kv_norm_w, w_kv, attn_w_q, attn_sinks, attn_w_o, ffn_w_gate, ffn_w_up, ffn_w_down):
    batch, seq, d = x.shape
    assert batch == 1 and d == D_MODEL and seq % SSD_ROWS == 0
    h = x.reshape(seq, d)
    row = lambda v: v.reshape(1, -1).astype(F32)

    w_gate, w_up, w_down = ffn_w_gate.astype(BF16), ffn_w_up.astype(BF16), ffn_w_down.astype(BF16)

    g = norm_w[0]
    w_in_t = ssm_w_in[0].T.astype(BF16)
    n_main = 2 * D_INNER + 2 * GN
    w_dt_t = jnp.pad(w_in_t[n_main:], ((0, LANES - SSM_HEADS), (0, 0)))
    proj, dt_raw = norm_matmul2_nt(h, row(g[0]), w_in_t, w_dt_t, n=n_main, name="in_proj")

    def per_group(t):
        xs = t[:, :D_INNER].reshape(-1, SSM_GROUPS, GROUP_INNER)
        bs = t[:, D_INNER:D_INNER + GN].reshape(-1, SSM_GROUPS, SSM_STATE)
        cs = t[:, D_INNER + GN:].reshape(-1, SSM_GROUPS, SSM_STATE)
        return jnp.transpose(jnp.concatenate([xs, bs, cs], axis=-1), (1, 0, 2)).astype(F32)

    y = ssd_mixer(proj, dt_raw, per_group(ssm_conv_w[0]), per_group(ssm_conv_b[0].reshape(1, -1)),
                  _pad_lanes(ssm_dt_bias[0]), _pad_lanes(ssm_A_log[0]),
                  row(jnp.repeat(ssm_D[0], SSM_HEAD_DIM)), row(ssm_norm_w[0]))
    h = matmul_norm_residual(y, ssm_w_out[0].astype(BF16), h, row(g[1]), name="ssm_out_proj")
    h = ffn(h, row(g[2]), w_gate, w_up, w_down, row(g[3]), 0, name="ffn0")

    g = norm_w[1]
    kv = norm_matmul(h, row(kv_norm_w), w_kv.astype(BF16), name="kv_proj")
    qt = norm_matmul_t(h, row(g[0]), attn_w_q[0].astype(BF16), name="q_proj")
    a = swa_attention(qt, kv, attn_sinks[0])
    h = matmul_norm_residual(a, attn_w_o[0].astype(BF16), h, row(g[1]), name="attn_out_proj")
    h = ffn(h, row(g[2]), w_gate, w_up, w_down, row(g[3]), 1, name="ffn1")
    return h.reshape(batch, seq, d)
```

```python
import math

import jax
import jax.numpy as jnp
from jax import lax
from jax.experimental import pallas as pl
from jax.experimental.pallas import tpu as pltpu

D_MODEL = 2048
D_INNER = 4096
SSM_HEAD_DIM = 64
SSM_HEADS = 64
SSM_GROUPS = 8
HEADS_PER_GROUP = SSM_HEADS // SSM_GROUPS
SSM_STATE = 128
CONV_WIDTH = 4
GN = SSM_GROUPS * SSM_STATE
GROUP_INNER = D_INNER // SSM_GROUPS
GROUP_CONV = GROUP_INNER + 2 * SSM_STATE
SSD_SUB = 128
SSD_ROWS = 1024
N_Q_HEADS = 32
N_KV_HEADS = 4
Q_PER_KV = N_Q_HEADS // N_KV_HEADS
HEAD_DIM = 64
WINDOW = 128
BLOCK = 128
KV_DIM = N_KV_HEADS * HEAD_DIM
EPS = 1e-6
LOG2E = math.log2(math.e)

LANES = 128
SUBLANES = 8
VMEM_LIMIT_BYTES = 63 * 1024 * 1024

ROW_TILE = 1024
COL_TILE = 1024
IN_PROJ_COL_TILE = 2048
K_TILE = 1024
FFN_CHUNK = 512

BF16 = jnp.bfloat16
F32 = jnp.float32


def _rms_scale(x):
    return lax.rsqrt(jnp.mean(x * x, axis=-1, keepdims=True) + EPS)


def _silu(x):
    return x * jax.nn.sigmoid(x)


def _params(*semantics):
    return pltpu.CompilerParams(dimension_semantics=semantics, vmem_limit_bytes=VMEM_LIMIT_BYTES)


def _normed_rows(x_ref, g_ref, xn_ref, first):
    if not first:
        return xn_ref[...]
    x = x_ref[...]
    xn = (x * _rms_scale(x) * g_ref[...]).astype(xn_ref.dtype)
    xn_ref[...] = xn
    return xn


def _first_and_rest(step):
    j = pl.program_id(1)
    pl.when(j == 0)(lambda: step(True))
    pl.when(j > 0)(lambda: step(False))


def _first_middle_last(step):
    j = pl.program_id(1)
    last = pl.num_programs(1) - 1
    pl.when(j == 0)(lambda: step(True, False))
    pl.when(jnp.logical_and(j > 0, j < last))(lambda: step(False, False))
    pl.when(j == last)(lambda: step(False, True))


def _norm_matmul_kernel(x_ref, g_ref, w_ref, o_ref, xn_ref):
    def step(first):
        xn = _normed_rows(x_ref, g_ref, xn_ref, first)
        o_ref[...] = jnp.dot(xn, w_ref[...], preferred_element_type=F32).astype(o_ref.dtype)

    _first_and_rest(step)


def _norm_matmul2_kernel(x_ref, g_ref, wt_ref, wt2_ref, o_ref, o2_ref, xn_ref):
    nt = (((1,), (1,)), ((), ()))

    def step(first):
        xn = _normed_rows(x_ref, g_ref, xn_ref, first)
        if first:
            o2_ref[...] = lax.dot_general(xn, wt2_ref[...], nt, preferred_element_type=F32)
        o_ref[...] = lax.dot_general(xn, wt_ref[...], nt, preferred_element_type=F32).astype(o_ref.dtype)

    _first_and_rest(step)


def _norm_matmul_t_kernel(x_ref, g_ref, w_ref, o_ref, xn_ref):
    def step(first):
        xn = _normed_rows(x_ref, g_ref, xn_ref, first)
        o_ref[...] = lax.dot_general(w_ref[...], xn, (((0,), (1,)), ((), ())),
                                     preferred_element_type=F32).astype(o_ref.dtype)

    _first_and_rest(step)


def norm_matmul_t(x, g, w, *, name):
    m, k = x.shape
    n = w.shape[1]
    tm, tn = min(ROW_TILE, m), min(COL_TILE, n)
    return pl.pallas_call(
        _norm_matmul_t_kernel, grid=(m // tm, n // tn),
        in_specs=[pl.BlockSpec((tm, k), lambda i, j: (i, 0)),
                  pl.BlockSpec((1, k), lambda i, j: (0, 0)),
                  pl.BlockSpec((k, tn), lambda i, j: (0, j))],
        out_specs=pl.BlockSpec((tn, tm), lambda i, j: (j, i)),
        out_shape=jax.ShapeDtypeStruct((n, m), BF16), scratch_shapes=[pltpu.VMEM((tm, k), BF16)],
        compiler_params=_params("parallel", "arbitrary"), name=name)(x, g, w)


def norm_matmul(x, g, w, *, name):
    m, k = x.shape
    n = w.shape[1]
    tm, tn = min(ROW_TILE, m), min(COL_TILE, n)
    return pl.pallas_call(
        _norm_matmul_kernel, grid=(m // tm, n // tn),
        in_specs=[pl.BlockSpec((tm, k), lambda i, j: (i, 0)),
                  pl.BlockSpec((1, k), lambda i, j: (0, 0)),
                  pl.BlockSpec((k, tn), lambda i, j: (0, j))],
        out_specs=pl.BlockSpec((tm, tn), lambda i, j: (i, j)),
        out_shape=jax.ShapeDtypeStruct((m, n), BF16), scratch_shapes=[pltpu.VMEM((tm, k), BF16)],
        compiler_params=_params("parallel", "arbitrary"), name=name)(x, g, w)


def norm_matmul2_nt(x, g, wt, wt2, *, n, name):
    m, k = x.shape
    n2 = wt2.shape[0]
    tm, tn = min(ROW_TILE, m), min(IN_PROJ_COL_TILE, n)
    assert n % tn == 0
    return pl.pallas_call(
        _norm_matmul2_kernel, grid=(m // tm, n // tn),
        in_specs=[pl.BlockSpec((tm, k), lambda i, j: (i, 0)),
                  pl.BlockSpec((1, k), lambda i, j: (0, 0)),
                  pl.BlockSpec((tn, k), lambda i, j: (j, 0)),
                  pl.BlockSpec((n2, k), lambda i, j: (0, 0))],
        out_specs=[pl.BlockSpec((tm, tn), lambda i, j: (i, j)), pl.BlockSpec((tm, n2), lambda i, j: (i, 0))],
        out_shape=[jax.ShapeDtypeStruct((m, n), BF16), jax.ShapeDtypeStruct((m, n2), F32)],
        scratch_shapes=[pltpu.VMEM((tm, k), BF16)],
        compiler_params=_params("parallel", "arbitrary"), name=name)(x, g, wt, wt2)


def _matmul_norm_residual_kernel(y_ref, w_ref, h_ref, g_ref, o_ref):
    def step(first, final):
        d = jnp.dot(y_ref[...], w_ref[...], preferred_element_type=F32)
        a = d if first else o_ref[...] + d
        o_ref[...] = h_ref[...] + a * _rms_scale(a) * g_ref[...] if final else a

    _first_middle_last(step)


def matmul_norm_residual(y, w, h, g, *, name):
    m, k = y.shape
    n = w.shape[1]
    tm, tk = min(ROW_TILE, m), min(K_TILE, k)
    assert k // tk >= 2
    return pl.pallas_call(
        _matmul_norm_residual_kernel, grid=(m // tm, k // tk),
        in_specs=[pl.BlockSpec((tm, tk), lambda i, kk: (i, kk)),
                  pl.BlockSpec((tk, n), lambda i, kk: (kk, 0)),
                  pl.BlockSpec((tm, n), lambda i, kk: (i, 0)),
                  pl.BlockSpec((1, n), lambda i, kk: (0, 0))],
        out_specs=pl.BlockSpec((tm, n), lambda i, kk: (i, 0)),
        out_shape=jax.ShapeDtypeStruct((m, n), F32),
        compiler_params=_params("parallel", "arbitrary"), name=name)(y, w, h, g)


def _ffn_kernel(h_ref, gpre_ref, wg_ref, wu_ref, wd_ref, gpost_ref, o_ref, xn_ref):
    def step(first, final):
        xn = _normed_rows(h_ref, gpre_ref, xn_ref, first)
        gate = jnp.dot(xn, wg_ref[...], preferred_element_type=F32)
        up = jnp.dot(xn, wu_ref[...], preferred_element_type=F32)
        act = (_silu(gate) * up).astype(BF16)
        d = jnp.dot(act, wd_ref[...], preferred_element_type=F32)
        a = d if first else o_ref[...] + d
        o_ref[...] = h_ref[...] + a * _rms_scale(a) * gpost_ref[...] if final else a

    _first_middle_last(step)


def ffn(h, g_pre, w_gate, w_up, w_down, g_post, layer, *, name):
    m, d = h.shape
    dff = w_gate.shape[2]
    tm, tf = min(ROW_TILE, m), FFN_CHUNK
    assert dff // tf >= 2
    return pl.pallas_call(
        _ffn_kernel, grid=(m // tm, dff // tf),
        in_specs=[pl.BlockSpec((tm, d), lambda i, f: (i, 0)),
                  pl.BlockSpec((1, d), lambda i, f: (0, 0)),
                  pl.BlockSpec((None, d, tf), lambda i, f: (layer, 0, f)),
                  pl.BlockSpec((None, d, tf), lambda i, f: (layer, 0, f)),
                  pl.BlockSpec((None, tf, d), lambda i, f: (layer, f, 0)),
                  pl.BlockSpec((1, d), lambda i, f: (0, 0))],
        out_specs=pl.BlockSpec((tm, d), lambda i, f: (i, 0)),
        out_shape=jax.ShapeDtypeStruct((m, d), F32),
        scratch_shapes=[pltpu.VMEM((tm, d), BF16)],
        compiler_params=_params("parallel", "arbitrary"), name=name)(h, g_pre, w_gate, w_up, w_down, g_post)


def _expand_heads(v, rows):
    lane = lax.broadcasted_iota(jnp.int32, (rows, LANES), 1)
    first = lane < SSM_HEAD_DIM
    pairs = [jnp.where(first, v[:, 2 * p:2 * p + 1], v[:, 2 * p + 1:2 * p + 2]) for p in range(HEADS_PER_GROUP // 2)]
    return jnp.concatenate(pairs, axis=1)


def _split_bf16(v):
    hi = v.astype(BF16)
    lo = (v - hi.astype(F32)).astype(BF16)
    return jnp.concatenate([hi, lo], axis=1)


def _ssd_kernel(z_ref, xs_ref, b_ref, c_ref, dt_ref, cw_ref, cb_ref, dtb_ref, alog_ref, dskip_ref, nw_ref,
                wg_ref, wu_ref, wd_ref, y_ref, wg16_ref, wu16_ref, wd16_ref,
                state_ref, tail_ref, pad_ref, act_ref, ccol_ref, crowT_ref, e1_ref, wl_ref, elast_ref):
    c = pl.program_id(0)
    g = pl.program_id(1)

    for w32_ref, w16_ref in ((wg_ref, wg16_ref), (wu_ref, wu16_ref), (wd_ref, wd16_ref)):
        w16_ref[...] = w32_ref[...].astype(BF16)

    rows = z_ref.shape[0]
    n_sub = rows // SSD_SUB
    sub_row = lax.broadcasted_iota(jnp.int32, (SSD_SUB, SSD_SUB), 0)
    sub_col = lax.broadcasted_iota(jnp.int32, (SSD_SUB, SSD_SUB), 1)
    causal = sub_row >= sub_col

    @pl.when(c == 0)
    def _():
        state_ref[g] = jnp.zeros((SSM_STATE, GROUP_INNER), F32)
        tail_ref[g] = jnp.zeros((SUBLANES, GROUP_CONV), F32)

    @pl.when(g == 0)
    def _():
        dtv = jax.nn.softplus(dt_ref[...] + dtb_ref[...])
        dta = dtv * (-jnp.exp(alog_ref[...]))
        log_dt = jnp.log(dtv)
        tri = causal.astype(F32)
        for s in range(n_sub):
            sl = slice(s * SSD_SUB, (s + 1) * SSD_SUB)
            cum = jnp.dot(tri, dta[sl], precision=lax.Precision.HIGHEST,
                          preferred_element_type=F32)
            clast = cum[SSD_SUB - 1:SSD_SUB, :]
            ccol_ref[sl, :] = cum * LOG2E
            crowT_ref[:, sl] = ((cum - log_dt[sl]) * LOG2E).T
            e1_ref[sl, :] = _split_bf16(jnp.exp(cum))
            wl_ref[sl, :] = _split_bf16(jnp.exp(clast - cum) * dtv[sl])
            elast_ref[s] = jnp.exp(clast)

    xcur = jnp.concatenate([xs_ref[...].astype(F32), b_ref[...].astype(F32), c_ref[...].astype(F32)], axis=1)
    cw = cw_ref[...]
    bias = cb_ref[...]
    pad_ref[0:SUBLANES, :] = tail_ref[g]
    pad_ref[SUBLANES:2 * SUBLANES, :] = xcur[0:SUBLANES]
    tail_ref[g] = xcur[rows - SUBLANES:rows]
    head = bias
    body = bias + cw[CONV_WIDTH - 1:CONV_WIDTH, :] * xcur
    for k in range(CONV_WIDTH):
        lo = SUBLANES - (CONV_WIDTH - 1) + k
        head = head + cw[k:k + 1, :] * pad_ref[lo:lo + SUBLANES, :]
        if k < CONV_WIDTH - 1:
            body = body + cw[k:k + 1, :] * pltpu.roll(xcur, CONV_WIDTH - 1 - k, 0)
    act_ref[...] = _silu(jnp.concatenate([head, body[SUBLANES:]], axis=0))

    shift = lax.rem(LANES - HEADS_PER_GROUP * g, LANES)
    r0 = pl.multiple_of(g * HEADS_PER_GROUP, HEADS_PER_GROUP)
    lane16 = sub_col.astype(F32).astype(BF16)
    low_half = lane16 < SSM_HEAD_DIM
    zero16 = jnp.zeros((SSD_SUB, LANES), BF16)
    sel_row = lax.broadcasted_iota(jnp.int32, (2 * LANES, GROUP_INNER), 0) & (LANES - 1)
    sel_head = lax.broadcasted_iota(jnp.int32, (2 * LANES, GROUP_INNER), 1) // SSM_HEAD_DIM
    expand = (sel_row == sel_head + HEADS_PER_GROUP * g).astype(F32).astype(BF16)

    for s in range(n_sub):
        rs = slice(s * SSD_SUB, (s + 1) * SSD_SUB)
        act = act_ref[rs, :]
        xg = act[:, 0:GROUP_INNER]
        xb = xg.astype(BF16)
        bb = act[:, GROUP_INNER:GROUP_INNER + SSM_STATE].astype(BF16)
        cc = act[:, GROUP_INNER + SSM_STATE:GROUP_CONV].astype(BF16)
        ccol = pltpu.roll(ccol_ref[rs, :], shift, 1)
        crow = crowT_ref[pl.ds(r0, HEADS_PER_GROUP), rs]
        cbm = lax.dot_general(cc, bb, (((1,), (1,)), ((), ())), preferred_element_type=F32).astype(BF16)

        ys = []
        for p in range(HEADS_PER_GROUP // 2):
            ms = []
            for r in (2 * p, 2 * p + 1):
                seg = ccol[:, r:r + 1] - crow[r:r + 1, :]
                ms.append(cbm * jnp.exp2(jnp.where(causal, seg, -jnp.inf)).astype(BF16))
            xp = xb[:, p * LANES:(p + 1) * LANES]
            bd = jnp.concatenate([jnp.where(low_half, xp, zero16), jnp.where(low_half, zero16, xp)], axis=0)
            ys.append(jnp.dot(jnp.concatenate(ms, axis=1), bd, preferred_element_type=F32))
        y = jnp.concatenate(ys, axis=1)

        st = state_ref[g]
        e1 = jnp.dot(e1_ref[rs, :], expand, preferred_element_type=F32)
        y = y + jnp.dot(cc, st.astype(BF16), preferred_element_type=F32) * e1
        wl = jnp.dot(wl_ref[rs, :], expand, preferred_element_type=F32)
        xw = (xg * wl).astype(BF16)
        ds = lax.dot_general(bb, xw, (((0,), (0,)), ((), ())), preferred_element_type=F32)
        el = pltpu.roll(elast_ref[s], shift, 1)
        state_ref[g] = st * _expand_heads(el, 1) + ds

        y = y + dskip_ref[...] * xg
        y = y * _silu(z_ref[rs, :].astype(F32))
        y_ref[rs, :] = (y * _rms_scale(y) * nw_ref[...]).astype(y_ref.dtype)


def _slab_spec(w, n_slabs, steps_per_slab):
    layers, r, c = w.shape
    assert (layers * r) % n_slabs == 0
    slab_rows = layers * r // n_slabs
    assert r % slab_rows == 0 and slab_rows % (2 * SUBLANES) == 0
    per_layer = r // slab_rows

    def index(cb, g):
        slab = (cb * SSM_GROUPS + g) // steps_per_slab
        return slab // per_layer, slab % per_layer, 0

    return pl.BlockSpec((None, slab_rows, c), index)


def ssd_mixer(proj, dt_raw, conv_w_g, conv_b_g, dt_bias, a_log, d_skip, norm_w, ffn_weights):
    seq = proj.shape[0]
    rows = min(SSD_ROWS, seq)
    n_sub = rows // SSD_SUB
    xcol0 = D_INNER // GROUP_INNER
    bcol0 = 2 * D_INNER // SSM_STATE
    ccol0 = bcol0 + GN // SSM_STATE
    n_steps = (seq // rows) * SSM_GROUPS
    w_gate, w_up, w_down = ffn_weights
    w_specs = [_slab_spec(w_gate, n_steps, 1), _slab_spec(w_up, n_steps, 1), _slab_spec(w_down, n_steps // 2, 2)]
    return pl.pallas_call(
        _ssd_kernel, grid=(seq // rows, SSM_GROUPS),
        in_specs=[pl.BlockSpec((rows, GROUP_INNER), lambda c, g: (c, g)),
                  pl.BlockSpec((rows, GROUP_INNER), lambda c, g: (c, xcol0 + g)),
                  pl.BlockSpec((rows, SSM_STATE), lambda c, g: (c, bcol0 + g)),
                  pl.BlockSpec((rows, SSM_STATE), lambda c, g: (c, ccol0 + g)),
                  pl.BlockSpec((rows, LANES), lambda c, g: (c, 0)),
                  pl.BlockSpec((None, CONV_WIDTH, GROUP_CONV), lambda c, g: (g, 0, 0)),
                  pl.BlockSpec((None, 1, GROUP_CONV), lambda c, g: (g, 0, 0)),
                  pl.BlockSpec((1, LANES), lambda c, g: (0, 0)),
                  pl.BlockSpec((1, LANES), lambda c, g: (0, 0)),
                  pl.BlockSpec((1, GROUP_INNER), lambda c, g: (0, g)),
                  pl.BlockSpec((1, GROUP_INNER), lambda c, g: (0, g))] + w_specs,
        out_specs=[pl.BlockSpec((rows, GROUP_INNER), lambda c, g: (c, g))] + w_specs,
        out_shape=[jax.ShapeDtypeStruct((seq, D_INNER), BF16)]
        + [jax.ShapeDtypeStruct(w.shape, BF16) for w in ffn_weights],
        scratch_shapes=[pltpu.VMEM((SSM_GROUPS, SSM_STATE, GROUP_INNER), F32),
                        pltpu.VMEM((SSM_GROUPS, SUBLANES, GROUP_CONV), F32),
                        pltpu.VMEM((2 * SUBLANES, GROUP_CONV), F32),
                        pltpu.VMEM((rows, GROUP_CONV), F32),
                        pltpu.VMEM((rows, LANES), F32),
                        pltpu.VMEM((LANES, rows), F32),
                        pltpu.VMEM((rows, 2 * LANES), BF16),
                        pltpu.VMEM((rows, 2 * LANES), BF16),
                        pltpu.VMEM((n_sub, 1, LANES), F32)],
        compiler_params=_params("arbitrary", "arbitrary"), name="ssd_mixer",
    )(proj, proj, proj, proj, dt_raw, conv_w_g, conv_b_g, dt_bias, a_log, d_skip, norm_w, *ffn_weights)


def _attn_kernel(sink_ref, qt_ref, kvp_ref, kvc_ref, o_ref, bias_ref):
    i = pl.program_id(0)
    group_lanes = Q_PER_KV * BLOCK

    @pl.when(i == 0)
    def _():
        sj = lax.broadcasted_iota(jnp.int32, (2 * BLOCK, group_lanes), 0)
        lane = lax.broadcasted_iota(jnp.int32, (2 * BLOCK, group_lanes), 1)
        dist_i = BLOCK + (lane & (BLOCK - 1)) - sj
        valid = (dist_i >= 0) & (dist_i < WINDOW)
        dist = dist_i.astype(F32)
        head_in_group = (lane // BLOCK).astype(F32)
        for kh in range(N_KV_HEADS):
            slope = jnp.exp2((head_in_group + (kh * Q_PER_KV + 1)) * (-8.0 / N_Q_HEADS))
            b = jnp.where(valid, -(slope * dist), -jnp.inf)
            bias_ref[1, kh] = b
            bias_ref[0, kh] = jnp.where(sj >= BLOCK, b, -jnp.inf)

    has_prev = jnp.minimum(i, 1)
    scale = HEAD_DIM ** -0.5
    kv = jnp.concatenate([kvp_ref[...], kvc_ref[...]], axis=0)
    for kh in range(N_KV_HEADS):
        k = kv[:, kh * HEAD_DIM:(kh + 1) * HEAD_DIM] * scale
        v = kv[:, KV_DIM + kh * HEAD_DIM:KV_DIM + (kh + 1) * HEAD_DIM]
        h0 = kh * Q_PER_KV
        qg = jnp.concatenate([qt_ref[(h0 + r) * HEAD_DIM:(h0 + r + 1) * HEAD_DIM, :] for r in range(Q_PER_KV)],
                             axis=1)
        sink = sink_ref[kh]
        logits = jnp.dot(k, qg, preferred_element_type=F32) + bias_ref[has_prev, kh]
        m = jnp.maximum(jnp.max(logits, axis=0, keepdims=True), sink)
        p = jnp.exp(logits - m)
        denom = jnp.sum(p, axis=0, keepdims=True) + jnp.exp(sink - m)
        ot = lax.dot_general(v, p.astype(BF16), (((0,), (0,)), ((), ())), preferred_element_type=F32) / denom
        for r in range(0, Q_PER_KV, 2):
            pair = jnp.concatenate([ot[:, r * BLOCK:(r + 1) * BLOCK], ot[:, (r + 1) * BLOCK:(r + 2) * BLOCK]], axis=0)
            o_ref[:, (h0 + r) * HEAD_DIM:(h0 + r + 2) * HEAD_DIM] = pair.T.astype(o_ref.dtype)


def swa_attention(qt, kv, sinks):
    seq = kv.shape[0]
    nb = seq // BLOCK
    sink_rows = jnp.repeat(sinks.astype(F32), BLOCK).reshape(N_KV_HEADS, 1, Q_PER_KV * BLOCK)
    return pl.pallas_call(
        _attn_kernel, grid=(nb,),
        in_specs=[pl.BlockSpec((N_KV_HEADS, 1, Q_PER_KV * BLOCK), lambda i: (0, 0, 0)),
                  pl.BlockSpec((N_Q_HEADS * HEAD_DIM, BLOCK), lambda i: (0, i)),
                  pl.BlockSpec((BLOCK, 2 * KV_DIM), lambda i: (jnp.maximum(i - 1, 0), 0)),
                  pl.BlockSpec((BLOCK, 2 * KV_DIM), lambda i: (i, 0))],
        out_specs=pl.BlockSpec((BLOCK, N_Q_HEADS * HEAD_DIM), lambda i: (i, 0)),
        out_shape=jax.ShapeDtypeStruct((seq, N_Q_HEADS * HEAD_DIM), BF16),
        scratch_shapes=[pltpu.VMEM((2, N_KV_HEADS, 2 * BLOCK, Q_PER_KV * BLOCK), F32)],
        compiler_params=_params("arbitrary"), name="swa_attention",
    )(sink_rows, qt, kv, kv)


def _pad_lanes(v, width=LANES):
    v = v.reshape(1, -1).astype(F32)
    return jnp.pad(v, ((0, 0), (0, width - v.shape[1])))


def kernel(x, norm_w, ssm_w_in, ssm_conv_w, ssm_conv_b, ssm_dt_bias, ssm_A_log, ssm_D, ssm_norm_w, ssm_w_out,
           kv_norm_w, w_kv, attn_w_q, attn_sinks, attn_w_o, ffn_w_gate, ffn_w_up, ffn_w_down):
    batch, seq, d = x.shape
    assert batch == 1 and d == D_MODEL and seq % SSD_ROWS == 0
    h = x.reshape(seq, d)
    row = lambda v: v.reshape(1, -1).astype(F32)

    g = norm_w[0]
    w_in_t = ssm_w_in[0].T.astype(BF16)
    n_main = 2 * D_INNER + 2 * GN
    w_dt_t = jnp.pad(w_in_t[n_main:], ((0, LANES - SSM_HEADS), (0, 0)))
    proj, dt_raw = norm_matmul2_nt(h, row(g[0]), w_in_t, w_dt_t, n=n_main, name="in_proj")

    def per_group(t):
        xs = t[:, :D_INNER].reshape(-1, SSM_GROUPS, GROUP_INNER)
        bs = t[:, D_INNER:D_INNER + GN].reshape(-1, SSM_GROUPS, SSM_STATE)
        cs = t[:, D_INNER + GN:].reshape(-1, SSM_GROUPS, SSM_STATE)
        return jnp.transpose(jnp.concatenate([xs, bs, cs], axis=-1), (1, 0, 2)).astype(F32)

    y, w_gate, w_up, w_down = ssd_mixer(
        proj, dt_raw, per_group(ssm_conv_w[0]), per_group(ssm_conv_b[0].reshape(1, -1)),
        _pad_lanes(ssm_dt_bias[0]), _pad_lanes(ssm_A_log[0]),
        row(jnp.repeat(ssm_D[0], SSM_HEAD_DIM)), row(ssm_norm_w[0]), (ffn_w_gate, ffn_w_up, ffn_w_down))
    h = matmul_norm_residual(y, ssm_w_out[0].astype(BF16), h, row(g[1]), name="ssm_out_proj")
    h = ffn(h, row(g[2]), w_gate, w_up, w_down, row(g[3]), 0, name="ffn0")

    g = norm_w[1]
    kv = norm_matmul(h, row(kv_norm_w), w_kv.astype(BF16), name="kv_proj")
    qt = norm_matmul_t(h, row(g[0]), attn_w_q[0].astype(BF16), name="q_proj")
    a = swa_attention(qt, kv, attn_sinks[0])
    h = matmul_norm_residual(a, attn_w_o[0].astype(BF16), h, row(g[1]), name="attn_out_proj")
    h = ffn(h, row(g[2]), w_gate, w_up, w_down, row(g[3]), 1, name="ffn1")
    return h.reshape(batch, seq, d)
```

```python
import functools
import math

import jax
import jax.numpy as jnp
from jax import lax
from jax.experimental import pallas as pl
from jax.experimental.pallas import tpu as pltpu

D_MODEL = 2048
D_INNER = 4096
SSM_HEAD_DIM = 64
SSM_HEADS = 64
SSM_GROUPS = 8
HEADS_PER_GROUP = SSM_HEADS // SSM_GROUPS
SSM_STATE = 128
CONV_WIDTH = 4
GN = SSM_GROUPS * SSM_STATE
GROUP_INNER = D_INNER // SSM_GROUPS
GROUP_CONV = GROUP_INNER + 2 * SSM_STATE
N_SSD_INPUTS = 11
SSD_SUB = 128
SSD_ROWS = 1024
N_Q_HEADS = 32
N_KV_HEADS = 4
Q_PER_KV = N_Q_HEADS // N_KV_HEADS
HEAD_DIM = 64
WINDOW = 128
BLOCK = 128
KV_DIM = N_KV_HEADS * HEAD_DIM
EPS = 1e-6
LOG2E = math.log2(math.e)

LANES = 128
SUBLANES = 8
VMEM_LIMIT_BYTES = 63 * 1024 * 1024

ROW_TILE = 1024
COL_TILE = 1024
IN_PROJ_COL_TILE = 2048
K_TILE = 1024
FFN_CHUNK = 512

BF16 = jnp.bfloat16
F32 = jnp.float32


def _rms_scale(x):
    return lax.rsqrt(jnp.mean(x * x, axis=-1, keepdims=True) + EPS)


def _silu(x):
    return x * jax.nn.sigmoid(x)


def _params(*semantics):
    return pltpu.CompilerParams(dimension_semantics=semantics, vmem_limit_bytes=VMEM_LIMIT_BYTES)


def _normed_rows(x_ref, g_ref, xn_ref, first):
    if not first:
        return xn_ref[...]
    x = x_ref[...]
    xn = (x * _rms_scale(x) * g_ref[...]).astype(xn_ref.dtype)
    xn_ref[...] = xn
    return xn


def _first_and_rest(step):
    j = pl.program_id(1)
    pl.when(j == 0)(lambda: step(True))
    pl.when(j > 0)(lambda: step(False))


def _first_middle_last(step):
    j = pl.program_id(1)
    last = pl.num_programs(1) - 1
    pl.when(j == 0)(lambda: step(True, False))
    pl.when(jnp.logical_and(j > 0, j < last))(lambda: step(False, False))
    pl.when(j == last)(lambda: step(False, True))


def _norm_matmul_kernel(x_ref, g_ref, w_ref, o_ref, xn_ref):
    def step(first):
        xn = _normed_rows(x_ref, g_ref, xn_ref, first)
        o_ref[...] = jnp.dot(xn, w_ref[...], preferred_element_type=F32).astype(o_ref.dtype)

    _first_and_rest(step)


def _norm_matmul2_kernel(x_ref, g_ref, wt_ref, wt2_ref, o_ref, o2_ref, xn_ref):
    nt = (((1,), (1,)), ((), ()))

    def step(first):
        xn = _normed_rows(x_ref, g_ref, xn_ref, first)
        if first:
            o2_ref[...] = lax.dot_general(xn, wt2_ref[...], nt, preferred_element_type=F32)
        o_ref[...] = lax.dot_general(xn, wt_ref[...], nt, preferred_element_type=F32).astype(o_ref.dtype)

    _first_and_rest(step)


def _norm_matmul_t_kernel(x_ref, g_ref, w_ref, o_ref, xn_ref):
    def step(first):
        xn = _normed_rows(x_ref, g_ref, xn_ref, first)
        o_ref[...] = lax.dot_general(w_ref[...], xn, (((0,), (1,)), ((), ())),
                                     preferred_element_type=F32).astype(o_ref.dtype)

    _first_and_rest(step)


def norm_matmul_t(x, g, w, *, name):
    m, k = x.shape
    n = w.shape[1]
    tm, tn = min(ROW_TILE, m), min(COL_TILE, n)
    return pl.pallas_call(
        _norm_matmul_t_kernel, grid=(m // tm, n // tn),
        in_specs=[pl.BlockSpec((tm, k), lambda i, j: (i, 0)),
                  pl.BlockSpec((1, k), lambda i, j: (0, 0)),
                  pl.BlockSpec((k, tn), lambda i, j: (0, j))],
        out_specs=pl.BlockSpec((tn, tm), lambda i, j: (j, i)),
        out_shape=jax.ShapeDtypeStruct((n, m), BF16), scratch_shapes=[pltpu.VMEM((tm, k), BF16)],
        compiler_params=_params("parallel", "arbitrary"), name=name)(x, g, w)


def norm_matmul(x, g, w, *, name):
    m, k = x.shape
    n = w.shape[1]
    tm, tn = min(ROW_TILE, m), min(COL_TILE, n)
    return pl.pallas_call(
        _norm_matmul_kernel, grid=(m // tm, n // tn),
        in_specs=[pl.BlockSpec((tm, k), lambda i, j: (i, 0)),
                  pl.BlockSpec((1, k), lambda i, j: (0, 0)),
                  pl.BlockSpec((k, tn), lambda i, j: (0, j))],
        out_specs=pl.BlockSpec((tm, tn), lambda i, j: (i, j)),
        out_shape=jax.ShapeDtypeStruct((m, n), BF16), scratch_shapes=[pltpu.VMEM((tm, k), BF16)],
        compiler_params=_params("parallel", "arbitrary"), name=name)(x, g, w)


def norm_matmul2_nt(x, g, wt, wt2, *, n, name):
    m, k = x.shape
    n2 = wt2.shape[0]
    tm, tn = min(ROW_TILE, m), min(IN_PROJ_COL_TILE, n)
    assert n % tn == 0
    return pl.pallas_call(
        _norm_matmul2_kernel, grid=(m // tm, n // tn),
        in_specs=[pl.BlockSpec((tm, k), lambda i, j: (i, 0)),
                  pl.BlockSpec((1, k), lambda i, j: (0, 0)),
                  pl.BlockSpec((tn, k), lambda i, j: (j, 0)),
                  pl.BlockSpec((n2, k), lambda i, j: (0, 0))],
        out_specs=[pl.BlockSpec((tm, tn), lambda i, j: (i, j)), pl.BlockSpec((tm, n2), lambda i, j: (i, 0))],
        out_shape=[jax.ShapeDtypeStruct((m, n), BF16), jax.ShapeDtypeStruct((m, n2), F32)],
        scratch_shapes=[pltpu.VMEM((tm, k), BF16)],
        compiler_params=_params("parallel", "arbitrary"), name=name)(x, g, wt, wt2)


def _matmul_norm_residual_kernel(y_ref, w_ref, h_ref, g_ref, o_ref):
    def step(first, final):
        d = jnp.dot(y_ref[...], w_ref[...], preferred_element_type=F32)
        a = d if first else o_ref[...] + d
        o_ref[...] = h_ref[...] + a * _rms_scale(a) * g_ref[...] if final else a

    _first_middle_last(step)


def matmul_norm_residual(y, w, h, g, *, name):
    m, k = y.shape
    n = w.shape[1]
    tm, tk = min(ROW_TILE, m), min(K_TILE, k)
    assert k // tk >= 2
    return pl.pallas_call(
        _matmul_norm_residual_kernel, grid=(m // tm, k // tk),
        in_specs=[pl.BlockSpec((tm, tk), lambda i, kk: (i, kk)),
                  pl.BlockSpec((tk, n), lambda i, kk: (kk, 0)),
                  pl.BlockSpec((tm, n), lambda i, kk: (i, 0)),
                  pl.BlockSpec((1, n), lambda i, kk: (0, 0))],
        out_specs=pl.BlockSpec((tm, n), lambda i, kk: (i, 0)),
        out_shape=jax.ShapeDtypeStruct((m, n), F32),
        compiler_params=_params("parallel", "arbitrary"), name=name)(y, w, h, g)


def _ffn_kernel(h_ref, gpre_ref, wg_ref, wu_ref, wd_ref, gpost_ref, o_ref, xn_ref):
    def step(first, final):
        xn = _normed_rows(h_ref, gpre_ref, xn_ref, first)
        gate = jnp.dot(xn, wg_ref[...], preferred_element_type=F32)
        up = jnp.dot(xn, wu_ref[...], preferred_element_type=F32)
        act = (_silu(gate) * up).astype(BF16)
        d = jnp.dot(act, wd_ref[...], preferred_element_type=F32)
        a = d if first else o_ref[...] + d
        o_ref[...] = h_ref[...] + a * _rms_scale(a) * gpost_ref[...] if final else a

    _first_middle_last(step)


def ffn(h, g_pre, w_gate, w_up, w_down, g_post, layer, *, name):
    m, d = h.shape
    dff = w_gate.shape[2]
    tm, tf = min(ROW_TILE, m), FFN_CHUNK
    assert dff // tf >= 2
    return pl.pallas_call(
        _ffn_kernel, grid=(m // tm, dff // tf),
        in_specs=[pl.BlockSpec((tm, d), lambda i, f: (i, 0)),
                  pl.BlockSpec((1, d), lambda i, f: (0, 0)),
                  pl.BlockSpec((None, d, tf), lambda i, f: (layer, 0, f)),
                  pl.BlockSpec((None, d, tf), lambda i, f: (layer, 0, f)),
                  pl.BlockSpec((None, tf, d), lambda i, f: (layer, f, 0)),
                  pl.BlockSpec((1, d), lambda i, f: (0, 0))],
        out_specs=pl.BlockSpec((tm, d), lambda i, f: (i, 0)),
        out_shape=jax.ShapeDtypeStruct((m, d), F32),
        scratch_shapes=[pltpu.VMEM((tm, d), BF16)],
        compiler_params=_params("parallel", "arbitrary"), name=name)(h, g_pre, w_gate, w_up, w_down, g_post)


def _expand_heads(v, rows):
    lane = lax.broadcasted_iota(jnp.int32, (rows, LANES), 1)
    first = lane < SSM_HEAD_DIM
    pairs = [jnp.where(first, v[:, 2 * p:2 * p + 1], v[:, 2 * p + 1:2 * p + 2]) for p in range(HEADS_PER_GROUP // 2)]
    return jnp.concatenate(pairs, axis=1)


def _split_bf16(v):
    hi = v.astype(BF16)
    lo = (v - hi.astype(F32)).astype(BF16)
    return jnp.concatenate([hi, lo], axis=1)


def _ssd_kernel(*refs, n_side):
    (z_ref, xs_ref, b_ref, c_ref, dt_ref, cw_ref, cb_ref, dtb_ref, alog_ref, dskip_ref, nw_ref) = refs[:N_SSD_INPUTS]
    side_in = refs[N_SSD_INPUTS:N_SSD_INPUTS + n_side]
    y_ref = refs[N_SSD_INPUTS + n_side]
    side_out = refs[N_SSD_INPUTS + n_side + 1:N_SSD_INPUTS + 2 * n_side + 1]
    (state_ref, tail_ref, pad_ref, act_ref, ccol_ref, crowT_ref, e1_ref, wl_ref,
     elast_ref) = refs[N_SSD_INPUTS + 2 * n_side + 1:]
    c = pl.program_id(0)
    g = pl.program_id(1)

    for w32_ref, w16_ref in zip(side_in, side_out):
        w16_ref[...] = w32_ref[...].astype(BF16)

    rows = z_ref.shape[0]
    n_sub = rows // SSD_SUB
    sub_row = lax.broadcasted_iota(jnp.int32, (SSD_SUB, SSD_SUB), 0)
    sub_col = lax.broadcasted_iota(jnp.int32, (SSD_SUB, SSD_SUB), 1)
    causal = sub_row >= sub_col

    @pl.when(c == 0)
    def _():
        state_ref[g] = jnp.zeros((SSM_STATE, GROUP_INNER), F32)
        tail_ref[g] = jnp.zeros((SUBLANES, GROUP_CONV), F32)

    @pl.when(g == 0)
    def _():
        dtv = jax.nn.softplus(dt_ref[...] + dtb_ref[...])
        dta = dtv * (-jnp.exp(alog_ref[...]))
        log_dt = jnp.log(dtv)
        tri = causal.astype(F32)
        for s in range(n_sub):
            sl = slice(s * SSD_SUB, (s + 1) * SSD_SUB)
            cum = jnp.dot(tri, dta[sl], precision=lax.Precision.HIGHEST,
                          preferred_element_type=F32)
            clast = cum[SSD_SUB - 1:SSD_SUB, :]
            ccol_ref[sl, :] = cum * LOG2E
            crowT_ref[:, sl] = ((cum - log_dt[sl]) * LOG2E).T
            e1_ref[sl, :] = _split_bf16(jnp.exp(cum))
            wl_ref[sl, :] = _split_bf16(jnp.exp(clast - cum) * dtv[sl])
            elast_ref[s] = jnp.exp(clast)

    xcur = jnp.concatenate([xs_ref[...].astype(F32), b_ref[...].astype(F32), c_ref[...].astype(F32)], axis=1)
    cw = cw_ref[...]
    bias = cb_ref[...]
    pad_ref[0:SUBLANES, :] = tail_ref[g]
    pad_ref[SUBLANES:2 * SUBLANES, :] = xcur[0:SUBLANES]
    tail_ref[g] = xcur[rows - SUBLANES:rows]
    head = bias
    body = bias + cw[CONV_WIDTH - 1:CONV_WIDTH, :] * xcur
    for k in range(CONV_WIDTH):
        lo = SUBLANES - (CONV_WIDTH - 1) + k
        head = head + cw[k:k + 1, :] * pad_ref[lo:lo + SUBLANES, :]
        if k < CONV_WIDTH - 1:
            body = body + cw[k:k + 1, :] * pltpu.roll(xcur, CONV_WIDTH - 1 - k, 0)
    act_ref[...] = _silu(jnp.concatenate([head, body[SUBLANES:]], axis=0))

    shift = lax.rem(LANES - HEADS_PER_GROUP * g, LANES)
    r0 = pl.multiple_of(g * HEADS_PER_GROUP, HEADS_PER_GROUP)
    lane16 = sub_col.astype(F32).astype(BF16)
    low_half = lane16 < SSM_HEAD_DIM
    zero16 = jnp.zeros((SSD_SUB, LANES), BF16)
    sel_row = lax.broadcasted_iota(jnp.int32, (2 * LANES, GROUP_INNER), 0) & (LANES - 1)
    sel_head = lax.broadcasted_iota(jnp.int32, (2 * LANES, GROUP_INNER), 1) // SSM_HEAD_DIM
    expand = (sel_row == sel_head + HEADS_PER_GROUP * g).astype(F32).astype(BF16)

    for s in range(n_sub):
        rs = slice(s * SSD_SUB, (s + 1) * SSD_SUB)
        act = act_ref[rs, :]
        xg = act[:, 0:GROUP_INNER]
        xb = xg.astype(BF16)
        bb = act[:, GROUP_INNER:GROUP_INNER + SSM_STATE].astype(BF16)
        cc = act[:, GROUP_INNER + SSM_STATE:GROUP_CONV].astype(BF16)
        ccol = pltpu.roll(ccol_ref[rs, :], shift, 1)
        crow = crowT_ref[pl.ds(r0, HEADS_PER_GROUP), rs]
        cbm = lax.dot_general(cc, bb, (((1,), (1,)), ((), ())), preferred_element_type=F32).astype(BF16)

        ys = []
        for p in range(HEADS_PER_GROUP // 2):
            ms = []
            for r in (2 * p, 2 * p + 1):
                seg = ccol[:, r:r + 1] - crow[r:r + 1, :]
                ms.append(cbm * jnp.exp2(jnp.where(causal, seg, -jnp.inf)).astype(BF16))
            xp = xb[:, p * LANES:(p + 1) * LANES]
            bd = jnp.concatenate([jnp.where(low_half, xp, zero16), jnp.where(low_half, zero16, xp)], axis=0)
            ys.append(jnp.dot(jnp.concatenate(ms, axis=1), bd, preferred_element_type=F32))
        y = jnp.concatenate(ys, axis=1)

        st = state_ref[g]
        e1 = jnp.dot(e1_ref[rs, :], expand, preferred_element_type=F32)
        y = y + jnp.dot(cc, st.astype(BF16), preferred_element_type=F32) * e1
        wl = jnp.dot(wl_ref[rs, :], expand, preferred_element_type=F32)
        xw = (xg * wl).astype(BF16)
        ds = lax.dot_general(bb, xw, (((0,), (0,)), ((), ())), preferred_element_type=F32)
        el = pltpu.roll(elast_ref[s], shift, 1)
        state_ref[g] = st * _expand_heads(el, 1) + ds

        y = y + dskip_ref[...] * xg
        y = y * _silu(z_ref[rs, :].astype(F32))
        y_ref[rs, :] = (y * _rms_scale(y) * nw_ref[...]).astype(y_ref.dtype)


def _slab_spec(w, n_steps):
    layers, r, c = w.shape
    steps_per_slab = 1 if (layers * r // n_steps) % (2 * SUBLANES) == 0 else 2
    n_slabs = n_steps // steps_per_slab
    assert (layers * r) % n_slabs == 0
    slab_rows = layers * r // n_slabs
    assert r % slab_rows == 0 and slab_rows % (2 * SUBLANES) == 0
    per_layer = r // slab_rows

    def index(cb, g):
        slab = (cb * SSM_GROUPS + g) // steps_per_slab
        return slab // per_layer, slab % per_layer, 0

    return pl.BlockSpec((None, slab_rows, c), index)


def ssd_mixer(proj, dt_raw, conv_w_g, conv_b_g, dt_bias, a_log, d_skip, norm_w, side_casts):
    seq = proj.shape[0]
    rows = min(SSD_ROWS, seq)
    n_sub = rows // SSD_SUB
    xcol0 = D_INNER // GROUP_INNER
    bcol0 = 2 * D_INNER // SSM_STATE
    ccol0 = bcol0 + GN // SSM_STATE
    n_steps = (seq // rows) * SSM_GROUPS
    w_specs = [_slab_spec(w, n_steps) for w in side_casts]
    return pl.pallas_call(
        functools.partial(_ssd_kernel, n_side=len(side_casts)), grid=(seq // rows, SSM_GROUPS),
        in_specs=[pl.BlockSpec((rows, GROUP_INNER), lambda c, g: (c, g)),
                  pl.BlockSpec((rows, GROUP_INNER), lambda c, g: (c, xcol0 + g)),
                  pl.BlockSpec((rows, SSM_STATE), lambda c, g: (c, bcol0 + g)),
                  pl.BlockSpec((rows, SSM_STATE), lambda c, g: (c, ccol0 + g)),
                  pl.BlockSpec((rows, LANES), lambda c, g: (c, 0)),
                  pl.BlockSpec((None, CONV_WIDTH, GROUP_CONV), lambda c, g: (g, 0, 0)),
                  pl.BlockSpec((None, 1, GROUP_CONV), lambda c, g: (g, 0, 0)),
                  pl.BlockSpec((1, LANES), lambda c, g: (0, 0)),
                  pl.BlockSpec((1, LANES), lambda c, g: (0, 0)),
                  pl.BlockSpec((1, GROUP_INNER), lambda c, g: (0, g)),
                  pl.BlockSpec((1, GROUP_INNER), lambda c, g: (0, g))] + w_specs,
        out_specs=[pl.BlockSpec((rows, GROUP_INNER), lambda c, g: (c, g))] + w_specs,
        out_shape=[jax.ShapeDtypeStruct((seq, D_INNER), BF16)]
        + [jax.ShapeDtypeStruct(w.shape, BF16) for w in side_casts],
        scratch_shapes=[pltpu.VMEM((SSM_GROUPS, SSM_STATE, GROUP_INNER), F32),
                        pltpu.VMEM((SSM_GROUPS, SUBLANES, GROUP_CONV), F32),
                        pltpu.VMEM((2 * SUBLANES, GROUP_CONV), F32),
                        pltpu.VMEM((rows, GROUP_CONV), F32),
                        pltpu.VMEM((rows, LANES), F32),
                        pltpu.VMEM((LANES, rows), F32),
                        pltpu.VMEM((rows, 2 * LANES), BF16),
                        pltpu.VMEM((rows, 2 * LANES), BF16),
                        pltpu.VMEM((n_sub, 1, LANES), F32)],
        compiler_params=_params("arbitrary", "arbitrary"), name="ssd_mixer",
    )(proj, proj, proj, proj, dt_raw, conv_w_g, conv_b_g, dt_bias, a_log, d_skip, norm_w, *side_casts)


def _attn_kernel(sink_ref, qt_ref, kvp_ref, kvc_ref, o_ref, bias_ref):
    i = pl.program_id(0)
    group_lanes = Q_PER_KV * BLOCK

    @pl.when(i == 0)
    def _():
        sj = lax.broadcasted_iota(jnp.int32, (2 * BLOCK, group_lanes), 0)
        lane = lax.broadcasted_iota(jnp.int32, (2 * BLOCK, group_lanes), 1)
        dist_i = BLOCK + (lane & (BLOCK - 1)) - sj
        valid = (dist_i >= 0) & (dist_i < WINDOW)
        dist = dist_i.astype(F32)
        head_in_group = (lane // BLOCK).astype(F32)
        for kh in range(N_KV_HEADS):
            slope = jnp.exp2((head_in_group + (kh * Q_PER_KV + 1)) * (-8.0 / N_Q_HEADS))
            b = jnp.where(valid, -(slope * dist), -jnp.inf)
            bias_ref[1, kh] = b
            bias_ref[0, kh] = jnp.where(sj >= BLOCK, b, -jnp.inf)

    has_prev = jnp.minimum(i, 1)
    scale = HEAD_DIM ** -0.5
    kv = jnp.concatenate([kvp_ref[...], kvc_ref[...]], axis=0)
    for kh in range(N_KV_HEADS):
        k = kv[:, kh * HEAD_DIM:(kh + 1) * HEAD_DIM] * scale
        v = kv[:, KV_DIM + kh * HEAD_DIM:KV_DIM + (kh + 1) * HEAD_DIM]
        h0 = kh * Q_PER_KV
        qg = jnp.concatenate([qt_ref[(h0 + r) * HEAD_DIM:(h0 + r + 1) * HEAD_DIM, :] for r in range(Q_PER_KV)],
                             axis=1)
        sink = sink_ref[kh]
        logits = jnp.dot(k, qg, preferred_element_type=F32) + bias_ref[has_prev, kh]
        m = jnp.maximum(jnp.max(logits, axis=0, keepdims=True), sink)
        p = jnp.exp(logits - m)
        denom = jnp.sum(p, axis=0, keepdims=True) + jnp.exp(sink - m)
        ot = lax.dot_general(v, p.astype(BF16), (((0,), (0,)), ((), ())), preferred_element_type=F32) / denom
        for r in range(0, Q_PER_KV, 2):
            pair = jnp.concatenate([ot[:, r * BLOCK:(r + 1) * BLOCK], ot[:, (r + 1) * BLOCK:(r + 2) * BLOCK]], axis=0)
            o_ref[:, (h0 + r) * HEAD_DIM:(h0 + r + 2) * HEAD_DIM] = pair.T.astype(o_ref.dtype)


def swa_attention(qt, kv, sinks):
    seq = kv.shape[0]
    nb = seq // BLOCK
    sink_rows = jnp.repeat(sinks.astype(F32), BLOCK).reshape(N_KV_HEADS, 1, Q_PER_KV * BLOCK)
    return pl.pallas_call(
        _attn_kernel, grid=(nb,),
        in_specs=[pl.BlockSpec((N_KV_HEADS, 1, Q_PER_KV * BLOCK), lambda i: (0, 0, 0)),
                  pl.BlockSpec((N_Q_HEADS * HEAD_DIM, BLOCK), lambda i: (0, i)),
                  pl.BlockSpec((BLOCK, 2 * KV_DIM), lambda i: (jnp.maximum(i - 1, 0), 0)),
                  pl.BlockSpec((BLOCK, 2 * KV_DIM), lambda i: (i, 0))],
        out_specs=pl.BlockSpec((BLOCK, N_Q_HEADS * HEAD_DIM), lambda i: (i, 0)),
        out_shape=jax.ShapeDtypeStruct((seq, N_Q_HEADS * HEAD_DIM), BF16),
        scratch_shapes=[pltpu.VMEM((2, N_KV_HEADS, 2 * BLOCK, Q_PER_KV * BLOCK), F32)],
        compiler_params=_params("arbitrary"), name="swa_attention",
    )(sink_rows, qt, kv, kv)


def _pad_lanes(v, width=LANES):
    v = v.reshape(1, -1).astype(F32)
    return jnp.pad(v, ((0, 0), (0, width - v.shape[1])))


def kernel(x, norm_w, ssm_w_in, ssm_conv_w, ssm_conv_b, ssm_dt_bias, ssm_A_log, ssm_D, ssm_norm_w, ssm_w_out,
           kv_norm_w, w_kv, attn_w_q, attn_sinks, attn_w_o, ffn_w_gate, ffn_w_up, ffn_w_down):
    batch, seq, d = x.shape
    assert batch == 1 and d == D_MODEL and seq % SSD_ROWS == 0
    h = x.reshape(seq, d)
    row = lambda v: v.reshape(1, -1).astype(F32)

    g = norm_w[0]
    w_in_t = ssm_w_in[0].T.astype(BF16)
    n_main = 2 * D_INNER + 2 * GN
    w_dt_t = jnp.pad(w_in_t[n_main:], ((0, LANES - SSM_HEADS), (0, 0)))
    proj, dt_raw = norm_matmul2_nt(h, row(g[0]), w_in_t, w_dt_t, n=n_main, name="in_proj")

    def per_group(t):
        xs = t[:, :D_INNER].reshape(-1, SSM_GROUPS, GROUP_INNER)
        bs = t[:, D_INNER:D_INNER + GN].reshape(-1, SSM_GROUPS, SSM_STATE)
        cs = t[:, D_INNER + GN:].reshape(-1, SSM_GROUPS, SSM_STATE)
        return jnp.transpose(jnp.concatenate([xs, bs, cs], axis=-1), (1, 0, 2)).astype(F32)

    later_weights = (ffn_w_gate, ffn_w_up, ffn_w_down, ssm_w_out, attn_w_q, attn_w_o, w_kv[None])
    y, w_gate, w_up, w_down, w_out, w_q, w_o, w_kv16 = ssd_mixer(
        proj, dt_raw, per_group(ssm_conv_w[0]), per_group(ssm_conv_b[0].reshape(1, -1)),
        _pad_lanes(ssm_dt_bias[0]), _pad_lanes(ssm_A_log[0]),
        row(jnp.repeat(ssm_D[0], SSM_HEAD_DIM)), row(ssm_norm_w[0]), later_weights)
    h = matmul_norm_residual(y, w_out[0], h, row(g[1]), name="ssm_out_proj")
    h = ffn(h, row(g[2]), w_gate, w_up, w_down, row(g[3]), 0, name="ffn0")

    g = norm_w[1]
    kv = norm_matmul(h, row(kv_norm_w), w_kv16[0], name="kv_proj")
    qt = norm_matmul_t(h, row(g[0]), w_q[0], name="q_proj")
    a = swa_attention(qt, kv, attn_sinks[0])
    h = matmul_norm_residual(a, w_o[0], h, row(g[1]), name="attn_out_proj")
    h = ffn(h, row(g[2]), w_gate, w_up, w_down, row(g[3]), 1, name="ffn1")
    return h.reshape(batch, seq, d)
```

```python
import functools
import math

import jax
import jax.numpy as jnp
from jax import lax
from jax.experimental import pallas as pl
from jax.experimental.pallas import tpu as pltpu

D_MODEL = 2048
D_INNER = 4096
SSM_HEAD_DIM = 64
SSM_HEADS = 64
SSM_GROUPS = 8
HEADS_PER_GROUP = SSM_HEADS // SSM_GROUPS
SSM_STATE = 128
CONV_WIDTH = 4
GN = SSM_GROUPS * SSM_STATE
GROUP_INNER = D_INNER // SSM_GROUPS
GROUP_CONV = GROUP_INNER + 2 * SSM_STATE
N_SSD_INPUTS = 11
SSD_SUB = 128
SSD_ROWS = 1024
N_Q_HEADS = 32
N_KV_HEADS = 4
Q_PER_KV = N_Q_HEADS // N_KV_HEADS
HEAD_DIM = 64
WINDOW = 128
BLOCK = 128
KV_DIM = N_KV_HEADS * HEAD_DIM
EPS = 1e-6
LOG2E = math.log2(math.e)

LANES = 128
SUBLANES = 8
VMEM_LIMIT_BYTES = 63 * 1024 * 1024

ROW_TILE = 1024
COL_TILE = 1024
IN_PROJ_COL_TILE = 1024
K_TILE = 1024
FFN_CHUNK = 512

BF16 = jnp.bfloat16
F32 = jnp.float32


def _rms_scale(x):
    return lax.rsqrt(jnp.mean(x * x, axis=-1, keepdims=True) + EPS)


def _silu(x):
    return x * jax.nn.sigmoid(x)


def _params(*semantics):
    return pltpu.CompilerParams(dimension_semantics=semantics, vmem_limit_bytes=VMEM_LIMIT_BYTES)


def _normed_rows(x_ref, g_ref, xn_ref, first):
    if not first:
        return xn_ref[...]
    x = x_ref[...]
    xn = (x * _rms_scale(x) * g_ref[...]).astype(xn_ref.dtype)
    xn_ref[...] = xn
    return xn


def _first_and_rest(step):
    j = pl.program_id(1)
    pl.when(j == 0)(lambda: step(True))
    pl.when(j > 0)(lambda: step(False))


def _first_middle_last(step):
    j = pl.program_id(1)
    last = pl.num_programs(1) - 1
    pl.when(j == 0)(lambda: step(True, False))
    pl.when(jnp.logical_and(j > 0, j < last))(lambda: step(False, False))
    pl.when(j == last)(lambda: step(False, True))


def _norm_matmul_kernel(x_ref, g_ref, w_ref, o_ref, xn_ref):
    def step(first):
        xn = _normed_rows(x_ref, g_ref, xn_ref, first)
        o_ref[...] = jnp.dot(xn, w_ref[...], preferred_element_type=F32).astype(o_ref.dtype)

    _first_and_rest(step)


def _norm_matmul2_kernel(x_ref, g_ref, wt_ref, wt2_ref, o_ref, o2_ref, xn_ref):
    nt = (((1,), (1,)), ((), ()))

    def step(first):
        xn = _normed_rows(x_ref, g_ref, xn_ref, first)
        if first:
            o2_ref[...] = lax.dot_general(xn, wt2_ref[...], nt, preferred_element_type=F32)
        wt = wt_ref[...].astype(BF16)
        o_ref[...] = lax.dot_general(xn, wt, nt, preferred_element_type=F32).astype(o_ref.dtype)

    _first_and_rest(step)


def _norm_matmul_t_kernel(x_ref, g_ref, w_ref, o_ref, xn_ref):
    def step(first):
        xn = _normed_rows(x_ref, g_ref, xn_ref, first)
        o_ref[...] = lax.dot_general(w_ref[...], xn, (((0,), (1,)), ((), ())),
                                     preferred_element_type=F32).astype(o_ref.dtype)

    _first_and_rest(step)


def norm_matmul_t(x, g, w, *, name):
    m, k = x.shape
    n = w.shape[1]
    tm, tn = min(ROW_TILE, m), min(COL_TILE, n)
    return pl.pallas_call(
        _norm_matmul_t_kernel, grid=(m // tm, n // tn),
        in_specs=[pl.BlockSpec((tm, k), lambda i, j: (i, 0)),
                  pl.BlockSpec((1, k), lambda i, j: (0, 0)),
                  pl.BlockSpec((k, tn), lambda i, j: (0, j))],
        out_specs=pl.BlockSpec((tn, tm), lambda i, j: (j, i)),
        out_shape=jax.ShapeDtypeStruct((n, m), BF16), scratch_shapes=[pltpu.VMEM((tm, k), BF16)],
        compiler_params=_params("parallel", "arbitrary"), name=name)(x, g, w)


def norm_matmul(x, g, w, *, name):
    m, k = x.shape
    n = w.shape[1]
    tm, tn = min(ROW_TILE, m), min(COL_TILE, n)
    return pl.pallas_call(
        _norm_matmul_kernel, grid=(m // tm, n // tn),
        in_specs=[pl.BlockSpec((tm, k), lambda i, j: (i, 0)),
                  pl.BlockSpec((1, k), lambda i, j: (0, 0)),
                  pl.BlockSpec((k, tn), lambda i, j: (0, j))],
        out_specs=pl.BlockSpec((tm, tn), lambda i, j: (i, j)),
        out_shape=jax.ShapeDtypeStruct((m, n), BF16), scratch_shapes=[pltpu.VMEM((tm, k), BF16)],
        compiler_params=_params("parallel", "arbitrary"), name=name)(x, g, w)


def norm_matmul2_nt(x, g, wt, wt2, *, n, name):
    m, k = x.shape
    n2 = wt2.shape[0]
    tm, tn = min(ROW_TILE, m), min(IN_PROJ_COL_TILE, n)
    assert n % tn == 0
    return pl.pallas_call(
        _norm_matmul2_kernel, grid=(m // tm, n // tn),
        in_specs=[pl.BlockSpec((tm, k), lambda i, j: (i, 0)),
                  pl.BlockSpec((1, k), lambda i, j: (0, 0)),
                  pl.BlockSpec((tn, k), lambda i, j: (j, 0)),
                  pl.BlockSpec((n2, k), lambda i, j: (0, 0))],
        out_specs=[pl.BlockSpec((tm, tn), lambda i, j: (i, j)), pl.BlockSpec((tm, n2), lambda i, j: (i, 0))],
        out_shape=[jax.ShapeDtypeStruct((m, n), BF16), jax.ShapeDtypeStruct((m, n2), F32)],
        scratch_shapes=[pltpu.VMEM((tm, k), BF16)],
        compiler_params=_params("parallel", "arbitrary"), name=name)(x, g, wt, wt2)


def _matmul_norm_residual_kernel(y_ref, w_ref, h_ref, g_ref, o_ref):
    def step(first, final):
        d = jnp.dot(y_ref[...], w_ref[...], preferred_element_type=F32)
        a = d if first else o_ref[...] + d
        o_ref[...] = h_ref[...] + a * _rms_scale(a) * g_ref[...] if final else a

    _first_middle_last(step)


def matmul_norm_residual(y, w, h, g, *, name):
    m, k = y.shape
    n = w.shape[1]
    tm, tk = min(ROW_TILE, m), min(K_TILE, k)
    assert k // tk >= 2
    return pl.pallas_call(
        _matmul_norm_residual_kernel, grid=(m // tm, k // tk),
        in_specs=[pl.BlockSpec((tm, tk), lambda i, kk: (i, kk)),
                  pl.BlockSpec((tk, n), lambda i, kk: (kk, 0)),
                  pl.BlockSpec((tm, n), lambda i, kk: (i, 0)),
                  pl.BlockSpec((1, n), lambda i, kk: (0, 0))],
        out_specs=pl.BlockSpec((tm, n), lambda i, kk: (i, 0)),
        out_shape=jax.ShapeDtypeStruct((m, n), F32),
        compiler_params=_params("parallel", "arbitrary"), name=name)(y, w, h, g)


def _ffn_kernel(h_ref, gpre_ref, wg_ref, wu_ref, wd_ref, gpost_ref, o_ref, xn_ref):
    def step(first, final):
        xn = _normed_rows(h_ref, gpre_ref, xn_ref, first)
        gate = jnp.dot(xn, wg_ref[...], preferred_element_type=F32)
        up = jnp.dot(xn, wu_ref[...], preferred_element_type=F32)
        act = (_silu(gate) * up).astype(BF16)
        d = jnp.dot(act, wd_ref[...], preferred_element_type=F32)
        a = d if first else o_ref[...] + d
        o_ref[...] = h_ref[...] + a * _rms_scale(a) * gpost_ref[...] if final else a

    _first_middle_last(step)


def ffn(h, g_pre, w_gate, w_up, w_down, g_post, layer, *, name):
    m, d = h.shape
    dff = w_gate.shape[2]
    tm, tf = min(ROW_TILE, m), FFN_CHUNK
    assert dff // tf >= 2
    return pl.pallas_call(
        _ffn_kernel, grid=(m // tm, dff // tf),
        in_specs=[pl.BlockSpec((tm, d), lambda i, f: (i, 0)),
                  pl.BlockSpec((1, d), lambda i, f: (0, 0)),
                  pl.BlockSpec((None, d, tf), lambda i, f: (layer, 0, f)),
                  pl.BlockSpec((None, d, tf), lambda i, f: (layer, 0, f)),
                  pl.BlockSpec((None, tf, d), lambda i, f: (layer, f, 0)),
                  pl.BlockSpec((1, d), lambda i, f: (0, 0))],
        out_specs=pl.BlockSpec((tm, d), lambda i, f: (i, 0)),
        out_shape=jax.ShapeDtypeStruct((m, d), F32),
        scratch_shapes=[pltpu.VMEM((tm, d), BF16)],
        compiler_params=_params("parallel", "arbitrary"), name=name)(h, g_pre, w_gate, w_up, w_down, g_post)


def _expand_heads(v, rows):
    lane = lax.broadcasted_iota(jnp.int32, (rows, LANES), 1)
    first = lane < SSM_HEAD_DIM
    pairs = [jnp.where(first, v[:, 2 * p:2 * p + 1], v[:, 2 * p + 1:2 * p + 2]) for p in range(HEADS_PER_GROUP // 2)]
    return jnp.concatenate(pairs, axis=1)


def _split_bf16(v):
    hi = v.astype(BF16)
    lo = (v - hi.astype(F32)).astype(BF16)
    return jnp.concatenate([hi, lo], axis=1)


def _ssd_kernel(*refs, n_side):
    (z_ref, xs_ref, b_ref, c_ref, dt_ref, cw_ref, cb_ref, dtb_ref, alog_ref, dskip_ref, nw_ref) = refs[:N_SSD_INPUTS]
    side_in = refs[N_SSD_INPUTS:N_SSD_INPUTS + n_side]
    y_ref = refs[N_SSD_INPUTS + n_side]
    side_out = refs[N_SSD_INPUTS + n_side + 1:N_SSD_INPUTS + 2 * n_side + 1]
    (state_ref, tail_ref, pad_ref, act_ref, ccol_ref, crowT_ref, e1_ref, wl_ref,
     elast_ref) = refs[N_SSD_INPUTS + 2 * n_side + 1:]
    c = pl.program_id(0)
    g = pl.program_id(1)

    for w32_ref, w16_ref in zip(side_in, side_out):
        w16_ref[...] = w32_ref[...].astype(BF16)

    rows = z_ref.shape[0]
    n_sub = rows // SSD_SUB
    sub_row = lax.broadcasted_iota(jnp.int32, (SSD_SUB, SSD_SUB), 0)
    sub_col = lax.broadcasted_iota(jnp.int32, (SSD_SUB, SSD_SUB), 1)
    causal = sub_row >= sub_col

    @pl.when(c == 0)
    def _():
        state_ref[g] = jnp.zeros((SSM_STATE, GROUP_INNER), F32)
        tail_ref[g] = jnp.zeros((SUBLANES, GROUP_CONV), F32)

    @pl.when(g == 0)
    def _():
        dtv = jax.nn.softplus(dt_ref[...] + dtb_ref[...])
        dta = dtv * (-jnp.exp(alog_ref[...]))
        log_dt = jnp.log(dtv)
        tri = causal.astype(F32)
        for s in range(n_sub):
            sl = slice(s * SSD_SUB, (s + 1) * SSD_SUB)
            cum = jnp.dot(tri, dta[sl], precision=lax.Precision.HIGHEST,
                          preferred_element_type=F32)
            clast = cum[SSD_SUB - 1:SSD_SUB, :]
            ccol_ref[sl, :] = cum * LOG2E
            crowT_ref[:, sl] = ((cum - log_dt[sl]) * LOG2E).T
            e1_ref[sl, :] = _split_bf16(jnp.exp(cum))
            wl_ref[sl, :] = _split_bf16(jnp.exp(clast - cum) * dtv[sl])
            elast_ref[s] = jnp.exp(clast)

    xcur = jnp.concatenate([xs_ref[...].astype(F32), b_ref[...].astype(F32), c_ref[...].astype(F32)], axis=1)
    cw = cw_ref[...]
    bias = cb_ref[...]
    pad_ref[0:SUBLANES, :] = tail_ref[g]
    pad_ref[SUBLANES:2 * SUBLANES, :] = xcur[0:SUBLANES]
    tail_ref[g] = xcur[rows - SUBLANES:rows]
    head = bias
    body = bias + cw[CONV_WIDTH - 1:CONV_WIDTH, :] * xcur
    for k in range(CONV_WIDTH):
        lo = SUBLANES - (CONV_WIDTH - 1) + k
        head = head + cw[k:k + 1, :] * pad_ref[lo:lo + SUBLANES, :]
        if k < CONV_WIDTH - 1:
            body = body + cw[k:k + 1, :] * pltpu.roll(xcur, CONV_WIDTH - 1 - k, 0)
    act_ref[...] = _silu(jnp.concatenate([head, body[SUBLANES:]], axis=0))

    shift = lax.rem(LANES - HEADS_PER_GROUP * g, LANES)
    r0 = pl.multiple_of(g * HEADS_PER_GROUP, HEADS_PER_GROUP)
    lane16 = sub_col.astype(F32).astype(BF16)
    low_half = lane16 < SSM_HEAD_DIM
    zero16 = jnp.zeros((SSD_SUB, LANES), BF16)
    sel_row = lax.broadcasted_iota(jnp.int32, (2 * LANES, GROUP_INNER), 0) & (LANES - 1)
    sel_head = lax.broadcasted_iota(jnp.int32, (2 * LANES, GROUP_INNER), 1) // SSM_HEAD_DIM
    expand = (sel_row == sel_head + HEADS_PER_GROUP * g).astype(F32).astype(BF16)

    for s in range(n_sub):
        rs = slice(s * SSD_SUB, (s + 1) * SSD_SUB)
        act = act_ref[rs, :]
        xg = act[:, 0:GROUP_INNER]
        xb = xg.astype(BF16)
        bb = act[:, GROUP_INNER:GROUP_INNER + SSM_STATE].astype(BF16)
        cc = act[:, GROUP_INNER + SSM_STATE:GROUP_CONV].astype(BF16)
        ccol = pltpu.roll(ccol_ref[rs, :], shift, 1)
        crow = crowT_ref[pl.ds(r0, HEADS_PER_GROUP), rs]
        cbm = lax.dot_general(cc, bb, (((1,), (1,)), ((), ())), preferred_element_type=F32).astype(BF16)

        ys = []
        for p in range(HEADS_PER_GROUP // 2):
            ms = []
            for r in (2 * p, 2 * p + 1):
                seg = ccol[:, r:r + 1] - crow[r:r + 1, :]
                ms.append(cbm * jnp.exp2(jnp.where(causal, seg, -jnp.inf)).astype(BF16))
            xp = xb[:, p * LANES:(p + 1) * LANES]
            bd = jnp.concatenate([jnp.where(low_half, xp, zero16), jnp.where(low_half, zero16, xp)], axis=0)
            ys.append(jnp.dot(jnp.concatenate(ms, axis=1), bd, preferred_element_type=F32))
        y = jnp.concatenate(ys, axis=1)

        st = state_ref[g]
        e1 = jnp.dot(e1_ref[rs, :], expand, preferred_element_type=F32)
        y = y + jnp.dot(cc, st.astype(BF16), preferred_element_type=F32) * e1
        wl = jnp.dot(wl_ref[rs, :], expand, preferred_element_type=F32)
        xw = (xg * wl).astype(BF16)
        ds = lax.dot_general(bb, xw, (((0,), (0,)), ((), ())), preferred_element_type=F32)
        el = pltpu.roll(elast_ref[s], shift, 1)
        state_ref[g] = st * _expand_heads(el, 1) + ds

        y = y + dskip_ref[...] * xg
        y = y * _silu(z_ref[rs, :].astype(F32))
        y_ref[rs, :] = (y * _rms_scale(y) * nw_ref[...]).astype(y_ref.dtype)


def _slab_spec(w, n_steps):
    layers, r, c = w.shape
    steps_per_slab = 1 if (layers * r // n_steps) % (2 * SUBLANES) == 0 else 2
    n_slabs = n_steps // steps_per_slab
    assert (layers * r) % n_slabs == 0
    slab_rows = layers * r // n_slabs
    assert r % slab_rows == 0 and slab_rows % (2 * SUBLANES) == 0
    per_layer = r // slab_rows

    def index(cb, g):
        slab = (cb * SSM_GROUPS + g) // steps_per_slab
        return slab // per_layer, slab % per_layer, 0

    return pl.BlockSpec((None, slab_rows, c), index)


def ssd_mixer(proj, dt_raw, conv_w_g, conv_b_g, dt_bias, a_log, d_skip, norm_w, side_casts):
    seq = proj.shape[0]
    rows = min(SSD_ROWS, seq)
    n_sub = rows // SSD_SUB
    xcol0 = D_INNER // GROUP_INNER
    bcol0 = 2 * D_INNER // SSM_STATE
    ccol0 = bcol0 + GN // SSM_STATE
    n_steps = (seq // rows) * SSM_GROUPS
    w_specs = [_slab_spec(w, n_steps) for w in side_casts]
    return pl.pallas_call(
        functools.partial(_ssd_kernel, n_side=len(side_casts)), grid=(seq // rows, SSM_GROUPS),
        in_specs=[pl.BlockSpec((rows, GROUP_INNER), lambda c, g: (c, g)),
                  pl.BlockSpec((rows, GROUP_INNER), lambda c, g: (c, xcol0 + g)),
                  pl.BlockSpec((rows, SSM_STATE), lambda c, g: (c, bcol0 + g)),
                  pl.BlockSpec((rows, SSM_STATE), lambda c, g: (c, ccol0 + g)),
                  pl.BlockSpec((rows, LANES), lambda c, g: (c, 0)),
                  pl.BlockSpec((None, CONV_WIDTH, GROUP_CONV), lambda c, g: (g, 0, 0)),
                  pl.BlockSpec((None, 1, GROUP_CONV), lambda c, g: (g, 0, 0)),
                  pl.BlockSpec((1, LANES), lambda c, g: (0, 0)),
                  pl.BlockSpec((1, LANES), lambda c, g: (0, 0)),
                  pl.BlockSpec((1, GROUP_INNER), lambda c, g: (0, g)),
                  pl.BlockSpec((1, GROUP_INNER), lambda c, g: (0, g))] + w_specs,
        out_specs=[pl.BlockSpec((rows, GROUP_INNER), lambda c, g: (c, g))] + w_specs,
        out_shape=[jax.ShapeDtypeStruct((seq, D_INNER), BF16)]
        + [jax.ShapeDtypeStruct(w.shape, BF16) for w in side_casts],
        scratch_shapes=[pltpu.VMEM((SSM_GROUPS, SSM_STATE, GROUP_INNER), F32),
                        pltpu.VMEM((SSM_GROUPS, SUBLANES, GROUP_CONV), F32),
                        pltpu.VMEM((2 * SUBLANES, GROUP_CONV), F32),
                        pltpu.VMEM((rows, GROUP_CONV), F32),
                        pltpu.VMEM((rows, LANES), F32),
                        pltpu.VMEM((LANES, rows), F32),
                        pltpu.VMEM((rows, 2 * LANES), BF16),
                        pltpu.VMEM((rows, 2 * LANES), BF16),
                        pltpu.VMEM((n_sub, 1, LANES), F32)],
        compiler_params=_params("arbitrary", "arbitrary"), name="ssd_mixer",
    )(proj, proj, proj, proj, dt_raw, conv_w_g, conv_b_g, dt_bias, a_log, d_skip, norm_w, *side_casts)


def _attn_kernel(sink_ref, qt_ref, kvp_ref, kvc_ref, o_ref, bias_ref):
    i = pl.program_id(0)
    group_lanes = Q_PER_KV * BLOCK

    @pl.when(i == 0)
    def _():
        sj = lax.broadcasted_iota(jnp.int32, (2 * BLOCK, group_lanes), 0)
        lane = lax.broadcasted_iota(jnp.int32, (2 * BLOCK, group_lanes), 1)
        dist_i = BLOCK + (lane & (BLOCK - 1)) - sj
        valid = (dist_i >= 0) & (dist_i < WINDOW)
        dist = dist_i.astype(F32)
        head_in_group = (lane // BLOCK).astype(F32)
        for kh in range(N_KV_HEADS):
            slope = jnp.exp2((head_in_group + (kh * Q_PER_KV + 1)) * (-8.0 / N_Q_HEADS))
            b = jnp.where(valid, -(slope * dist), -jnp.inf)
            bias_ref[1, kh] = b
            bias_ref[0, kh] = jnp.where(sj >= BLOCK, b, -jnp.inf)

    has_prev = jnp.minimum(i, 1)
    scale = HEAD_DIM ** -0.5
    kv = jnp.concatenate([kvp_ref[...], kvc_ref[...]], axis=0)
    for kh in range(N_KV_HEADS):
        k = kv[:, kh * HEAD_DIM:(kh + 1) * HEAD_DIM] * scale
        v = kv[:, KV_DIM + kh * HEAD_DIM:KV_DIM + (kh + 1) * HEAD_DIM]
        h0 = kh * Q_PER_KV
        qg = jnp.concatenate([qt_ref[(h0 + r) * HEAD_DIM:(h0 + r + 1) * HEAD_DIM, :] for r in range(Q_PER_KV)],
                             axis=1)
        sink = sink_ref[kh]
        logits = jnp.dot(k, qg, preferred_element_type=F32) + bias_ref[has_prev, kh]
        m = jnp.maximum(jnp.max(logits, axis=0, keepdims=True), sink)
        p = jnp.exp(logits - m)
        denom = jnp.sum(p, axis=0, keepdims=True) + jnp.exp(sink - m)
        ot = lax.dot_general(v, p.astype(BF16), (((0,), (0,)), ((), ())), preferred_element_type=F32) / denom
        for r in range(0, Q_PER_KV, 2):
            pair = jnp.concatenate([ot[:, r * BLOCK:(r + 1) * BLOCK], ot[:, (r + 1) * BLOCK:(r + 2) * BLOCK]], axis=0)
            o_ref[:, (h0 + r) * HEAD_DIM:(h0 + r + 2) * HEAD_DIM] = pair.T.astype(o_ref.dtype)


def swa_attention(qt, kv, sinks):
    seq = kv.shape[0]
    nb = seq // BLOCK
    sink_rows = jnp.repeat(sinks.astype(F32), BLOCK).reshape(N_KV_HEADS, 1, Q_PER_KV * BLOCK)
    return pl.pallas_call(
        _attn_kernel, grid=(nb,),
        in_specs=[pl.BlockSpec((N_KV_HEADS, 1, Q_PER_KV * BLOCK), lambda i: (0, 0, 0)),
                  pl.BlockSpec((N_Q_HEADS * HEAD_DIM, BLOCK), lambda i: (0, i)),
                  pl.BlockSpec((BLOCK, 2 * KV_DIM), lambda i: (jnp.maximum(i - 1, 0), 0)),
                  pl.BlockSpec((BLOCK, 2 * KV_DIM), lambda i: (i, 0))],
        out_specs=pl.BlockSpec((BLOCK, N_Q_HEADS * HEAD_DIM), lambda i: (i, 0)),
        out_shape=jax.ShapeDtypeStruct((seq, N_Q_HEADS * HEAD_DIM), BF16),
        scratch_shapes=[pltpu.VMEM((2, N_KV_HEADS, 2 * BLOCK, Q_PER_KV * BLOCK), F32)],
        compiler_params=_params("arbitrary"), name="swa_attention",
    )(sink_rows, qt, kv, kv)


def _pad_lanes(v, width=LANES):
    v = v.reshape(1, -1).astype(F32)
    return jnp.pad(v, ((0, 0), (0, width - v.shape[1])))


def kernel(x, norm_w, ssm_w_in, ssm_conv_w, ssm_conv_b, ssm_dt_bias, ssm_A_log, ssm_D, ssm_norm_w, ssm_w_out,
           kv_norm_w, w_kv, attn_w_q, attn_sinks, attn_w_o, ffn_w_gate, ffn_w_up, ffn_w_down):
    batch, seq, d = x.shape
    assert batch == 1 and d == D_MODEL and seq % SSD_ROWS == 0
    h = x.reshape(seq, d)
    row = lambda v: v.reshape(1, -1).astype(F32)

    g = norm_w[0]
    w_in_t = ssm_w_in[0].T
    n_main = 2 * D_INNER + 2 * GN
    w_dt_t = jnp.pad(w_in_t[n_main:], ((0, LANES - SSM_HEADS), (0, 0))).astype(BF16)
    proj, dt_raw = norm_matmul2_nt(h, row(g[0]), w_in_t, w_dt_t, n=n_main, name="in_proj")

    def per_group(t):
        xs = t[:, :D_INNER].reshape(-1, SSM_GROUPS, GROUP_INNER)
        bs = t[:, D_INNER:D_INNER + GN].reshape(-1, SSM_GROUPS, SSM_STATE)
        cs = t[:, D_INNER + GN:].reshape(-1, SSM_GROUPS, SSM_STATE)
        return jnp.transpose(jnp.concatenate([xs, bs, cs], axis=-1), (1, 0, 2)).astype(F32)

    later_weights = (ffn_w_gate, ffn_w_up, ffn_w_down, ssm_w_out, attn_w_q, attn_w_o, w_kv[None])
    y, w_gate, w_up, w_down, w_out, w_q, w_o, w_kv16 = ssd_mixer(
        proj, dt_raw, per_group(ssm_conv_w[0]), per_group(ssm_conv_b[0].reshape(1, -1)),
        _pad_lanes(ssm_dt_bias[0]), _pad_lanes(ssm_A_log[0]),
        row(jnp.repeat(ssm_D[0], SSM_HEAD_DIM)), row(ssm_norm_w[0]), later_weights)
    h = matmul_norm_residual(y, w_out[0], h, row(g[1]), name="ssm_out_proj")
    h = ffn(h, row(g[2]), w_gate, w_up, w_down, row(g[3]), 0, name="ffn0")

    g = norm_w[1]
    kv = norm_matmul(h, row(kv_norm_w), w_kv16[0], name="kv_proj")
    qt = norm_matmul_t(h, row(g[0]), w_q[0], name="q_proj")
    a = swa_attention(qt, kv, attn_sinks[0])
    h = matmul_norm_residual(a, w_o[0], h, row(g[1]), name="attn_out_proj")
    h = ffn(h, row(g[2]), w_gate, w_up, w_down, row(g[3]), 1, name="ffn1")
    return h.reshape(batch, seq, d)
```

```python
import functools
import math

import jax
import jax.numpy as jnp
from jax import lax
from jax.experimental import pallas as pl
from jax.experimental.pallas import tpu as pltpu

D_MODEL = 2048
D_INNER = 4096
SSM_HEAD_DIM = 64
SSM_HEADS = 64
SSM_GROUPS = 8
HEADS_PER_GROUP = SSM_HEADS // SSM_GROUPS
SSM_STATE = 128
CONV_WIDTH = 4
GN = SSM_GROUPS * SSM_STATE
GROUP_INNER = D_INNER // SSM_GROUPS
GROUP_CONV = GROUP_INNER + 2 * SSM_STATE
N_SSD_INPUTS = 11
SSD_SUB = 128
SSD_ROWS = 1024
N_Q_HEADS = 32
N_KV_HEADS = 4
Q_PER_KV = N_Q_HEADS // N_KV_HEADS
HEAD_DIM = 64
WINDOW = 128
BLOCK = 128
ATTN_BLOCKS = 4
KV_DIM = N_KV_HEADS * HEAD_DIM
EPS = 1e-6
LOG2E = math.log2(math.e)

LANES = 128
SUBLANES = 8
VMEM_LIMIT_BYTES = 63 * 1024 * 1024

ROW_TILE = 1024
COL_TILE = 1024
IN_PROJ_COL_TILE = 2048
K_TILE = 1024
FFN_CHUNK = 512

BF16 = jnp.bfloat16
F32 = jnp.float32


def _rms_scale(x):
    return lax.rsqrt(jnp.mean(x * x, axis=-1, keepdims=True) + EPS)


def _silu(x):
    return x * jax.nn.sigmoid(x)


def _params(*semantics):
    return pltpu.CompilerParams(dimension_semantics=semantics, vmem_limit_bytes=VMEM_LIMIT_BYTES)


def _normed_rows(x_ref, g_ref, xn_ref, first):
    if not first:
        return xn_ref[...]
    x = x_ref[...]
    xn = (x * _rms_scale(x) * g_ref[...]).astype(xn_ref.dtype)
    xn_ref[...] = xn
    return xn


def _first_and_rest(step):
    j = pl.program_id(1)
    pl.when(j == 0)(lambda: step(True))
    pl.when(j > 0)(lambda: step(False))


def _first_middle_last(step):
    j = pl.program_id(1)
    last = pl.num_programs(1) - 1
    pl.when(j == 0)(lambda: step(True, False))
    pl.when(jnp.logical_and(j > 0, j < last))(lambda: step(False, False))
    pl.when(j == last)(lambda: step(False, True))


def _norm_matmul_kernel(x_ref, g_ref, w_ref, o_ref, xn_ref):
    def step(first):
        xn = _normed_rows(x_ref, g_ref, xn_ref, first)
        o_ref[...] = jnp.dot(xn, w_ref[...], preferred_element_type=F32).astype(o_ref.dtype)

    _first_and_rest(step)


def _norm_matmul2_kernel(x_ref, g_ref, wt_ref, wt2_ref, o_ref, o2_ref, xn_ref):
    nt = (((1,), (1,)), ((), ()))

    def step(first):
        xn = _normed_rows(x_ref, g_ref, xn_ref, first)
        if first:
            o2_ref[...] = lax.dot_general(xn, wt2_ref[...], nt, preferred_element_type=F32)
        o_ref[...] = lax.dot_general(xn, wt_ref[...], nt, preferred_element_type=F32).astype(o_ref.dtype)

    _first_and_rest(step)


def _norm_matmul_t_kernel(x_ref, g_ref, w_ref, o_ref, xn_ref):
    def step(first):
        xn = _normed_rows(x_ref, g_ref, xn_ref, first)
        o_ref[...] = lax.dot_general(w_ref[...], xn, (((0,), (1,)), ((), ())),
                                     preferred_element_type=F32).astype(o_ref.dtype)

    _first_and_rest(step)


def norm_matmul_t(x, g, w, *, name):
    m, k = x.shape
    n = w.shape[1]
    tm, tn = min(ROW_TILE, m), min(COL_TILE, n)
    return pl.pallas_call(
        _norm_matmul_t_kernel, grid=(m // tm, n // tn),
        in_specs=[pl.BlockSpec((tm, k), lambda i, j: (i, 0)),
                  pl.BlockSpec((1, k), lambda i, j: (0, 0)),
                  pl.BlockSpec((k, tn), lambda i, j: (0, j))],
        out_specs=pl.BlockSpec((tn, tm), lambda i, j: (j, i)),
        out_shape=jax.ShapeDtypeStruct((n, m), BF16), scratch_shapes=[pltpu.VMEM((tm, k), BF16)],
        compiler_params=_params("parallel", "arbitrary"), name=name)(x, g, w)


def norm_matmul(x, g, w, *, name):
    m, k = x.shape
    n = w.shape[1]
    tm, tn = min(ROW_TILE, m), min(COL_TILE, n)
    return pl.pallas_call(
        _norm_matmul_kernel, grid=(m // tm, n // tn),
        in_specs=[pl.BlockSpec((tm, k), lambda i, j: (i, 0)),
                  pl.BlockSpec((1, k), lambda i, j: (0, 0)),
                  pl.BlockSpec((k, tn), lambda i, j: (0, j))],
        out_specs=pl.BlockSpec((tm, tn), lambda i, j: (i, j)),
        out_shape=jax.ShapeDtypeStruct((m, n), BF16), scratch_shapes=[pltpu.VMEM((tm, k), BF16)],
        compiler_params=_params("parallel", "arbitrary"), name=name)(x, g, w)


def norm_matmul2_nt(x, g, wt, wt2, *, n, name):
    m, k = x.shape
    n2 = wt2.shape[0]
    tm, tn = min(ROW_TILE, m), min(IN_PROJ_COL_TILE, n)
    assert n % tn == 0
    return pl.pallas_call(
        _norm_matmul2_kernel, grid=(m // tm, n // tn),
        in_specs=[pl.BlockSpec((tm, k), lambda i, j: (i, 0)),
                  pl.BlockSpec((1, k), lambda i, j: (0, 0)),
                  pl.BlockSpec((tn, k), lambda i, j: (j, 0)),
                  pl.BlockSpec((n2, k), lambda i, j: (0, 0))],
        out_specs=[pl.BlockSpec((tm, tn), lambda i, j: (i, j)), pl.BlockSpec((tm, n2), lambda i, j: (i, 0))],
        out_shape=[jax.ShapeDtypeStruct((m, n), BF16), jax.ShapeDtypeStruct((m, n2), F32)],
        scratch_shapes=[pltpu.VMEM((tm, k), BF16)],
        compiler_params=_params("parallel", "arbitrary"), name=name)(x, g, wt, wt2)


def _matmul_norm_residual_kernel(y_ref, w_ref, h_ref, g_ref, o_ref):
    def step(first, final):
        d = jnp.dot(y_ref[...], w_ref[...], preferred_element_type=F32)
        a = d if first else o_ref[...] + d
        o_ref[...] = h_ref[...] + a * _rms_scale(a) * g_ref[...] if final else a

    _first_middle_last(step)


def matmul_norm_residual(y, w, h, g, *, name):
    m, k = y.shape
    n = w.shape[1]
    tm, tk = min(ROW_TILE, m), min(K_TILE, k)
    assert k // tk >= 2
    return pl.pallas_call(
        _matmul_norm_residual_kernel, grid=(m // tm, k // tk),
        in_specs=[pl.BlockSpec((tm, tk), lambda i, kk: (i, kk)),
                  pl.BlockSpec((tk, n), lambda i, kk: (kk, 0)),
                  pl.BlockSpec((tm, n), lambda i, kk: (i, 0)),
                  pl.BlockSpec((1, n), lambda i, kk: (0, 0))],
        out_specs=pl.BlockSpec((tm, n), lambda i, kk: (i, 0)),
        out_shape=jax.ShapeDtypeStruct((m, n), F32),
        compiler_params=_params("parallel", "arbitrary"), name=name)(y, w, h, g)


def _ffn_kernel(h_ref, gpre_ref, wg_ref, wu_ref, wd_ref, gpost_ref, o_ref, xn_ref):
    def step(first, final):
        xn = _normed_rows(h_ref, gpre_ref, xn_ref, first)
        gate = jnp.dot(xn, wg_ref[...], preferred_element_type=F32)
        up = jnp.dot(xn, wu_ref[...], preferred_element_type=F32)
        act = (_silu(gate) * up).astype(BF16)
        d = jnp.dot(act, wd_ref[...], preferred_element_type=F32)
        a = d if first else o_ref[...] + d
        o_ref[...] = h_ref[...] + a * _rms_scale(a) * gpost_ref[...] if final else a

    _first_middle_last(step)


def ffn(h, g_pre, w_gate, w_up, w_down, g_post, layer, *, name):
    m, d = h.shape
    dff = w_gate.shape[2]
    tm, tf = min(ROW_TILE, m), FFN_CHUNK
    assert dff // tf >= 2
    return pl.pallas_call(
        _ffn_kernel, grid=(m // tm, dff // tf),
        in_specs=[pl.BlockSpec((tm, d), lambda i, f: (i, 0)),
                  pl.BlockSpec((1, d), lambda i, f: (0, 0)),
                  pl.BlockSpec((None, d, tf), lambda i, f: (layer, 0, f)),
                  pl.BlockSpec((None, d, tf), lambda i, f: (layer, 0, f)),
                  pl.BlockSpec((None, tf, d), lambda i, f: (layer, f, 0)),
                  pl.BlockSpec((1, d), lambda i, f: (0, 0))],
        out_specs=pl.BlockSpec((tm, d), lambda i, f: (i, 0)),
        out_shape=jax.ShapeDtypeStruct((m, d), F32),
        scratch_shapes=[pltpu.VMEM((tm, d), BF16)],
        compiler_params=_params("parallel", "arbitrary"), name=name)(h, g_pre, w_gate, w_up, w_down, g_post)


def _expand_heads(v, rows):
    lane = lax.broadcasted_iota(jnp.int32, (rows, LANES), 1)
    first = lane < SSM_HEAD_DIM
    pairs = [jnp.where(first, v[:, 2 * p:2 * p + 1], v[:, 2 * p + 1:2 * p + 2]) for p in range(HEADS_PER_GROUP // 2)]
    return jnp.concatenate(pairs, axis=1)


def _split_bf16(v):
    hi = v.astype(BF16)
    lo = (v - hi.astype(F32)).astype(BF16)
    return jnp.concatenate([hi, lo], axis=1)


def _ssd_kernel(*refs, n_side):
    (z_ref, xs_ref, b_ref, c_ref, dt_ref, cw_ref, cb_ref, dtb_ref, alog_ref, dskip_ref, nw_ref) = refs[:N_SSD_INPUTS]
    side_in = refs[N_SSD_INPUTS:N_SSD_INPUTS + n_side]
    y_ref = refs[N_SSD_INPUTS + n_side]
    side_out = refs[N_SSD_INPUTS + n_side + 1:N_SSD_INPUTS + 2 * n_side + 1]
    (state_ref, tail_ref, pad_ref, act_ref, ccol_ref, crowT_ref, e1_ref, wl_ref,
     elast_ref) = refs[N_SSD_INPUTS + 2 * n_side + 1:]
    c = pl.program_id(0)
    g = pl.program_id(1)

    for w32_ref, w16_ref in zip(side_in, side_out):
        w16_ref[...] = w32_ref[...].astype(BF16)

    rows = z_ref.shape[0]
    n_sub = rows // SSD_SUB
    sub_row = lax.broadcasted_iota(jnp.int32, (SSD_SUB, SSD_SUB), 0)
    sub_col = lax.broadcasted_iota(jnp.int32, (SSD_SUB, SSD_SUB), 1)
    causal = sub_row >= sub_col

    @pl.when(c == 0)
    def _():
        state_ref[g] = jnp.zeros((SSM_STATE, GROUP_INNER), F32)
        tail_ref[g] = jnp.zeros((SUBLANES, GROUP_CONV), F32)

    @pl.when(g == 0)
    def _():
        dtv = jax.nn.softplus(dt_ref[...] + dtb_ref[...])
        dta = dtv * (-jnp.exp(alog_ref[...]))
        log_dt = jnp.log(dtv)
        tri = causal.astype(F32)
        for s in range(n_sub):
            sl = slice(s * SSD_SUB, (s + 1) * SSD_SUB)
            cum = jnp.dot(tri, dta[sl], precision=lax.Precision.HIGHEST,
                          preferred_element_type=F32)
            clast = cum[SSD_SUB - 1:SSD_SUB, :]
            ccol_ref[sl, :] = cum * LOG2E
            crowT_ref[:, sl] = ((cum - log_dt[sl]) * LOG2E).T
            e1_ref[sl, :] = _split_bf16(jnp.exp(cum))
            wl_ref[sl, :] = _split_bf16(jnp.exp(clast - cum) * dtv[sl])
            elast_ref[s] = jnp.exp(clast)

    xcur = jnp.concatenate([xs_ref[...].astype(F32), b_ref[...].astype(F32), c_ref[...].astype(F32)], axis=1)
    cw = cw_ref[...]
    bias = cb_ref[...]
    pad_ref[0:SUBLANES, :] = tail_ref[g]
    pad_ref[SUBLANES:2 * SUBLANES, :] = xcur[0:SUBLANES]
    tail_ref[g] = xcur[rows - SUBLANES:rows]
    head = bias
    body = bias + cw[CONV_WIDTH - 1:CONV_WIDTH, :] * xcur
    for k in range(CONV_WIDTH):
        lo = SUBLANES - (CONV_WIDTH - 1) + k
        head = head + cw[k:k + 1, :] * pad_ref[lo:lo + SUBLANES, :]
        if k < CONV_WIDTH - 1:
            body = body + cw[k:k + 1, :] * pltpu.roll(xcur, CONV_WIDTH - 1 - k, 0)
    act_ref[...] = _silu(jnp.concatenate([head, body[SUBLANES:]], axis=0))

    shift = lax.rem(LANES - HEADS_PER_GROUP * g, LANES)
    r0 = pl.multiple_of(g * HEADS_PER_GROUP, HEADS_PER_GROUP)
    lane16 = sub_col.astype(F32).astype(BF16)
    low_half = lane16 < SSM_HEAD_DIM
    zero16 = jnp.zeros((SSD_SUB, LANES), BF16)
    sel_row = lax.broadcasted_iota(jnp.int32, (2 * LANES, GROUP_INNER), 0) & (LANES - 1)
    sel_head = lax.broadcasted_iota(jnp.int32, (2 * LANES, GROUP_INNER), 1) // SSM_HEAD_DIM
    expand = (sel_row == sel_head + HEADS_PER_GROUP * g).astype(F32).astype(BF16)

    for s in range(n_sub):
        rs = slice(s * SSD_SUB, (s + 1) * SSD_SUB)
        act = act_ref[rs, :]
        xg = act[:, 0:GROUP_INNER]
        xb = xg.astype(BF16)
        bb = act[:, GROUP_INNER:GROUP_INNER + SSM_STATE].astype(BF16)
        cc = act[:, GROUP_INNER + SSM_STATE:GROUP_CONV].astype(BF16)
        ccol = pltpu.roll(ccol_ref[rs, :], shift, 1)
        crow = crowT_ref[pl.ds(r0, HEADS_PER_GROUP), rs]
        cbm = lax.dot_general(cc, bb, (((1,), (1,)), ((), ())), preferred_element_type=F32).astype(BF16)

        ys = []
        for p in range(HEADS_PER_GROUP // 2):
            ms = []
            for r in (2 * p, 2 * p + 1):
                seg = ccol[:, r:r + 1] - crow[r:r + 1, :]
                ms.append(cbm * jnp.exp2(jnp.where(causal, seg, -jnp.inf)).astype(BF16))
            xp = xb[:, p * LANES:(p + 1) * LANES]
            bd = jnp.concatenate([jnp.where(low_half, xp, zero16), jnp.where(low_half, zero16, xp)], axis=0)
            ys.append(jnp.dot(jnp.concatenate(ms, axis=1), bd, preferred_element_type=F32))
        y = jnp.concatenate(ys, axis=1)

        st = state_ref[g]
        e1 = jnp.dot(e1_ref[rs, :], expand, preferred_element_type=F32)
        y = y + jnp.dot(cc, st.astype(BF16), preferred_element_type=F32) * e1
        wl = jnp.dot(wl_ref[rs, :], expand, preferred_element_type=F32)
        xw = (xg * wl).astype(BF16)
        ds = lax.dot_general(bb, xw, (((0,), (0,)), ((), ())), preferred_element_type=F32)
        el = pltpu.roll(elast_ref[s], shift, 1)
        state_ref[g] = st * _expand_heads(el, 1) + ds

        y = y + dskip_ref[...] * xg
        y = y * _silu(z_ref[rs, :].astype(F32))
        y_ref[rs, :] = (y * _rms_scale(y) * nw_ref[...]).astype(y_ref.dtype)


def _slab_spec(w, n_steps):
    layers, r, c = w.shape
    steps_per_slab = 1 if (layers * r // n_steps) % (2 * SUBLANES) == 0 else 2
    n_slabs = n_steps // steps_per_slab
    assert (layers * r) % n_slabs == 0
    slab_rows = layers * r // n_slabs
    assert r % slab_rows == 0 and slab_rows % (2 * SUBLANES) == 0
    per_layer = r // slab_rows

    def index(cb, g):
        slab = (cb * SSM_GROUPS + g) // steps_per_slab
        return slab // per_layer, slab % per_layer, 0

    return pl.BlockSpec((None, slab_rows, c), index)


def ssd_mixer(proj, dt_raw, conv_w_g, conv_b_g, dt_bias, a_log, d_skip, norm_w, side_casts):
    seq = proj.shape[0]
    rows = min(SSD_ROWS, seq)
    n_sub = rows // SSD_SUB
    xcol0 = D_INNER // GROUP_INNER
    bcol0 = 2 * D_INNER // SSM_STATE
    ccol0 = bcol0 + GN // SSM_STATE
    n_steps = (seq // rows) * SSM_GROUPS
    w_specs = [_slab_spec(w, n_steps) for w in side_casts]
    return pl.pallas_call(
        functools.partial(_ssd_kernel, n_side=len(side_casts)), grid=(seq // rows, SSM_GROUPS),
        in_specs=[pl.BlockSpec((rows, GROUP_INNER), lambda c, g: (c, g)),
                  pl.BlockSpec((rows, GROUP_INNER), lambda c, g: (c, xcol0 + g)),
                  pl.BlockSpec((rows, SSM_STATE), lambda c, g: (c, bcol0 + g)),
                  pl.BlockSpec((rows, SSM_STATE), lambda c, g: (c, ccol0 + g)),
                  pl.BlockSpec((rows, LANES), lambda c, g: (c, 0)),
                  pl.BlockSpec((None, CONV_WIDTH, GROUP_CONV), lambda c, g: (g, 0, 0)),
                  pl.BlockSpec((None, 1, GROUP_CONV), lambda c, g: (g, 0, 0)),
                  pl.BlockSpec((1, LANES), lambda c, g: (0, 0)),
                  pl.BlockSpec((1, LANES), lambda c, g: (0, 0)),
                  pl.BlockSpec((1, GROUP_INNER), lambda c, g: (0, g)),
                  pl.BlockSpec((1, GROUP_INNER), lambda c, g: (0, g))] + w_specs,
        out_specs=[pl.BlockSpec((rows, GROUP_INNER), lambda c, g: (c, g))] + w_specs,
        out_shape=[jax.ShapeDtypeStruct((seq, D_INNER), BF16)]
        + [jax.ShapeDtypeStruct(w.shape, BF16) for w in side_casts],
        scratch_shapes=[pltpu.VMEM((SSM_GROUPS, SSM_STATE, GROUP_INNER), F32),
                        pltpu.VMEM((SSM_GROUPS, SUBLANES, GROUP_CONV), F32),
                        pltpu.VMEM((2 * SUBLANES, GROUP_CONV), F32),
                        pltpu.VMEM((rows, GROUP_CONV), F32),
                        pltpu.VMEM((rows, LANES), F32),
                        pltpu.VMEM((LANES, rows), F32),
                        pltpu.VMEM((rows, 2 * LANES), BF16),
                        pltpu.VMEM((rows, 2 * LANES), BF16),
                        pltpu.VMEM((n_sub, 1, LANES), F32)],
        compiler_params=_params("arbitrary", "arbitrary"), name="ssd_mixer",
    )(proj, proj, proj, proj, dt_raw, conv_w_g, conv_b_g, dt_bias, a_log, d_skip, norm_w, *side_casts)


def _attn_kernel(sink_ref, qt_ref, kvp_ref, kvc_ref, o_ref, bias_ref):
    i = pl.program_id(0)
    group_lanes = Q_PER_KV * BLOCK

    @pl.when(i == 0)
    def _():
        sj = lax.broadcasted_iota(jnp.int32, (2 * BLOCK, group_lanes), 0)
        lane = lax.broadcasted_iota(jnp.int32, (2 * BLOCK, group_lanes), 1)
        dist_i = BLOCK + (lane & (BLOCK - 1)) - sj
        valid = (dist_i >= 0) & (dist_i < WINDOW)
        dist = dist_i.astype(F32)
        head_in_group = (lane // BLOCK).astype(F32)
        for kh in range(N_KV_HEADS):
            slope = jnp.exp2((head_in_group + (kh * Q_PER_KV + 1)) * (-8.0 / N_Q_HEADS))
            b = jnp.where(valid, -(slope * dist), -jnp.inf)
            bias_ref[1, kh] = b
            bias_ref[0, kh] = jnp.where(sj >= BLOCK, b, -jnp.inf)

    scale = HEAD_DIM ** -0.5
    kv_all = jnp.concatenate([kvp_ref[...], kvc_ref[...]], axis=0)
    for blk in range(ATTN_BLOCKS):
        has_prev = jnp.minimum(i, 1) if blk == 0 else 1
        kv = kv_all[blk * BLOCK:(blk + 2) * BLOCK]
        qs = slice(blk * BLOCK, (blk + 1) * BLOCK)
        for kh in range(N_KV_HEADS):
            k = kv[:, kh * HEAD_DIM:(kh + 1) * HEAD_DIM] * scale
            v = kv[:, KV_DIM + kh * HEAD_DIM:KV_DIM + (kh + 1) * HEAD_DIM]
            h0 = kh * Q_PER_KV
            qg = jnp.concatenate([qt_ref[(h0 + r) * HEAD_DIM:(h0 + r + 1) * HEAD_DIM, qs] for r in range(Q_PER_KV)],
                                 axis=1)
            sink = sink_ref[kh]
            logits = jnp.dot(k, qg, preferred_element_type=F32) + bias_ref[has_prev, kh]
            m = jnp.maximum(jnp.max(logits, axis=0, keepdims=True), sink)
            p = jnp.exp(logits - m)
            denom = jnp.sum(p, axis=0, keepdims=True) + jnp.exp(sink - m)
            ot = lax.dot_general(v, p.astype(BF16), (((0,), (0,)), ((), ())), preferred_element_type=F32) / denom
            for r in range(0, Q_PER_KV, 2):
                pair = jnp.concatenate([ot[:, r * BLOCK:(r + 1) * BLOCK], ot[:, (r + 1) * BLOCK:(r + 2) * BLOCK]], axis=0)
                o_ref[qs, (h0 + r) * HEAD_DIM:(h0 + r + 2) * HEAD_DIM] = pair.T.astype(o_ref.dtype)


def swa_attention(qt, kv, sinks):
    seq = kv.shape[0]
    step_rows = ATTN_BLOCKS * BLOCK
    nb = seq // step_rows
    sink_rows = jnp.repeat(sinks.astype(F32), BLOCK).reshape(N_KV_HEADS, 1, Q_PER_KV * BLOCK)
    return pl.pallas_call(
        _attn_kernel, grid=(nb,),
        in_specs=[pl.BlockSpec((N_KV_HEADS, 1, Q_PER_KV * BLOCK), lambda i: (0, 0, 0)),
                  pl.BlockSpec((N_Q_HEADS * HEAD_DIM, step_rows), lambda i: (0, i)),
                  pl.BlockSpec((BLOCK, 2 * KV_DIM), lambda i: (jnp.maximum(ATTN_BLOCKS * i - 1, 0), 0)),
                  pl.BlockSpec((step_rows, 2 * KV_DIM), lambda i: (i, 0))],
        out_specs=pl.BlockSpec((step_rows, N_Q_HEADS * HEAD_DIM), lambda i: (i, 0)),
        out_shape=jax.ShapeDtypeStruct((seq, N_Q_HEADS * HEAD_DIM), BF16),
        scratch_shapes=[pltpu.VMEM((2, N_KV_HEADS, 2 * BLOCK, Q_PER_KV * BLOCK), F32)],
        compiler_params=_params("arbitrary"), name="swa_attention",
    )(sink_rows, qt, kv, kv)


def _pad_lanes(v, width=LANES):
    v = v.reshape(1, -1).astype(F32)
    return jnp.pad(v, ((0, 0), (0, width - v.shape[1])))


def kernel(x, norm_w, ssm_w_in, ssm_conv_w, ssm_conv_b, ssm_dt_bias, ssm_A_log, ssm_D, ssm_norm_w, ssm_w_out,
           kv_norm_w, w_kv, attn_w_q, attn_sinks, attn_w_o, ffn_w_gate, ffn_w_up, ffn_w_down):
    batch, seq, d = x.shape
    assert batch == 1 and d == D_MODEL and seq % SSD_ROWS == 0
    h = x.reshape(seq, d)
    row = lambda v: v.reshape(1, -1).astype(F32)

    g = norm_w[0]
    w_in_t = ssm_w_in[0].T.astype(BF16)
    n_main = 2 * D_INNER + 2 * GN
    w_dt_t = jnp.pad(w_in_t[n_main:], ((0, LANES - SSM_HEADS), (0, 0)))
    proj, dt_raw = norm_matmul2_nt(h, row(g[0]), w_in_t, w_dt_t, n=n_main, name="in_proj")

    def per_group(t):
        xs = t[:, :D_INNER].reshape(-1, SSM_GROUPS, GROUP_INNER)
        bs = t[:, D_INNER:D_INNER + GN].reshape(-1, SSM_GROUPS, SSM_STATE)
        cs = t[:, D_INNER + GN:].reshape(-1, SSM_GROUPS, SSM_STATE)
        return jnp.transpose(jnp.concatenate([xs, bs, cs], axis=-1), (1, 0, 2)).astype(F32)

    later_weights = (ffn_w_gate, ffn_w_up, ffn_w_down, ssm_w_out, attn_w_q, attn_w_o, w_kv[None])
    y, w_gate, w_up, w_down, w_out, w_q, w_o, w_kv16 = ssd_mixer(
        proj, dt_raw, per_group(ssm_conv_w[0]), per_group(ssm_conv_b[0].reshape(1, -1)),
        _pad_lanes(ssm_dt_bias[0]), _pad_lanes(ssm_A_log[0]),
        row(jnp.repeat(ssm_D[0], SSM_HEAD_DIM)), row(ssm_norm_w[0]), later_weights)
    h = matmul_norm_residual(y, w_out[0], h, row(g[1]), name="ssm_out_proj")
    h = ffn(h, row(g[2]), w_gate, w_up, w_down, row(g[3]), 0, name="ffn0")

    g = norm_w[1]
    kv = norm_matmul(h, row(kv_norm_w), w_kv16[0], name="kv_proj")
    qt = norm_matmul_t(h, row(g[0]), w_q[0], name="q_proj")
    a = swa_attention(qt, kv, attn_sinks[0])
    h = matmul_norm_residual(a, w_o[0], h, row(g[1]), name="attn_out_proj")
    h = ffn(h, row(g[2]), w_gate, w_up, w_down, row(g[3]), 1, name="ffn1")
    return h.reshape(batch, seq, d)
```

```python
import functools
import math

import jax
import jax.numpy as jnp
from jax import lax
from jax.experimental import pallas as pl
from jax.experimental.pallas import tpu as pltpu

D_MODEL = 2048
D_INNER = 4096
SSM_HEAD_DIM = 64
SSM_HEADS = 64
SSM_GROUPS = 8
HEADS_PER_GROUP = SSM_HEADS // SSM_GROUPS
SSM_STATE = 128
CONV_WIDTH = 4
GN = SSM_GROUPS * SSM_STATE
GROUP_INNER = D_INNER // SSM_GROUPS
GROUP_CONV = GROUP_INNER + 2 * SSM_STATE
N_SSD_INPUTS = 11
SSD_SUB = 128
SSD_ROWS = 1024
N_Q_HEADS = 32
N_KV_HEADS = 4
Q_PER_KV = N_Q_HEADS // N_KV_HEADS
HEAD_DIM = 64
WINDOW = 128
BLOCK = 128
ATTN_BLOCKS = 4
KV_DIM = N_KV_HEADS * HEAD_DIM
EPS = 1e-6
LOG2E = math.log2(math.e)

LANES = 128
SUBLANES = 8
VMEM_LIMIT_BYTES = 63 * 1024 * 1024

ROW_TILE = 1024
COL_TILE = 1024
IN_PROJ_COL_TILE = 2048
OUT_PROJ_ROW_TILE = 512
FFN_CHUNK = 512

BF16 = jnp.bfloat16
F32 = jnp.float32


def _rms_scale(x):
    return lax.rsqrt(jnp.mean(x * x, axis=-1, keepdims=True) + EPS)


def _silu(x):
    return x * jax.nn.sigmoid(x)


def _params(*semantics):
    return pltpu.CompilerParams(dimension_semantics=semantics, vmem_limit_bytes=VMEM_LIMIT_BYTES)


def _normed_rows(x_ref, g_ref, xn_ref, first):
    if not first:
        return xn_ref[...]
    x = x_ref[...]
    xn = (x * _rms_scale(x) * g_ref[...]).astype(xn_ref.dtype)
    xn_ref[...] = xn
    return xn


def _first_and_rest(step):
    j = pl.program_id(1)
    pl.when(j == 0)(lambda: step(True))
    pl.when(j > 0)(lambda: step(False))


def _first_middle_last(step):
    j = pl.program_id(1)
    last = pl.num_programs(1) - 1
    pl.when(j == 0)(lambda: step(True, False))
    pl.when(jnp.logical_and(j > 0, j < last))(lambda: step(False, False))
    pl.when(j == last)(lambda: step(False, True))


def _norm_matmul_kernel(x_ref, g_ref, w_ref, o_ref, xn_ref):
    def step(first):
        xn = _normed_rows(x_ref, g_ref, xn_ref, first)
        o_ref[...] = jnp.dot(xn, w_ref[...], preferred_element_type=F32).astype(o_ref.dtype)

    _first_and_rest(step)


def _norm_matmul2_kernel(x_ref, g_ref, wt_ref, wt2_ref, o_ref, o2_ref, xn_ref):
    nt = (((1,), (1,)), ((), ()))

    def step(first):
        xn = _normed_rows(x_ref, g_ref, xn_ref, first)
        if first:
            o2_ref[...] = lax.dot_general(xn, wt2_ref[...], nt, preferred_element_type=F32)
        o_ref[...] = lax.dot_general(xn, wt_ref[...], nt, preferred_element_type=F32).astype(o_ref.dtype)

    _first_and_rest(step)


def _norm_matmul_t_kernel(x_ref, g_ref, w_ref, o_ref, xn_ref):
    def step(first):
        xn = _normed_rows(x_ref, g_ref, xn_ref, first)
        o_ref[...] = lax.dot_general(w_ref[...], xn, (((0,), (1,)), ((), ())),
                                     preferred_element_type=F32).astype(o_ref.dtype)

    _first_and_rest(step)


def norm_matmul_t(x, g, w, *, name):
    m, k = x.shape
    n = w.shape[1]
    tm, tn = min(ROW_TILE, m), min(COL_TILE, n)
    return pl.pallas_call(
        _norm_matmul_t_kernel, grid=(m // tm, n // tn),
        in_specs=[pl.BlockSpec((tm, k), lambda i, j: (i, 0)),
                  pl.BlockSpec((1, k), lambda i, j: (0, 0)),
                  pl.BlockSpec((k, tn), lambda i, j: (0, j))],
        out_specs=pl.BlockSpec((tn, tm), lambda i, j: (j, i)),
        out_shape=jax.ShapeDtypeStruct((n, m), BF16), scratch_shapes=[pltpu.VMEM((tm, k), BF16)],
        compiler_params=_params("parallel", "arbitrary"), name=name)(x, g, w)


def norm_matmul(x, g, w, *, name):
    m, k = x.shape
    n = w.shape[1]
    tm, tn = min(ROW_TILE, m), min(COL_TILE, n)
    return pl.pallas_call(
        _norm_matmul_kernel, grid=(m // tm, n // tn),
        in_specs=[pl.BlockSpec((tm, k), lambda i, j: (i, 0)),
                  pl.BlockSpec((1, k), lambda i, j: (0, 0)),
                  pl.BlockSpec((k, tn), lambda i, j: (0, j))],
        out_specs=pl.BlockSpec((tm, tn), lambda i, j: (i, j)),
        out_shape=jax.ShapeDtypeStruct((m, n), BF16), scratch_shapes=[pltpu.VMEM((tm, k), BF16)],
        compiler_params=_params("parallel", "arbitrary"), name=name)(x, g, w)


def norm_matmul2_nt(x, g, wt, wt2, *, n, name):
    m, k = x.shape
    n2 = wt2.shape[0]
    tm, tn = min(ROW_TILE, m), min(IN_PROJ_COL_TILE, n)
    assert n % tn == 0
    return pl.pallas_call(
        _norm_matmul2_kernel, grid=(m // tm, n // tn),
        in_specs=[pl.BlockSpec((tm, k), lambda i, j: (i, 0)),
                  pl.BlockSpec((1, k), lambda i, j: (0, 0)),
                  pl.BlockSpec((tn, k), lambda i, j: (j, 0)),
                  pl.BlockSpec((n2, k), lambda i, j: (0, 0))],
        out_specs=[pl.BlockSpec((tm, tn), lambda i, j: (i, j)), pl.BlockSpec((tm, n2), lambda i, j: (i, 0))],
        out_shape=[jax.ShapeDtypeStruct((m, n), BF16), jax.ShapeDtypeStruct((m, n2), F32)],
        scratch_shapes=[pltpu.VMEM((tm, k), BF16)],
        compiler_params=_params("parallel", "arbitrary"), name=name)(x, g, wt, wt2)


def _matmul_norm_residual_kernel(y_ref, w_ref, h_ref, g_ref, o_ref):
    a = jnp.dot(y_ref[...], w_ref[...], preferred_element_type=F32)
    o_ref[...] = h_ref[...] + a * _rms_scale(a) * g_ref[...]


def matmul_norm_residual(y, w, h, g, *, name):
    m, k = y.shape
    n = w.shape[1]
    tm = min(OUT_PROJ_ROW_TILE, m)
    return pl.pallas_call(
        _matmul_norm_residual_kernel, grid=(m // tm,),
        in_specs=[pl.BlockSpec((tm, k), lambda i: (i, 0)),
                  pl.BlockSpec((k, n), lambda i: (0, 0), pipeline_mode=pl.Buffered(1)),
                  pl.BlockSpec((tm, n), lambda i: (i, 0)),
                  pl.BlockSpec((1, n), lambda i: (0, 0))],
        out_specs=pl.BlockSpec((tm, n), lambda i: (i, 0)),
        out_shape=jax.ShapeDtypeStruct((m, n), F32),
        compiler_params=_params("parallel"), name=name)(y, w, h, g)


def _ffn_kernel(h_ref, gpre_ref, wg_ref, wu_ref, wd_ref, gpost_ref, o_ref, xn_ref):
    def step(first, final):
        xn = _normed_rows(h_ref, gpre_ref, xn_ref, first)
        gate = jnp.dot(xn, wg_ref[...], preferred_element_type=F32)
        up = jnp.dot(xn, wu_ref[...], preferred_element_type=F32)
        act = (_silu(gate) * up).astype(BF16)
        d = jnp.dot(act, wd_ref[...], preferred_element_type=F32)
        a = d if first else o_ref[...] + d
        o_ref[...] = h_ref[...] + a * _rms_scale(a) * gpost_ref[...] if final else a

    _first_middle_last(step)


def ffn(h, g_pre, w_gate, w_up, w_down, g_post, layer, *, name):
    m, d = h.shape
    dff = w_gate.shape[2]
    tm, tf = min(ROW_TILE, m), FFN_CHUNK
    assert dff // tf >= 2
    return pl.pallas_call(
        _ffn_kernel, grid=(m // tm, dff // tf),
        in_specs=[pl.BlockSpec((tm, d), lambda i, f: (i, 0)),
                  pl.BlockSpec((1, d), lambda i, f: (0, 0)),
                  pl.BlockSpec((None, d, tf), lambda i, f: (layer, 0, f)),
                  pl.BlockSpec((None, d, tf), lambda i, f: (layer, 0, f)),
                  pl.BlockSpec((None, tf, d), lambda i, f: (layer, f, 0)),
                  pl.BlockSpec((1, d), lambda i, f: (0, 0))],
        out_specs=pl.BlockSpec((tm, d), lambda i, f: (i, 0)),
        out_shape=jax.ShapeDtypeStruct((m, d), F32),
        scratch_shapes=[pltpu.VMEM((tm, d), BF16)],
        compiler_params=_params("parallel", "arbitrary"), name=name)(h, g_pre, w_gate, w_up, w_down, g_post)


def _expand_heads(v, rows):
    lane = lax.broadcasted_iota(jnp.int32, (rows, LANES), 1)
    first = lane < SSM_HEAD_DIM
    pairs = [jnp.where(first, v[:, 2 * p:2 * p + 1], v[:, 2 * p + 1:2 * p + 2]) for p in range(HEADS_PER_GROUP // 2)]
    return jnp.concatenate(pairs, axis=1)


def _split_bf16(v):
    hi = v.astype(BF16)
    lo = (v - hi.astype(F32)).astype(BF16)
    return jnp.concatenate([hi, lo], axis=1)


def _ssd_kernel(*refs, n_side):
    (z_ref, xs_ref, b_ref, c_ref, dt_ref, cw_ref, cb_ref, dtb_ref, alog_ref, dskip_ref, nw_ref) = refs[:N_SSD_INPUTS]
    side_in = refs[N_SSD_INPUTS:N_SSD_INPUTS + n_side]
    y_ref = refs[N_SSD_INPUTS + n_side]
    side_out = refs[N_SSD_INPUTS + n_side + 1:N_SSD_INPUTS + 2 * n_side + 1]
    (state_ref, tail_ref, pad_ref, act_ref, ccol_ref, crowT_ref, e1_ref, wl_ref,
     elast_ref) = refs[N_SSD_INPUTS + 2 * n_side + 1:]
    c = pl.program_id(0)
    g = pl.program_id(1)

    for w32_ref, w16_ref in zip(side_in, side_out):
        w16_ref[...] = w32_ref[...].astype(BF16)

    rows = z_ref.shape[0]
    n_sub = rows // SSD_SUB
    sub_row = lax.broadcasted_iota(jnp.int32, (SSD_SUB, SSD_SUB), 0)
    sub_col = lax.broadcasted_iota(jnp.int32, (SSD_SUB, SSD_SUB), 1)
    causal = sub_row >= sub_col

    @pl.when(c == 0)
    def _():
        state_ref[g] = jnp.zeros((SSM_STATE, GROUP_INNER), F32)
        tail_ref[g] = jnp.zeros((SUBLANES, GROUP_CONV), F32)

    @pl.when(g == 0)
    def _():
        dtv = jax.nn.softplus(dt_ref[...] + dtb_ref[...])
        dta = dtv * (-jnp.exp(alog_ref[...]))
        log_dt = jnp.log(dtv)
        tri = causal.astype(F32)
        for s in range(n_sub):
            sl = slice(s * SSD_SUB, (s + 1) * SSD_SUB)
            cum = jnp.dot(tri, dta[sl], precision=lax.Precision.HIGHEST,
                          preferred_element_type=F32)
            clast = cum[SSD_SUB - 1:SSD_SUB, :]
            ccol_ref[sl, :] = cum * LOG2E
            crowT_ref[:, sl] = ((cum - log_dt[sl]) * LOG2E).T
            e1_ref[sl, :] = _split_bf16(jnp.exp(cum))
            wl_ref[sl, :] = _split_bf16(jnp.exp(clast - cum) * dtv[sl])
            elast_ref[s] = jnp.exp(clast)

    xcur = jnp.concatenate([xs_ref[...].astype(F32), b_ref[...].astype(F32), c_ref[...].astype(F32)], axis=1)
    cw = cw_ref[...]
    bias = cb_ref[...]
    pad_ref[0:SUBLANES, :] = tail_ref[g]
    pad_ref[SUBLANES:2 * SUBLANES, :] = xcur[0:SUBLANES]
    tail_ref[g] = xcur[rows - SUBLANES:rows]
    head = bias
    body = bias + cw[CONV_WIDTH - 1:CONV_WIDTH, :] * xcur
    for k in range(CONV_WIDTH):
        lo = SUBLANES - (CONV_WIDTH - 1) + k
        head = head + cw[k:k + 1, :] * pad_ref[lo:lo + SUBLANES, :]
        if k < CONV_WIDTH - 1:
            body = body + cw[k:k + 1, :] * pltpu.roll(xcur, CONV_WIDTH - 1 - k, 0)
    act_ref[...] = _silu(jnp.concatenate([head, body[SUBLANES:]], axis=0))

    shift = lax.rem(LANES - HEADS_PER_GROUP * g, LANES)
    r0 = pl.multiple_of(g * HEADS_PER_GROUP, HEADS_PER_GROUP)
    lane16 = sub_col.astype(F32).astype(BF16)
    low_half = lane16 < SSM_HEAD_DIM
    zero16 = jnp.zeros((SSD_SUB, LANES), BF16)
    sel_row = lax.broadcasted_iota(jnp.int32, (2 * LANES, GROUP_INNER), 0) & (LANES - 1)
    sel_head = lax.broadcasted_iota(jnp.int32, (2 * LANES, GROUP_INNER), 1) // SSM_HEAD_DIM
    expand = (sel_row == sel_head + HEADS_PER_GROUP * g).astype(F32).astype(BF16)

    for s in range(n_sub):
        rs = slice(s * SSD_SUB, (s + 1) * SSD_SUB)
        act = act_ref[rs, :]
        xg = act[:, 0:GROUP_INNER]
        xb = xg.astype(BF16)
        bb = act[:, GROUP_INNER:GROUP_INNER + SSM_STATE].astype(BF16)
        cc = act[:, GROUP_INNER + SSM_STATE:GROUP_CONV].astype(BF16)
        ccol = pltpu.roll(ccol_ref[rs, :], shift, 1)
        crow = crowT_ref[pl.ds(r0, HEADS_PER_GROUP), rs]
        cbm = lax.dot_general(cc, bb, (((1,), (1,)), ((), ())), preferred_element_type=F32).astype(BF16)

        ys = []
        for p in range(HEADS_PER_GROUP // 2):
            ms = []
            for r in (2 * p, 2 * p + 1):
                seg = ccol[:, r:r + 1] - crow[r:r + 1, :]
                ms.append(cbm * jnp.exp2(jnp.where(causal, seg, -jnp.inf)).astype(BF16))
            xp = xb[:, p * LANES:(p + 1) * LANES]
            bd = jnp.concatenate([jnp.where(low_half, xp, zero16), jnp.where(low_half, zero16, xp)], axis=0)
            ys.append(jnp.dot(jnp.concatenate(ms, axis=1), bd, preferred_element_type=F32))
        y = jnp.concatenate(ys, axis=1)

        st = state_ref[g]
        e1 = jnp.dot(e1_ref[rs, :], expand, preferred_element_type=F32)
        y = y + jnp.dot(cc, st.astype(BF16), preferred_element_type=F32) * e1
        wl = jnp.dot(wl_ref[rs, :], expand, preferred_element_type=F32)
        xw = (xg * wl).astype(BF16)
        ds = lax.dot_general(bb, xw, (((0,), (0,)), ((), ())), preferred_element_type=F32)
        el = pltpu.roll(elast_ref[s], shift, 1)
        state_ref[g] = st * _expand_heads(el, 1) + ds

        y = y + dskip_ref[...] * xg
        y = y * _silu(z_ref[rs, :].astype(F32))
        y_ref[rs, :] = (y * _rms_scale(y) * nw_ref[...]).astype(y_ref.dtype)


def _slab_spec(w, n_steps):
    layers, r, c = w.shape
    steps_per_slab = 1 if (layers * r // n_steps) % (2 * SUBLANES) == 0 else 2
    n_slabs = n_steps // steps_per_slab
    assert (layers * r) % n_slabs == 0
    slab_rows = layers * r // n_slabs
    assert r % slab_rows == 0 and slab_rows % (2 * SUBLANES) == 0
    per_layer = r // slab_rows

    def index(cb, g):
        slab = (cb * SSM_GROUPS + g) // steps_per_slab
        return slab // per_layer, slab % per_layer, 0

    return pl.BlockSpec((None, slab_rows, c), index)


def ssd_mixer(proj, dt_raw, conv_w_g, conv_b_g, dt_bias, a_log, d_skip, norm_w, side_casts):
    seq = proj.shape[0]
    rows = min(SSD_ROWS, seq)
    n_sub = rows // SSD_SUB
    xcol0 = D_INNER // GROUP_INNER
    bcol0 = 2 * D_INNER // SSM_STATE
    ccol0 = bcol0 + GN // SSM_STATE
    n_steps = (seq // rows) * SSM_GROUPS
    w_specs = [_slab_spec(w, n_steps) for w in side_casts]
    return pl.pallas_call(
        functools.partial(_ssd_kernel, n_side=len(side_casts)), grid=(seq // rows, SSM_GROUPS),
        in_specs=[pl.BlockSpec((rows, GROUP_INNER), lambda c, g: (c, g)),
                  pl.BlockSpec((rows, GROUP_INNER), lambda c, g: (c, xcol0 + g)),
                  pl.BlockSpec((rows, SSM_STATE), lambda c, g: (c, bcol0 + g)),
                  pl.BlockSpec((rows, SSM_STATE), lambda c, g: (c, ccol0 + g)),
                  pl.BlockSpec((rows, LANES), lambda c, g: (c, 0)),
                  pl.BlockSpec((None, CONV_WIDTH, GROUP_CONV), lambda c, g: (g, 0, 0)),
                  pl.BlockSpec((None, 1, GROUP_CONV), lambda c, g: (g, 0, 0)),
                  pl.BlockSpec((1, LANES), lambda c, g: (0, 0)),
                  pl.BlockSpec((1, LANES), lambda c, g: (0, 0)),
                  pl.BlockSpec((1, GROUP_INNER), lambda c, g: (0, g)),
                  pl.BlockSpec((1, GROUP_INNER), lambda c, g: (0, g))] + w_specs,
        out_specs=[pl.BlockSpec((rows, GROUP_INNER), lambda c, g: (c, g))] + w_specs,
        out_shape=[jax.ShapeDtypeStruct((seq, D_INNER), BF16)]
        + [jax.ShapeDtypeStruct(w.shape, BF16) for w in side_casts],
        scratch_shapes=[pltpu.VMEM((SSM_GROUPS, SSM_STATE, GROUP_INNER), F32),
                        pltpu.VMEM((SSM_GROUPS, SUBLANES, GROUP_CONV), F32),
                        pltpu.VMEM((2 * SUBLANES, GROUP_CONV), F32),
                        pltpu.VMEM((rows, GROUP_CONV), F32),
                        pltpu.VMEM((rows, LANES), F32),
                        pltpu.VMEM((LANES, rows), F32),
                        pltpu.VMEM((rows, 2 * LANES), BF16),
                        pltpu.VMEM((rows, 2 * LANES), BF16),
                        pltpu.VMEM((n_sub, 1, LANES), F32)],
        compiler_params=_params("arbitrary", "arbitrary"), name="ssd_mixer",
    )(proj, proj, proj, proj, dt_raw, conv_w_g, conv_b_g, dt_bias, a_log, d_skip, norm_w, *side_casts)


def _attn_kernel(sink_ref, qt_ref, kvp_ref, kvc_ref, o_ref, bias_ref):
    i = pl.program_id(0)
    group_lanes = Q_PER_KV * BLOCK

    @pl.when(i == 0)
    def _():
        sj = lax.broadcasted_iota(jnp.int32, (2 * BLOCK, group_lanes), 0)
        lane = lax.broadcasted_iota(jnp.int32, (2 * BLOCK, group_lanes), 1)
        dist_i = BLOCK + (lane & (BLOCK - 1)) - sj
        valid = (dist_i >= 0) & (dist_i < WINDOW)
        dist = dist_i.astype(F32)
        head_in_group = (lane // BLOCK).astype(F32)
        for kh in range(N_KV_HEADS):
            slope = jnp.exp2((head_in_group + (kh * Q_PER_KV + 1)) * (-8.0 / N_Q_HEADS))
            b = jnp.where(valid, -(slope * dist), -jnp.inf)
            bias_ref[1, kh] = b
            bias_ref[0, kh] = jnp.where(sj >= BLOCK, b, -jnp.inf)

    scale = HEAD_DIM ** -0.5
    kv_all = jnp.concatenate([kvp_ref[...], kvc_ref[...]], axis=0)
    for blk in range(ATTN_BLOCKS):
        has_prev = jnp.minimum(i, 1) if blk == 0 else 1
        kv = kv_all[blk * BLOCK:(blk + 2) * BLOCK]
        qs = slice(blk * BLOCK, (blk + 1) * BLOCK)
        for kh in range(N_KV_HEADS):
            k = kv[:, kh * HEAD_DIM:(kh + 1) * HEAD_DIM] * scale
            v = kv[:, KV_DIM + kh * HEAD_DIM:KV_DIM + (kh + 1) * HEAD_DIM]
            h0 = kh * Q_PER_KV
            qg = jnp.concatenate([qt_ref[(h0 + r) * HEAD_DIM:(h0 + r + 1) * HEAD_DIM, qs] for r in range(Q_PER_KV)],
                                 axis=1)
            sink = sink_ref[kh]
            logits = jnp.dot(k, qg, preferred_element_type=F32) + bias_ref[has_prev, kh]
            m = jnp.maximum(jnp.max(logits, axis=0, keepdims=True), sink)
            p = jnp.exp(logits - m)
            denom = jnp.sum(p, axis=0, keepdims=True) + jnp.exp(sink - m)
            ot = lax.dot_general(v, p.astype(BF16), (((0,), (0,)), ((), ())), preferred_element_type=F32) / denom
            for r in range(0, Q_PER_KV, 2):
                pair = jnp.concatenate([ot[:, r * BLOCK:(r + 1) * BLOCK], ot[:, (r + 1) * BLOCK:(r + 2) * BLOCK]], axis=0)
                o_ref[qs, (h0 + r) * HEAD_DIM:(h0 + r + 2) * HEAD_DIM] = pair.T.astype(o_ref.dtype)


def swa_attention(qt, kv, sinks):
    seq = kv.shape[0]
    step_rows = ATTN_BLOCKS * BLOCK
    nb = seq // step_rows
    sink_rows = jnp.repeat(sinks.astype(F32), BLOCK).reshape(N_KV_HEADS, 1, Q_PER_KV * BLOCK)
    return pl.pallas_call(
        _attn_kernel, grid=(nb,),
        in_specs=[pl.BlockSpec((N_KV_HEADS, 1, Q_PER_KV * BLOCK), lambda i: (0, 0, 0)),
                  pl.BlockSpec((N_Q_HEADS * HEAD_DIM, step_rows), lambda i: (0, i)),
                  pl.BlockSpec((BLOCK, 2 * KV_DIM), lambda i: (jnp.maximum(ATTN_BLOCKS * i - 1, 0), 0)),
                  pl.BlockSpec((step_rows, 2 * KV_DIM), lambda i: (i, 0))],
        out_specs=pl.BlockSpec((step_rows, N_Q_HEADS * HEAD_DIM), lambda i: (i, 0)),
        out_shape=jax.ShapeDtypeStruct((seq, N_Q_HEADS * HEAD_DIM), BF16),
        scratch_shapes=[pltpu.VMEM((2, N_KV_HEADS, 2 * BLOCK, Q_PER_KV * BLOCK), F32)],
        compiler_params=_params("arbitrary"), name="swa_attention",
    )(sink_rows, qt, kv, kv)


def _pad_lanes(v, width=LANES):
    v = v.reshape(1, -1).astype(F32)
    return jnp.pad(v, ((0, 0), (0, width - v.shape[1])))


def kernel(x, norm_w, ssm_w_in, ssm_conv_w, ssm_conv_b, ssm_dt_bias, ssm_A_log, ssm_D, ssm_norm_w, ssm_w_out,
           kv_norm_w, w_kv, attn_w_q, attn_sinks, attn_w_o, ffn_w_gate, ffn_w_up, ffn_w_down):
    batch, seq, d = x.shape
    assert batch == 1 and d == D_MODEL and seq % SSD_ROWS == 0
    h = x.reshape(seq, d)
    row = lambda v: v.reshape(1, -1).astype(F32)

    g = norm_w[0]
    w_in_t = ssm_w_in[0].T.astype(BF16)
    n_main = 2 * D_INNER + 2 * GN
    w_dt_t = jnp.pad(w_in_t[n_main:], ((0, LANES - SSM_HEADS), (0, 0)))
    proj, dt_raw = norm_matmul2_nt(h, row(g[0]), w_in_t, w_dt_t, n=n_main, name="in_proj")

    def per_group(t):
        xs = t[:, :D_INNER].reshape(-1, SSM_GROUPS, GROUP_INNER)
        bs = t[:, D_INNER:D_INNER + GN].reshape(-1, SSM_GROUPS, SSM_STATE)
        cs = t[:, D_INNER + GN:].reshape(-1, SSM_GROUPS, SSM_STATE)
        return jnp.transpose(jnp.concatenate([xs, bs, cs], axis=-1), (1, 0, 2)).astype(F32)

    later_weights = (ffn_w_gate, ffn_w_up, ffn_w_down, ssm_w_out, attn_w_q, attn_w_o, w_kv[None])
    y, w_gate, w_up, w_down, w_out, w_q, w_o, w_kv16 = ssd_mixer(
        proj, dt_raw, per_group(ssm_conv_w[0]), per_group(ssm_conv_b[0].reshape(1, -1)),
        _pad_lanes(ssm_dt_bias[0]), _pad_lanes(ssm_A_log[0]),
        row(jnp.repeat(ssm_D[0], SSM_HEAD_DIM)), row(ssm_norm_w[0]), later_weights)
    h = matmul_norm_residual(y, w_out[0], h, row(g[1]), name="ssm_out_proj")
    h = ffn(h, row(g[2]), w_gate, w_up, w_down, row(g[3]), 0, name="ffn0")

    g = norm_w[1]
    kv = norm_matmul(h, row(kv_norm_w), w_kv16[0], name="kv_proj")
    qt = norm_matmul_t(h, row(g[0]), w_q[0], name="q_proj")
    a = swa_attention(qt, kv, attn_sinks[0])
    h = matmul_norm_residual(a, w_o[0], h, row(g[1]), name="attn_out_proj")
    h = ffn(h, row(g[2]), w_gate, w_up, w_down, row(g[3]), 1, name="ffn1")
    return h.reshape(batch, seq, d)
```

```python
import functools
import math

import jax
import jax.numpy as jnp
from jax import lax
from jax.experimental import pallas as pl
from jax.experimental.pallas import tpu as pltpu

D_MODEL = 2048
D_INNER = 4096
SSM_HEAD_DIM = 64
SSM_HEADS = 64
SSM_GROUPS = 8
HEADS_PER_GROUP = SSM_HEADS // SSM_GROUPS
SSM_STATE = 128
CONV_WIDTH = 4
GN = SSM_GROUPS * SSM_STATE
GROUP_INNER = D_INNER // SSM_GROUPS
GROUP_CONV = GROUP_INNER + 2 * SSM_STATE
N_SSD_INPUTS = 11
SSD_SUB = 128
SSD_ROWS = 1024
N_Q_HEADS = 32
N_KV_HEADS = 4
Q_PER_KV = N_Q_HEADS // N_KV_HEADS
HEAD_DIM = 64
WINDOW = 128
BLOCK = 128
ATTN_BLOCKS = 4
KV_DIM = N_KV_HEADS * HEAD_DIM
EPS = 1e-6
LOG2E = math.log2(math.e)

LANES = 128
SUBLANES = 8
VMEM_LIMIT_BYTES = 63 * 1024 * 1024

ROW_TILE = 1024
IN_PROJ_COL_TILE = 2048
OUT_PROJ_ROW_TILE = 512
FFN_CHUNK = 512

BF16 = jnp.bfloat16
F32 = jnp.float32


def _rms_scale(x):
    return lax.rsqrt(jnp.mean(x * x, axis=-1, keepdims=True) + EPS)


def _silu(x):
    return x * jax.nn.sigmoid(x)


def _params(*semantics):
    return pltpu.CompilerParams(dimension_semantics=semantics, vmem_limit_bytes=VMEM_LIMIT_BYTES)


def _normed_rows(x_ref, g_ref, xn_ref, first):
    if not first:
        return xn_ref[...]
    x = x_ref[...]
    xn = (x * _rms_scale(x) * g_ref[...]).astype(xn_ref.dtype)
    xn_ref[...] = xn
    return xn


def _first_and_rest(step):
    j = pl.program_id(1)
    pl.when(j == 0)(lambda: step(True))
    pl.when(j > 0)(lambda: step(False))


def _first_middle_last(step):
    j = pl.program_id(1)
    last = pl.num_programs(1) - 1
    pl.when(j == 0)(lambda: step(True, False))
    pl.when(jnp.logical_and(j > 0, j < last))(lambda: step(False, False))
    pl.when(j == last)(lambda: step(False, True))


def _norm_matmul2_kernel(x_ref, g_ref, wt_ref, wt2_ref, o_ref, o2_ref, xn_ref):
    nt = (((1,), (1,)), ((), ()))

    def step(first):
        xn = _normed_rows(x_ref, g_ref, xn_ref, first)
        if first:
            o2_ref[...] = lax.dot_general(xn, wt2_ref[...], nt, preferred_element_type=F32)
        o_ref[...] = lax.dot_general(xn, wt_ref[...], nt, preferred_element_type=F32).astype(o_ref.dtype)

    _first_and_rest(step)


def _qkv_kernel(x_ref, gq_ref, gkv_ref, wq_ref, wkv_ref, qt_ref, kv_ref):
    x = x_ref[...]
    xs = x * _rms_scale(x)
    xq = (xs * gq_ref[...]).astype(BF16)
    xkv = (xs * gkv_ref[...]).astype(BF16)
    qt_ref[...] = lax.dot_general(wq_ref[...], xq, (((0,), (1,)), ((), ())),
                                  preferred_element_type=F32).astype(qt_ref.dtype)
    kv_ref[...] = jnp.dot(xkv, wkv_ref[...], preferred_element_type=F32).astype(kv_ref.dtype)


def qkv_proj(x, g_q, g_kv, w_q, w_kv, *, name):
    m, k = x.shape
    n_q, n_kv = w_q.shape[1], w_kv.shape[1]
    tm = min(ROW_TILE, m)
    resident = pl.Buffered(1)
    return pl.pallas_call(
        _qkv_kernel, grid=(m // tm,),
        in_specs=[pl.BlockSpec((tm, k), lambda i: (i, 0)),
                  pl.BlockSpec((1, k), lambda i: (0, 0)),
                  pl.BlockSpec((1, k), lambda i: (0, 0)),
                  pl.BlockSpec((k, n_q), lambda i: (0, 0), pipeline_mode=resident),
                  pl.BlockSpec((k, n_kv), lambda i: (0, 0), pipeline_mode=resident)],
        out_specs=[pl.BlockSpec((n_q, tm), lambda i: (0, i)), pl.BlockSpec((tm, n_kv), lambda i: (i, 0))],
        out_shape=[jax.ShapeDtypeStruct((n_q, m), BF16), jax.ShapeDtypeStruct((m, n_kv), BF16)],
        compiler_params=_params("parallel"), name=name)(x, g_q, g_kv, w_q, w_kv)


def norm_matmul2_nt(x, g, wt, wt2, *, n, name):
    m, k = x.shape
    n2 = wt2.shape[0]
    tm, tn = min(ROW_TILE, m), min(IN_PROJ_COL_TILE, n)
    assert n % tn == 0
    return pl.pallas_call(
        _norm_matmul2_kernel, grid=(m // tm, n // tn),
        in_specs=[pl.BlockSpec((tm, k), lambda i, j: (i, 0)),
                  pl.BlockSpec((1, k), lambda i, j: (0, 0)),
                  pl.BlockSpec((tn, k), lambda i, j: (j, 0)),
                  pl.BlockSpec((n2, k), lambda i, j: (0, 0))],
        out_specs=[pl.BlockSpec((tm, tn), lambda i, j: (i, j)), pl.BlockSpec((tm, n2), lambda i, j: (i, 0))],
        out_shape=[jax.ShapeDtypeStruct((m, n), BF16), jax.ShapeDtypeStruct((m, n2), F32)],
        scratch_shapes=[pltpu.VMEM((tm, k), BF16)],
        compiler_params=_params("parallel", "arbitrary"), name=name)(x, g, wt, wt2)


def _matmul_norm_residual_kernel(y_ref, w_ref, h_ref, g_ref, o_ref):
    a = jnp.dot(y_ref[...], w_ref[...], preferred_element_type=F32)
    o_ref[...] = h_ref[...] + a * _rms_scale(a) * g_ref[...]


def matmul_norm_residual(y, w, h, g, *, name):
    m, k = y.shape
    n = w.shape[1]
    tm = min(OUT_PROJ_ROW_TILE, m)
    return pl.pallas_call(
        _matmul_norm_residual_kernel, grid=(m // tm,),
        in_specs=[pl.BlockSpec((tm, k), lambda i: (i, 0)),
                  pl.BlockSpec((k, n), lambda i: (0, 0), pipeline_mode=pl.Buffered(1)),
                  pl.BlockSpec((tm, n), lambda i: (i, 0)),
                  pl.BlockSpec((1, n), lambda i: (0, 0))],
        out_specs=pl.BlockSpec((tm, n), lambda i: (i, 0)),
        out_shape=jax.ShapeDtypeStruct((m, n), F32),
        compiler_params=_params("parallel"), name=name)(y, w, h, g)


def _ffn_kernel(h_ref, gpre_ref, wg_ref, wu_ref, wd_ref, gpost_ref, o_ref, xn_ref):
    def step(first, final):
        xn = _normed_rows(h_ref, gpre_ref, xn_ref, first)
        gate = jnp.dot(xn, wg_ref[...], preferred_element_type=F32)
        up = jnp.dot(xn, wu_ref[...], preferred_element_type=F32)
        act = (_silu(gate) * up).astype(BF16)
        d = jnp.dot(act, wd_ref[...], preferred_element_type=F32)
        a = d if first else o_ref[...] + d
        o_ref[...] = h_ref[...] + a * _rms_scale(a) * gpost_ref[...] if final else a

    _first_middle_last(step)


def ffn(h, g_pre, w_gate, w_up, w_down, g_post, layer, *, name):
    m, d = h.shape
    dff = w_gate.shape[2]
    tm, tf = min(ROW_TILE, m), FFN_CHUNK
    assert dff // tf >= 2
    return pl.pallas_call(
        _ffn_kernel, grid=(m // tm, dff // tf),
        in_specs=[pl.BlockSpec((tm, d), lambda i, f: (i, 0)),
                  pl.BlockSpec((1, d), lambda i, f: (0, 0)),
                  pl.BlockSpec((None, d, tf), lambda i, f: (layer, 0, f)),
                  pl.BlockSpec((None, d, tf), lambda i, f: (layer, 0, f)),
                  pl.BlockSpec((None, tf, d), lambda i, f: (layer, f, 0)),
                  pl.BlockSpec((1, d), lambda i, f: (0, 0))],
        out_specs=pl.BlockSpec((tm, d), lambda i, f: (i, 0)),
        out_shape=jax.ShapeDtypeStruct((m, d), F32),
        scratch_shapes=[pltpu.VMEM((tm, d), BF16)],
        compiler_params=_params("parallel", "arbitrary"), name=name)(h, g_pre, w_gate, w_up, w_down, g_post)


def _expand_heads(v, rows):
    lane = lax.broadcasted_iota(jnp.int32, (rows, LANES), 1)
    first = lane < SSM_HEAD_DIM
    pairs = [jnp.where(first, v[:, 2 * p:2 * p + 1], v[:, 2 * p + 1:2 * p + 2]) for p in range(HEADS_PER_GROUP // 2)]
    return jnp.concatenate(pairs, axis=1)


def _split_bf16(v):
    hi = v.astype(BF16)
    lo = (v - hi.astype(F32)).astype(BF16)
    return jnp.concatenate([hi, lo], axis=1)


def _ssd_kernel(*refs, n_side):
    (z_ref, xs_ref, b_ref, c_ref, dt_ref, cw_ref, cb_ref, dtb_ref, alog_ref, dskip_ref, nw_ref) = refs[:N_SSD_INPUTS]
    side_in = refs[N_SSD_INPUTS:N_SSD_INPUTS + n_side]
    y_ref = refs[N_SSD_INPUTS + n_side]
    side_out = refs[N_SSD_INPUTS + n_side + 1:N_SSD_INPUTS + 2 * n_side + 1]
    (state_ref, tail_ref, pad_ref, act_ref, ccol_ref, crowT_ref, e1_ref, wl_ref,
     elast_ref) = refs[N_SSD_INPUTS + 2 * n_side + 1:]
    c = pl.program_id(0)
    g = pl.program_id(1)

    for w32_ref, w16_ref in zip(side_in, side_out):
        w16_ref[...] = w32_ref[...].astype(BF16)

    rows = z_ref.shape[0]
    n_sub = rows // SSD_SUB
    sub_row = lax.broadcasted_iota(jnp.int32, (SSD_SUB, SSD_SUB), 0)
    sub_col = lax.broadcasted_iota(jnp.int32, (SSD_SUB, SSD_SUB), 1)
    causal = sub_row >= sub_col

    @pl.when(c == 0)
    def _():
        state_ref[g] = jnp.zeros((SSM_STATE, GROUP_INNER), F32)
        tail_ref[g] = jnp.zeros((SUBLANES, GROUP_CONV), F32)

    @pl.when(g == 0)
    def _():
        dtv = jax.nn.softplus(dt_ref[...] + dtb_ref[...])
        dta = dtv * (-jnp.exp(alog_ref[...]))
        log_dt = jnp.log(dtv)
        tri = causal.astype(F32)
        for s in range(n_sub):
            sl = slice(s * SSD_SUB, (s + 1) * SSD_SUB)
            cum = jnp.dot(tri, dta[sl], precision=lax.Precision.HIGHEST,
                          preferred_element_type=F32)
            clast = cum[SSD_SUB - 1:SSD_SUB, :]
            ccol_ref[sl, :] = cum * LOG2E
            crowT_ref[:, sl] = ((cum - log_dt[sl]) * LOG2E).T
            e1_ref[sl, :] = _split_bf16(jnp.exp(cum))
            wl_ref[sl, :] = _split_bf16(jnp.exp(clast - cum) * dtv[sl])
            elast_ref[s] = jnp.exp(clast)

    xcur = jnp.concatenate([xs_ref[...].astype(F32), b_ref[...].astype(F32), c_ref[...].astype(F32)], axis=1)
    cw = cw_ref[...]
    bias = cb_ref[...]
    pad_ref[0:SUBLANES, :] = tail_ref[g]
    pad_ref[SUBLANES:2 * SUBLANES, :] = xcur[0:SUBLANES]
    tail_ref[g] = xcur[rows - SUBLANES:rows]
    head = bias
    body = bias + cw[CONV_WIDTH - 1:CONV_WIDTH, :] * xcur
    for k in range(CONV_WIDTH):
        lo = SUBLANES - (CONV_WIDTH - 1) + k
        head = head + cw[k:k + 1, :] * pad_ref[lo:lo + SUBLANES, :]
        if k < CONV_WIDTH - 1:
            body = body + cw[k:k + 1, :] * pltpu.roll(xcur, CONV_WIDTH - 1 - k, 0)
    act_ref[...] = _silu(jnp.concatenate([head, body[SUBLANES:]], axis=0))

    shift = lax.rem(LANES - HEADS_PER_GROUP * g, LANES)
    r0 = pl.multiple_of(g * HEADS_PER_GROUP, HEADS_PER_GROUP)
    lane16 = sub_col.astype(F32).astype(BF16)
    low_half = lane16 < SSM_HEAD_DIM
    zero16 = jnp.zeros((SSD_SUB, LANES), BF16)
    sel_row = lax.broadcasted_iota(jnp.int32, (2 * LANES, GROUP_INNER), 0) & (LANES - 1)
    sel_head = lax.broadcasted_iota(jnp.int32, (2 * LANES, GROUP_INNER), 1) // SSM_HEAD_DIM
    expand = (sel_row == sel_head + HEADS_PER_GROUP * g).astype(F32).astype(BF16)

    for s in range(n_sub):
        rs = slice(s * SSD_SUB, (s + 1) * SSD_SUB)
        act = act_ref[rs, :]
        xg = act[:, 0:GROUP_INNER]
        xb = xg.astype(BF16)
        bb = act[:, GROUP_INNER:GROUP_INNER + SSM_STATE].astype(BF16)
        cc = act[:, GROUP_INNER + SSM_STATE:GROUP_CONV].astype(BF16)
        ccol = pltpu.roll(ccol_ref[rs, :], shift, 1)
        crow = crowT_ref[pl.ds(r0, HEADS_PER_GROUP), rs]
        cbm = lax.dot_general(cc, bb, (((1,), (1,)), ((), ())), preferred_element_type=F32).astype(BF16)

        ys = []
        for p in range(HEADS_PER_GROUP // 2):
            ms = []
            for r in (2 * p, 2 * p + 1):
                seg = ccol[:, r:r + 1] - crow[r:r + 1, :]
                ms.append(cbm * jnp.exp2(jnp.where(causal, seg, -jnp.inf)).astype(BF16))
            xp = xb[:, p * LANES:(p + 1) * LANES]
            bd = jnp.concatenate([jnp.where(low_half, xp, zero16), jnp.where(low_half, zero16, xp)], axis=0)
            ys.append(jnp.dot(jnp.concatenate(ms, axis=1), bd, preferred_element_type=F32))
        y = jnp.concatenate(ys, axis=1)

        st = state_ref[g]
        e1 = jnp.dot(e1_ref[rs, :], expand, preferred_element_type=F32)
        y = y + jnp.dot(cc, st.astype(BF16), preferred_element_type=F32) * e1
        wl = jnp.dot(wl_ref[rs, :], expand, preferred_element_type=F32)
        xw = (xg * wl).astype(BF16)
        ds = lax.dot_general(bb, xw, (((0,), (0,)), ((), ())), preferred_element_type=F32)
        el = pltpu.roll(elast_ref[s], shift, 1)
        state_ref[g] = st * _expand_heads(el, 1) + ds

        y = y + dskip_ref[...] * xg
        y = y * _silu(z_ref[rs, :].astype(F32))
        y_ref[rs, :] = (y * _rms_scale(y) * nw_ref[...]).astype(y_ref.dtype)


def _slab_spec(w, n_steps):
    layers, r, c = w.shape
    steps_per_slab = 1 if (layers * r // n_steps) % (2 * SUBLANES) == 0 else 2
    n_slabs = n_steps // steps_per_slab
    assert (layers * r) % n_slabs == 0
    slab_rows = layers * r // n_slabs
    assert r % slab_rows == 0 and slab_rows % (2 * SUBLANES) == 0
    per_layer = r // slab_rows

    def index(cb, g):
        slab = (cb * SSM_GROUPS + g) // steps_per_slab
        return slab // per_layer, slab % per_layer, 0

    return pl.BlockSpec((None, slab_rows, c), index)


def ssd_mixer(proj, dt_raw, conv_w_g, conv_b_g, dt_bias, a_log, d_skip, norm_w, side_casts):
    seq = proj.shape[0]
    rows = min(SSD_ROWS, seq)
    n_sub = rows // SSD_SUB
    xcol0 = D_INNER // GROUP_INNER
    bcol0 = 2 * D_INNER // SSM_STATE
    ccol0 = bcol0 + GN // SSM_STATE
    n_steps = (seq // rows) * SSM_GROUPS
    w_specs = [_slab_spec(w, n_steps) for w in side_casts]
    return pl.pallas_call(
        functools.partial(_ssd_kernel, n_side=len(side_casts)), grid=(seq // rows, SSM_GROUPS),
        in_specs=[pl.BlockSpec((rows, GROUP_INNER), lambda c, g: (c, g)),
                  pl.BlockSpec((rows, GROUP_INNER), lambda c, g: (c, xcol0 + g)),
                  pl.BlockSpec((rows, SSM_STATE), lambda c, g: (c, bcol0 + g)),
                  pl.BlockSpec((rows, SSM_STATE), lambda c, g: (c, ccol0 + g)),
                  pl.BlockSpec((rows, LANES), lambda c, g: (c, 0)),
                  pl.BlockSpec((None, CONV_WIDTH, GROUP_CONV), lambda c, g: (g, 0, 0)),
                  pl.BlockSpec((None, 1, GROUP_CONV), lambda c, g: (g, 0, 0)),
                  pl.BlockSpec((1, LANES), lambda c, g: (0, 0)),
                  pl.BlockSpec((1, LANES), lambda c, g: (0, 0)),
                  pl.BlockSpec((1, GROUP_INNER), lambda c, g: (0, g)),
                  pl.BlockSpec((1, GROUP_INNER), lambda c, g: (0, g))] + w_specs,
        out_specs=[pl.BlockSpec((rows, GROUP_INNER), lambda c, g: (c, g))] + w_specs,
        out_shape=[jax.ShapeDtypeStruct((seq, D_INNER), BF16)]
        + [jax.ShapeDtypeStruct(w.shape, BF16) for w in side_casts],
        scratch_shapes=[pltpu.VMEM((SSM_GROUPS, SSM_STATE, GROUP_INNER), F32),
                        pltpu.VMEM((SSM_GROUPS, SUBLANES, GROUP_CONV), F32),
                        pltpu.VMEM((2 * SUBLANES, GROUP_CONV), F32),
                        pltpu.VMEM((rows, GROUP_CONV), F32),
                        pltpu.VMEM((rows, LANES), F32),
                        pltpu.VMEM((LANES, rows), F32),
                        pltpu.VMEM((rows, 2 * LANES), BF16),
                        pltpu.VMEM((rows, 2 * LANES), BF16),
                        pltpu.VMEM((n_sub, 1, LANES), F32)],
        compiler_params=_params("arbitrary", "arbitrary"), name="ssd_mixer",
    )(proj, proj, proj, proj, dt_raw, conv_w_g, conv_b_g, dt_bias, a_log, d_skip, norm_w, *side_casts)


def _attn_kernel(sink_ref, qt_ref, kvp_ref, kvc_ref, o_ref, bias_ref):
    i = pl.program_id(0)
    group_lanes = Q_PER_KV * BLOCK

    @pl.when(i == 0)
    def _():
        sj = lax.broadcasted_iota(jnp.int32, (2 * BLOCK, group_lanes), 0)
        lane = lax.broadcasted_iota(jnp.int32, (2 * BLOCK, group_lanes), 1)
        dist_i = BLOCK + (lane & (BLOCK - 1)) - sj
        valid = (dist_i >= 0) & (dist_i < WINDOW)
        dist = dist_i.astype(F32)
        head_in_group = (lane // BLOCK).astype(F32)
        for kh in range(N_KV_HEADS):
            slope = jnp.exp2((head_in_group + (kh * Q_PER_KV + 1)) * (-8.0 / N_Q_HEADS))
            b = jnp.where(valid, -(slope * dist), -jnp.inf)
            bias_ref[1, kh] = b
            bias_ref[0, kh] = jnp.where(sj >= BLOCK, b, -jnp.inf)

    scale = HEAD_DIM ** -0.5
    kv_all = jnp.concatenate([kvp_ref[...], kvc_ref[...]], axis=0)
    for blk in range(ATTN_BLOCKS):
        has_prev = jnp.minimum(i, 1) if blk == 0 else 1
        kv = kv_all[blk * BLOCK:(blk + 2) * BLOCK]
        qs = slice(blk * BLOCK, (blk + 1) * BLOCK)
        for kh in range(N_KV_HEADS):
            k = kv[:, kh * HEAD_DIM:(kh + 1) * HEAD_DIM] * scale
            v = kv[:, KV_DIM + kh * HEAD_DIM:KV_DIM + (kh + 1) * HEAD_DIM]
            h0 = kh * Q_PER_KV
            qg = jnp.concatenate([qt_ref[(h0 + r) * HEAD_DIM:(h0 + r + 1) * HEAD_DIM, qs] for r in range(Q_PER_KV)],
                                 axis=1)
            sink = sink_ref[kh]
            logits = jnp.dot(k, qg, preferred_element_type=F32) + bias_ref[has_prev, kh]
            m = jnp.maximum(jnp.max(logits, axis=0, keepdims=True), sink)
            p = jnp.exp(logits - m)
            denom = jnp.sum(p, axis=0, keepdims=True) + jnp.exp(sink - m)
            ot = lax.dot_general(v, p.astype(BF16), (((0,), (0,)), ((), ())), preferred_element_type=F32) / denom
            for r in range(0, Q_PER_KV, 2):
                pair = jnp.concatenate([ot[:, r * BLOCK:(r + 1) * BLOCK], ot[:, (r + 1) * BLOCK:(r + 2) * BLOCK]], axis=0)
                o_ref[qs, (h0 + r) * HEAD_DIM:(h0 + r + 2) * HEAD_DIM] = pair.T.astype(o_ref.dtype)


def swa_attention(qt, kv, sinks):
    seq = kv.shape[0]
    step_rows = ATTN_BLOCKS * BLOCK
    nb = seq // step_rows
    sink_rows = jnp.repeat(sinks.astype(F32), BLOCK).reshape(N_KV_HEADS, 1, Q_PER_KV * BLOCK)
    return pl.pallas_call(
        _attn_kernel, grid=(nb,),
        in_specs=[pl.BlockSpec((N_KV_HEADS, 1, Q_PER_KV * BLOCK), lambda i: (0, 0, 0)),
                  pl.BlockSpec((N_Q_HEADS * HEAD_DIM, step_rows), lambda i: (0, i)),
                  pl.BlockSpec((BLOCK, 2 * KV_DIM), lambda i: (jnp.maximum(ATTN_BLOCKS * i - 1, 0), 0)),
                  pl.BlockSpec((step_rows, 2 * KV_DIM), lambda i: (i, 0))],
        out_specs=pl.BlockSpec((step_rows, N_Q_HEADS * HEAD_DIM), lambda i: (i, 0)),
        out_shape=jax.ShapeDtypeStruct((seq, N_Q_HEADS * HEAD_DIM), BF16),
        scratch_shapes=[pltpu.VMEM((2, N_KV_HEADS, 2 * BLOCK, Q_PER_KV * BLOCK), F32)],
        compiler_params=_params("arbitrary"), name="swa_attention",
    )(sink_rows, qt, kv, kv)


def _pad_lanes(v, width=LANES):
    v = v.reshape(1, -1).astype(F32)
    return jnp.pad(v, ((0, 0), (0, width - v.shape[1])))


def kernel(x, norm_w, ssm_w_in, ssm_conv_w, ssm_conv_b, ssm_dt_bias, ssm_A_log, ssm_D, ssm_norm_w, ssm_w_out,
           kv_norm_w, w_kv, attn_w_q, attn_sinks, attn_w_o, ffn_w_gate, ffn_w_up, ffn_w_down):
    batch, seq, d = x.shape
    assert batch == 1 and d == D_MODEL and seq % SSD_ROWS == 0
    h = x.reshape(seq, d)
    row = lambda v: v.reshape(1, -1).astype(F32)

    g = norm_w[0]
    w_in_t = ssm_w_in[0].T.astype(BF16)
    n_main = 2 * D_INNER + 2 * GN
    w_dt_t = jnp.pad(w_in_t[n_main:], ((0, LANES - SSM_HEADS), (0, 0)))
    proj, dt_raw = norm_matmul2_nt(h, row(g[0]), w_in_t, w_dt_t, n=n_main, name="in_proj")

    def per_group(t):
        xs = t[:, :D_INNER].reshape(-1, SSM_GROUPS, GROUP_INNER)
        bs = t[:, D_INNER:D_INNER + GN].reshape(-1, SSM_GROUPS, SSM_STATE)
        cs = t[:, D_INNER + GN:].reshape(-1, SSM_GROUPS, SSM_STATE)
        return jnp.transpose(jnp.concatenate([xs, bs, cs], axis=-1), (1, 0, 2)).astype(F32)

    later_weights = (ffn_w_gate, ffn_w_up, ffn_w_down, ssm_w_out, attn_w_q, attn_w_o, w_kv[None])
    y, w_gate, w_up, w_down, w_out, w_q, w_o, w_kv16 = ssd_mixer(
        proj, dt_raw, per_group(ssm_conv_w[0]), per_group(ssm_conv_b[0].reshape(1, -1)),
        _pad_lanes(ssm_dt_bias[0]), _pad_lanes(ssm_A_log[0]),
        row(jnp.repeat(ssm_D[0], SSM_HEAD_DIM)), row(ssm_norm_w[0]), later_weights)
    h = matmul_norm_residual(y, w_out[0], h, row(g[1]), name="ssm_out_proj")
    h = ffn(h, row(g[2]), w_gate, w_up, w_down, row(g[3]), 0, name="ffn0")

    g = norm_w[1]
    qt, kv = qkv_proj(h, row(g[0]), row(kv_norm_w), w_q[0], w_kv16[0], name="qkv_proj")
    a = swa_attention(qt, kv, attn_sinks[0])
    h = matmul_norm_residual(a, w_o[0], h, row(g[1]), name="attn_out_proj")
    h = ffn(h, row(g[2]), w_gate, w_up, w_down, row(g[3]), 1, name="ffn1")
    return h.reshape(batch, seq, d)
```

```python
import functools
import math

import jax
import jax.numpy as jnp
from jax import lax
from jax.experimental import pallas as pl
from jax.experimental.pallas import tpu as pltpu

D_MODEL = 2048
D_INNER = 4096
SSM_HEAD_DIM = 64
SSM_HEADS = 64
SSM_GROUPS = 8
HEADS_PER_GROUP = SSM_HEADS // SSM_GROUPS
SSM_STATE = 128
CONV_WIDTH = 4
GN = SSM_GROUPS * SSM_STATE
GROUP_INNER = D_INNER // SSM_GROUPS
GROUP_CONV = GROUP_INNER + 2 * SSM_STATE
N_SSD_INPUTS = 11
SSD_SUB = 128
SSD_ROWS = 1024
N_Q_HEADS = 32
N_KV_HEADS = 4
Q_PER_KV = N_Q_HEADS // N_KV_HEADS
HEAD_DIM = 64
WINDOW = 128
BLOCK = 128
ATTN_BLOCKS = 8
KV_DIM = N_KV_HEADS * HEAD_DIM
EPS = 1e-6
LOG2E = math.log2(math.e)

LANES = 128
SUBLANES = 8
VMEM_LIMIT_BYTES = 63 * 1024 * 1024

ROW_TILE = 1024
IN_PROJ_COL_TILE = 2048
OUT_PROJ_ROW_TILE = 512
FFN_CHUNK = 512

BF16 = jnp.bfloat16
F32 = jnp.float32


def _rms_scale(x):
    return lax.rsqrt(jnp.mean(x * x, axis=-1, keepdims=True) + EPS)


def _silu(x):
    return x * jax.nn.sigmoid(x)


def _params(*semantics):
    return pltpu.CompilerParams(dimension_semantics=semantics, vmem_limit_bytes=VMEM_LIMIT_BYTES)


def _normed_rows(x_ref, g_ref, xn_ref, first):
    if not first:
        return xn_ref[...]
    x = x_ref[...]
    xn = (x * _rms_scale(x) * g_ref[...]).astype(xn_ref.dtype)
    xn_ref[...] = xn
    return xn


def _first_and_rest(step):
    j = pl.program_id(1)
    pl.when(j == 0)(lambda: step(True))
    pl.when(j > 0)(lambda: step(False))


def _first_middle_last(step):
    j = pl.program_id(1)
    last = pl.num_programs(1) - 1
    pl.when(j == 0)(lambda: step(True, False))
    pl.when(jnp.logical_and(j > 0, j < last))(lambda: step(False, False))
    pl.when(j == last)(lambda: step(False, True))


def _norm_matmul2_kernel(x_ref, g_ref, wt_ref, wt2_ref, o_ref, o2_ref, xn_ref):
    nt = (((1,), (1,)), ((), ()))

    def step(first):
        xn = _normed_rows(x_ref, g_ref, xn_ref, first)
        if first:
            o2_ref[...] = lax.dot_general(xn, wt2_ref[...], nt, preferred_element_type=F32)
        o_ref[...] = lax.dot_general(xn, wt_ref[...], nt, preferred_element_type=F32).astype(o_ref.dtype)

    _first_and_rest(step)


def _qkv_kernel(x_ref, gq_ref, gkv_ref, wq_ref, wkv_ref, qt_ref, kv_ref):
    x = x_ref[...]
    xs = x * _rms_scale(x)
    xq = (xs * gq_ref[...]).astype(BF16)
    xkv = (xs * gkv_ref[...]).astype(BF16)
    qt_ref[...] = lax.dot_general(wq_ref[...], xq, (((0,), (1,)), ((), ())),
                                  preferred_element_type=F32).astype(qt_ref.dtype)
    kv_ref[...] = jnp.dot(xkv, wkv_ref[...], preferred_element_type=F32).astype(kv_ref.dtype)


def qkv_proj(x, g_q, g_kv, w_q, w_kv, *, name):
    m, k = x.shape
    n_q, n_kv = w_q.shape[1], w_kv.shape[1]
    tm = min(ROW_TILE, m)
    resident = pl.Buffered(1)
    return pl.pallas_call(
        _qkv_kernel, grid=(m // tm,),
        in_specs=[pl.BlockSpec((tm, k), lambda i: (i, 0)),
                  pl.BlockSpec((1, k), lambda i: (0, 0)),
                  pl.BlockSpec((1, k), lambda i: (0, 0)),
                  pl.BlockSpec((k, n_q), lambda i: (0, 0), pipeline_mode=resident),
                  pl.BlockSpec((k, n_kv), lambda i: (0, 0), pipeline_mode=resident)],
        out_specs=[pl.BlockSpec((n_q, tm), lambda i: (0, i)), pl.BlockSpec((tm, n_kv), lambda i: (i, 0))],
        out_shape=[jax.ShapeDtypeStruct((n_q, m), BF16), jax.ShapeDtypeStruct((m, n_kv), BF16)],
        compiler_params=_params("parallel"), name=name)(x, g_q, g_kv, w_q, w_kv)


def norm_matmul2_nt(x, g, wt, wt2, *, n, name):
    m, k = x.shape
    n2 = wt2.shape[0]
    tm, tn = min(ROW_TILE, m), min(IN_PROJ_COL_TILE, n)
    assert n % tn == 0
    return pl.pallas_call(
        _norm_matmul2_kernel, grid=(m // tm, n // tn),
        in_specs=[pl.BlockSpec((tm, k), lambda i, j: (i, 0)),
                  pl.BlockSpec((1, k), lambda i, j: (0, 0)),
                  pl.BlockSpec((tn, k), lambda i, j: (j, 0)),
                  pl.BlockSpec((n2, k), lambda i, j: (0, 0))],
        out_specs=[pl.BlockSpec((tm, tn), lambda i, j: (i, j)), pl.BlockSpec((tm, n2), lambda i, j: (i, 0))],
        out_shape=[jax.ShapeDtypeStruct((m, n), BF16), jax.ShapeDtypeStruct((m, n2), F32)],
        scratch_shapes=[pltpu.VMEM((tm, k), BF16)],
        compiler_params=_params("parallel", "arbitrary"), name=name)(x, g, wt, wt2)


def _matmul_norm_residual_kernel(y_ref, w_ref, h_ref, g_ref, o_ref):
    a = jnp.dot(y_ref[...], w_ref[...], preferred_element_type=F32)
    o_ref[...] = h_ref[...] + a * _rms_scale(a) * g_ref[...]


def matmul_norm_residual(y, w, h, g, *, name):
    m, k = y.shape
    n = w.shape[1]
    tm = min(OUT_PROJ_ROW_TILE, m)
    return pl.pallas_call(
        _matmul_norm_residual_kernel, grid=(m // tm,),
        in_specs=[pl.BlockSpec((tm, k), lambda i: (i, 0)),
                  pl.BlockSpec((k, n), lambda i: (0, 0), pipeline_mode=pl.Buffered(1)),
                  pl.BlockSpec((tm, n), lambda i: (i, 0)),
                  pl.BlockSpec((1, n), lambda i: (0, 0))],
        out_specs=pl.BlockSpec((tm, n), lambda i: (i, 0)),
        out_shape=jax.ShapeDtypeStruct((m, n), F32),
        compiler_params=_params("parallel"), name=name)(y, w, h, g)


def _ffn_kernel(h_ref, gpre_ref, wg_ref, wu_ref, wd_ref, gpost_ref, o_ref, xn_ref):
    def step(first, final):
        xn = _normed_rows(h_ref, gpre_ref, xn_ref, first)
        gate = jnp.dot(xn, wg_ref[...], preferred_element_type=F32)
        up = jnp.dot(xn, wu_ref[...], preferred_element_type=F32)
        act = (_silu(gate) * up).astype(BF16)
        d = jnp.dot(act, wd_ref[...], preferred_element_type=F32)
        a = d if first else o_ref[...] + d
        o_ref[...] = h_ref[...] + a * _rms_scale(a) * gpost_ref[...] if final else a

    _first_middle_last(step)


def ffn(h, g_pre, w_gate, w_up, w_down, g_post, layer, *, name):
    m, d = h.shape
    dff = w_gate.shape[2]
    tm, tf = min(ROW_TILE, m), FFN_CHUNK
    assert dff // tf >= 2
    return pl.pallas_call(
        _ffn_kernel, grid=(m // tm, dff // tf),
        in_specs=[pl.BlockSpec((tm, d), lambda i, f: (i, 0)),
                  pl.BlockSpec((1, d), lambda i, f: (0, 0)),
                  pl.BlockSpec((None, d, tf), lambda i, f: (layer, 0, f)),
                  pl.BlockSpec((None, d, tf), lambda i, f: (layer, 0, f)),
                  pl.BlockSpec((None, tf, d), lambda i, f: (layer, f, 0)),
                  pl.BlockSpec((1, d), lambda i, f: (0, 0))],
        out_specs=pl.BlockSpec((tm, d), lambda i, f: (i, 0)),
        out_shape=jax.ShapeDtypeStruct((m, d), F32),
        scratch_shapes=[pltpu.VMEM((tm, d), BF16)],
        compiler_params=_params("parallel", "arbitrary"), name=name)(h, g_pre, w_gate, w_up, w_down, g_post)


def _expand_heads(v, rows):
    lane = lax.broadcasted_iota(jnp.int32, (rows, LANES), 1)
    first = lane < SSM_HEAD_DIM
    pairs = [jnp.where(first, v[:, 2 * p:2 * p + 1], v[:, 2 * p + 1:2 * p + 2]) for p in range(HEADS_PER_GROUP // 2)]
    return jnp.concatenate(pairs, axis=1)


def _split_bf16(v):
    hi = v.astype(BF16)
    lo = (v - hi.astype(F32)).astype(BF16)
    return jnp.concatenate([hi, lo], axis=1)


def _ssd_kernel(*refs, n_side):
    (z_ref, xs_ref, b_ref, c_ref, dt_ref, cw_ref, cb_ref, dtb_ref, alog_ref, dskip_ref, nw_ref) = refs[:N_SSD_INPUTS]
    side_in = refs[N_SSD_INPUTS:N_SSD_INPUTS + n_side]
    y_ref = refs[N_SSD_INPUTS + n_side]
    side_out = refs[N_SSD_INPUTS + n_side + 1:N_SSD_INPUTS + 2 * n_side + 1]
    (state_ref, tail_ref, pad_ref, act_ref, ccol_ref, crowT_ref, e1_ref, wl_ref,
     elast_ref, expand_ref) = refs[N_SSD_INPUTS + 2 * n_side + 1:]
    c = pl.program_id(0)
    g = pl.program_id(1)

    for w32_ref, w16_ref in zip(side_in, side_out):
        w16_ref[...] = w32_ref[...].astype(BF16)

    rows = z_ref.shape[0]
    n_sub = rows // SSD_SUB
    sub_row = lax.broadcasted_iota(jnp.int32, (SSD_SUB, SSD_SUB), 0)
    sub_col = lax.broadcasted_iota(jnp.int32, (SSD_SUB, SSD_SUB), 1)
    causal = sub_row >= sub_col

    @pl.when(c == 0)
    def _():
        state_ref[g] = jnp.zeros((SSM_STATE, GROUP_INNER), F32)
        tail_ref[g] = jnp.zeros((SUBLANES, GROUP_CONV), F32)
        sel_row = lax.broadcasted_iota(jnp.int32, (2 * LANES, GROUP_INNER), 0) & (LANES - 1)
        sel_head = lax.broadcasted_iota(jnp.int32, (2 * LANES, GROUP_INNER), 1) // SSM_HEAD_DIM
        expand_ref[g] = (sel_row == sel_head + HEADS_PER_GROUP * g).astype(F32).astype(BF16)

    @pl.when(g == 0)
    def _():
        dtv = jax.nn.softplus(dt_ref[...] + dtb_ref[...])
        dta = dtv * (-jnp.exp(alog_ref[...]))
        log_dt = jnp.log(dtv)
        tri = causal.astype(F32)
        for s in range(n_sub):
            sl = slice(s * SSD_SUB, (s + 1) * SSD_SUB)
            cum = jnp.dot(tri, dta[sl], precision=lax.Precision.HIGHEST,
                          preferred_element_type=F32)
            clast = cum[SSD_SUB - 1:SSD_SUB, :]
            ccol_ref[sl, :] = cum * LOG2E
            crowT_ref[:, sl] = ((cum - log_dt[sl]) * LOG2E).T
            e1_ref[sl, :] = _split_bf16(jnp.exp(cum))
            wl_ref[sl, :] = _split_bf16(jnp.exp(clast - cum) * dtv[sl])
            elast_ref[s] = jnp.exp(clast)

    xcur = jnp.concatenate([xs_ref[...].astype(F32), b_ref[...].astype(F32), c_ref[...].astype(F32)], axis=1)
    cw = cw_ref[...]
    bias = cb_ref[...]
    pad_ref[0:SUBLANES, :] = tail_ref[g]
    pad_ref[SUBLANES:2 * SUBLANES, :] = xcur[0:SUBLANES]
    tail_ref[g] = xcur[rows - SUBLANES:rows]
    head = bias
    body = bias + cw[CONV_WIDTH - 1:CONV_WIDTH, :] * xcur
    for k in range(CONV_WIDTH):
        lo = SUBLANES - (CONV_WIDTH - 1) + k
        head = head + cw[k:k + 1, :] * pad_ref[lo:lo + SUBLANES, :]
        if k < CONV_WIDTH - 1:
            body = body + cw[k:k + 1, :] * pltpu.roll(xcur, CONV_WIDTH - 1 - k, 0)
    act_ref[...] = _silu(jnp.concatenate([head, body[SUBLANES:]], axis=0))

    shift = lax.rem(LANES - HEADS_PER_GROUP * g, LANES)
    r0 = pl.multiple_of(g * HEADS_PER_GROUP, HEADS_PER_GROUP)
    lane16 = sub_col.astype(F32).astype(BF16)
    low_half = lane16 < SSM_HEAD_DIM
    zero16 = jnp.zeros((SSD_SUB, LANES), BF16)
    expand = expand_ref[g]

    for s in range(n_sub):
        rs = slice(s * SSD_SUB, (s + 1) * SSD_SUB)
        act = act_ref[rs, :]
        xg = act[:, 0:GROUP_INNER]
        xb = xg.astype(BF16)
        bb = act[:, GROUP_INNER:GROUP_INNER + SSM_STATE].astype(BF16)
        cc = act[:, GROUP_INNER + SSM_STATE:GROUP_CONV].astype(BF16)
        ccol = pltpu.roll(ccol_ref[rs, :], shift, 1)
        crow = crowT_ref[pl.ds(r0, HEADS_PER_GROUP), rs]
        cbm = lax.dot_general(cc, bb, (((1,), (1,)), ((), ())), preferred_element_type=F32).astype(BF16)

        ys = []
        for p in range(HEADS_PER_GROUP // 2):
            ms = []
            for r in (2 * p, 2 * p + 1):
                seg = ccol[:, r:r + 1] - crow[r:r + 1, :]
                ms.append(cbm * jnp.exp2(jnp.where(causal, seg, -jnp.inf)).astype(BF16))
            xp = xb[:, p * LANES:(p + 1) * LANES]
            bd = jnp.concatenate([jnp.where(low_half, xp, zero16), jnp.where(low_half, zero16, xp)], axis=0)
            ys.append(jnp.dot(jnp.concatenate(ms, axis=1), bd, preferred_element_type=F32))
        y = jnp.concatenate(ys, axis=1)

        st = state_ref[g]
        e1 = jnp.dot(e1_ref[rs, :], expand, preferred_element_type=F32)
        y = y + jnp.dot(cc, st.astype(BF16), preferred_element_type=F32) * e1
        wl = jnp.dot(wl_ref[rs, :], expand, preferred_element_type=F32)
        xw = (xg * wl).astype(BF16)
        ds = lax.dot_general(bb, xw, (((0,), (0,)), ((), ())), preferred_element_type=F32)
        el = pltpu.roll(elast_ref[s], shift, 1)
        state_ref[g] = st * _expand_heads(el, 1) + ds

        y = y + dskip_ref[...] * xg
        y = y * _silu(z_ref[rs, :].astype(F32))
        y_ref[rs, :] = (y * _rms_scale(y) * nw_ref[...]).astype(y_ref.dtype)


def _slab_spec(w, n_steps):
    layers, r, c = w.shape
    steps_per_slab = 1 if (layers * r // n_steps) % (2 * SUBLANES) == 0 else 2
    n_slabs = n_steps // steps_per_slab
    assert (layers * r) % n_slabs == 0
    slab_rows = layers * r // n_slabs
    assert r % slab_rows == 0 and slab_rows % (2 * SUBLANES) == 0
    per_layer = r // slab_rows

    def index(cb, g):
        slab = (cb * SSM_GROUPS + g) // steps_per_slab
        return slab // per_layer, slab % per_layer, 0

    return pl.BlockSpec((None, slab_rows, c), index)


def ssd_mixer(proj, dt_raw, conv_w_g, conv_b_g, dt_bias, a_log, d_skip, norm_w, side_casts):
    seq = proj.shape[0]
    rows = min(SSD_ROWS, seq)
    n_sub = rows // SSD_SUB
    xcol0 = D_INNER // GROUP_INNER
    bcol0 = 2 * D_INNER // SSM_STATE
    ccol0 = bcol0 + GN // SSM_STATE
    n_steps = (seq // rows) * SSM_GROUPS
    w_specs = [_slab_spec(w, n_steps) for w in side_casts]
    return pl.pallas_call(
        functools.partial(_ssd_kernel, n_side=len(side_casts)), grid=(seq // rows, SSM_GROUPS),
        in_specs=[pl.BlockSpec((rows, GROUP_INNER), lambda c, g: (c, g)),
                  pl.BlockSpec((rows, GROUP_INNER), lambda c, g: (c, xcol0 + g)),
                  pl.BlockSpec((rows, SSM_STATE), lambda c, g: (c, bcol0 + g)),
                  pl.BlockSpec((rows, SSM_STATE), lambda c, g: (c, ccol0 + g)),
                  pl.BlockSpec((rows, LANES), lambda c, g: (c, 0)),
                  pl.BlockSpec((None, CONV_WIDTH, GROUP_CONV), lambda c, g: (g, 0, 0)),
                  pl.BlockSpec((None, 1, GROUP_CONV), lambda c, g: (g, 0, 0)),
                  pl.BlockSpec((1, LANES), lambda c, g: (0, 0)),
                  pl.BlockSpec((1, LANES), lambda c, g: (0, 0)),
                  pl.BlockSpec((1, GROUP_INNER), lambda c, g: (0, g)),
                  pl.BlockSpec((1, GROUP_INNER), lambda c, g: (0, g))] + w_specs,
        out_specs=[pl.BlockSpec((rows, GROUP_INNER), lambda c, g: (c, g))] + w_specs,
        out_shape=[jax.ShapeDtypeStruct((seq, D_INNER), BF16)]
        + [jax.ShapeDtypeStruct(w.shape, BF16) for w in side_casts],
        scratch_shapes=[pltpu.VMEM((SSM_GROUPS, SSM_STATE, GROUP_INNER), F32),
                        pltpu.VMEM((SSM_GROUPS, SUBLANES, GROUP_CONV), F32),
                        pltpu.VMEM((2 * SUBLANES, GROUP_CONV), F32),
                        pltpu.VMEM((rows, GROUP_CONV), F32),
                        pltpu.VMEM((rows, LANES), F32),
                        pltpu.VMEM((LANES, rows), F32),
                        pltpu.VMEM((rows, 2 * LANES), BF16),
                        pltpu.VMEM((rows, 2 * LANES), BF16),
                        pltpu.VMEM((n_sub, 1, LANES), F32),
                        pltpu.VMEM((SSM_GROUPS, 2 * LANES, GROUP_INNER), BF16)],
        compiler_params=_params("arbitrary", "arbitrary"), name="ssd_mixer",
    )(proj, proj, proj, proj, dt_raw, conv_w_g, conv_b_g, dt_bias, a_log, d_skip, norm_w, *side_casts)


def _attn_kernel(sink_ref, qt_ref, kvp_ref, kvc_ref, o_ref, bias_ref):
    i = pl.program_id(0)
    group_lanes = Q_PER_KV * BLOCK

    @pl.when(i == 0)
    def _():
        sj = lax.broadcasted_iota(jnp.int32, (2 * BLOCK, group_lanes), 0)
        lane = lax.broadcasted_iota(jnp.int32, (2 * BLOCK, group_lanes), 1)
        dist_i = BLOCK + (lane & (BLOCK - 1)) - sj
        valid = (dist_i >= 0) & (dist_i < WINDOW)
        dist = dist_i.astype(F32)
        head_in_group = (lane // BLOCK).astype(F32)
        for kh in range(N_KV_HEADS):
            slope = jnp.exp2((head_in_group + (kh * Q_PER_KV + 1)) * (-8.0 / N_Q_HEADS))
            b = jnp.where(valid, -(slope * dist), -jnp.inf)
            bias_ref[1, kh] = b
            bias_ref[0, kh] = jnp.where(sj >= BLOCK, b, -jnp.inf)

    scale = HEAD_DIM ** -0.5
    kv_all = jnp.concatenate([kvp_ref[...], kvc_ref[...]], axis=0)
    for blk in range(ATTN_BLOCKS):
        has_prev = jnp.minimum(i, 1) if blk == 0 else 1
        kv = kv_all[blk * BLOCK:(blk + 2) * BLOCK]
        qs = slice(blk * BLOCK, (blk + 1) * BLOCK)
        for kh in range(N_KV_HEADS):
            k = kv[:, kh * HEAD_DIM:(kh + 1) * HEAD_DIM] * scale
            v = kv[:, KV_DIM + kh * HEAD_DIM:KV_DIM + (kh + 1) * HEAD_DIM]
            h0 = kh * Q_PER_KV
            qg = jnp.concatenate([qt_ref[(h0 + r) * HEAD_DIM:(h0 + r + 1) * HEAD_DIM, qs] for r in range(Q_PER_KV)],
                                 axis=1)
            sink = sink_ref[kh]
            logits = jnp.dot(k, qg, preferred_element_type=F32) + bias_ref[has_prev, kh]
            m = jnp.maximum(jnp.max(logits, axis=0, keepdims=True), sink)
            p = jnp.exp(logits - m)
            denom = jnp.sum(p, axis=0, keepdims=True) + jnp.exp(sink - m)
            ot = lax.dot_general(v, p.astype(BF16), (((0,), (0,)), ((), ())), preferred_element_type=F32) / denom
            for r in range(0, Q_PER_KV, 2):
                pair = jnp.concatenate([ot[:, r * BLOCK:(r + 1) * BLOCK], ot[:, (r + 1) * BLOCK:(r + 2) * BLOCK]], axis=0)
                o_ref[qs, (h0 + r) * HEAD_DIM:(h0 + r + 2) * HEAD_DIM] = pair.T.astype(o_ref.dtype)


def swa_attention(qt, kv, sinks):
    seq = kv.shape[0]
    step_rows = ATTN_BLOCKS * BLOCK
    nb = seq // step_rows
    sink_rows = jnp.repeat(sinks.astype(F32), BLOCK).reshape(N_KV_HEADS, 1, Q_PER_KV * BLOCK)
    return pl.pallas_call(
        _attn_kernel, grid=(nb,),
        in_specs=[pl.BlockSpec((N_KV_HEADS, 1, Q_PER_KV * BLOCK), lambda i: (0, 0, 0)),
                  pl.BlockSpec((N_Q_HEADS * HEAD_DIM, step_rows), lambda i: (0, i)),
                  pl.BlockSpec((BLOCK, 2 * KV_DIM), lambda i: (jnp.maximum(ATTN_BLOCKS * i - 1, 0), 0)),
                  pl.BlockSpec((step_rows, 2 * KV_DIM), lambda i: (i, 0))],
        out_specs=pl.BlockSpec((step_rows, N_Q_HEADS * HEAD_DIM), lambda i: (i, 0)),
        out_shape=jax.ShapeDtypeStruct((seq, N_Q_HEADS * HEAD_DIM), BF16),
        scratch_shapes=[pltpu.VMEM((2, N_KV_HEADS, 2 * BLOCK, Q_PER_KV * BLOCK), F32)],
        compiler_params=_params("arbitrary"), name="swa_attention",
    )(sink_rows, qt, kv, kv)


def _pad_lanes(v, width=LANES):
    v = v.reshape(1, -1).astype(F32)
    return jnp.pad(v, ((0, 0), (0, width - v.shape[1])))


def kernel(x, norm_w, ssm_w_in, ssm_conv_w, ssm_conv_b, ssm_dt_bias, ssm_A_log, ssm_D, ssm_norm_w, ssm_w_out,
           kv_norm_w, w_kv, attn_w_q, attn_sinks, attn_w_o, ffn_w_gate, ffn_w_up, ffn_w_down):
    batch, seq, d = x.shape
    assert batch == 1 and d == D_MODEL and seq % SSD_ROWS == 0
    h = x.reshape(seq, d)
    row = lambda v: v.reshape(1, -1).astype(F32)

    g = norm_w[0]
    w_in_t = ssm_w_in[0].T.astype(BF16)
    n_main = 2 * D_INNER + 2 * GN
    w_dt_t = jnp.pad(w_in_t[n_main:], ((0, LANES - SSM_HEADS), (0, 0)))
    proj, dt_raw = norm_matmul2_nt(h, row(g[0]), w_in_t, w_dt_t, n=n_main, name="in_proj")

    def per_group(t):
        xs = t[:, :D_INNER].reshape(-1, SSM_GROUPS, GROUP_INNER)
        bs = t[:, D_INNER:D_INNER + GN].reshape(-1, SSM_GROUPS, SSM_STATE)
        cs = t[:, D_INNER + GN:].reshape(-1, SSM_GROUPS, SSM_STATE)
        return jnp.transpose(jnp.concatenate([xs, bs, cs], axis=-1), (1, 0, 2)).astype(F32)

    later_weights = (ffn_w_gate, ffn_w_up, ffn_w_down, ssm_w_out, attn_w_q, attn_w_o, w_kv[None])
    y, w_gate, w_up, w_down, w_out, w_q, w_o, w_kv16 = ssd_mixer(
        proj, dt_raw, per_group(ssm_conv_w[0]), per_group(ssm_conv_b[0].reshape(1, -1)),
        _pad_lanes(ssm_dt_bias[0]), _pad_lanes(ssm_A_log[0]),
        row(jnp.repeat(ssm_D[0], SSM_HEAD_DIM)), row(ssm_norm_w[0]), later_weights)
    h = matmul_norm_residual(y, w_out[0], h, row(g[1]), name="ssm_out_proj")
    h = ffn(h, row(g[2]), w_gate, w_up, w_down, row(g[3]), 0, name="ffn0")

    g = norm_w[1]
    qt, kv = qkv_proj(h, row(g[0]), row(kv_norm_w), w_q[0], w_kv16[0], name="qkv_proj")
    a = swa_attention(qt, kv, attn_sinks[0])
    h = matmul_norm_residual(a, w_o[0], h, row(g[1]), name="attn_out_proj")
    h = ffn(h, row(g[2]), w_gate, w_up, w_down, row(g[3]), 1, name="ffn1")
    return h.reshape(batch, seq, d)
```

```python
import functools
import math

import jax
import jax.numpy as jnp
from jax import lax
from jax.experimental import pallas as pl
from jax.experimental.pallas import tpu as pltpu

D_MODEL = 2048
D_INNER = 4096
SSM_HEAD_DIM = 64
SSM_HEADS = 64
SSM_GROUPS = 8
HEADS_PER_GROUP = SSM_HEADS // SSM_GROUPS
SSM_STATE = 128
CONV_WIDTH = 4
GN = SSM_GROUPS * SSM_STATE
GROUP_INNER = D_INNER // SSM_GROUPS
GROUP_CONV = GROUP_INNER + 2 * SSM_STATE
N_SSD_INPUTS = 11
SSD_SUB = 128
SSD_ROWS = 2048
N_Q_HEADS = 32
N_KV_HEADS = 4
Q_PER_KV = N_Q_HEADS // N_KV_HEADS
HEAD_DIM = 64
WINDOW = 128
BLOCK = 128
ATTN_BLOCKS = 8
KV_DIM = N_KV_HEADS * HEAD_DIM
EPS = 1e-6
LOG2E = math.log2(math.e)

LANES = 128
SUBLANES = 8
VMEM_LIMIT_BYTES = 63 * 1024 * 1024

ROW_TILE = 1024
IN_PROJ_COL_TILE = 2048
OUT_PROJ_ROW_TILE = 512
FFN_CHUNK = 512

BF16 = jnp.bfloat16
F32 = jnp.float32


def _rms_scale(x):
    return lax.rsqrt(jnp.mean(x * x, axis=-1, keepdims=True) + EPS)


def _silu(x):
    return x * jax.nn.sigmoid(x)


def _params(*semantics):
    return pltpu.CompilerParams(dimension_semantics=semantics, vmem_limit_bytes=VMEM_LIMIT_BYTES)


def _normed_rows(x_ref, g_ref, xn_ref, first):
    if not first:
        return xn_ref[...]
    x = x_ref[...]
    xn = (x * _rms_scale(x) * g_ref[...]).astype(xn_ref.dtype)
    xn_ref[...] = xn
    return xn


def _first_and_rest(step):
    j = pl.program_id(1)
    pl.when(j == 0)(lambda: step(True))
    pl.when(j > 0)(lambda: step(False))


def _first_middle_last(step):
    j = pl.program_id(1)
    last = pl.num_programs(1) - 1
    pl.when(j == 0)(lambda: step(True, False))
    pl.when(jnp.logical_and(j > 0, j < last))(lambda: step(False, False))
    pl.when(j == last)(lambda: step(False, True))


def _norm_matmul2_kernel(x_ref, g_ref, wt_ref, wt2_ref, o_ref, o2_ref, xn_ref):
    nt = (((1,), (1,)), ((), ()))

    def step(first):
        xn = _normed_rows(x_ref, g_ref, xn_ref, first)
        if first:
            o2_ref[...] = lax.dot_general(xn, wt2_ref[...], nt, preferred_element_type=F32)
        o_ref[...] = lax.dot_general(xn, wt_ref[...], nt, preferred_element_type=F32).astype(o_ref.dtype)

    _first_and_rest(step)


def _qkv_kernel(x_ref, gq_ref, gkv_ref, wq_ref, wkv_ref, qt_ref, kv_ref):
    x = x_ref[...]
    xs = x * _rms_scale(x)
    xq = (xs * gq_ref[...]).astype(BF16)
    xkv = (xs * gkv_ref[...]).astype(BF16)
    qt_ref[...] = lax.dot_general(wq_ref[...], xq, (((0,), (1,)), ((), ())),
                                  preferred_element_type=F32).astype(qt_ref.dtype)
    kv_ref[...] = jnp.dot(xkv, wkv_ref[...], preferred_element_type=F32).astype(kv_ref.dtype)


def qkv_proj(x, g_q, g_kv, w_q, w_kv, *, name):
    m, k = x.shape
    n_q, n_kv = w_q.shape[1], w_kv.shape[1]
    tm = min(ROW_TILE, m)
    resident = pl.Buffered(1)
    return pl.pallas_call(
        _qkv_kernel, grid=(m // tm,),
        in_specs=[pl.BlockSpec((tm, k), lambda i: (i, 0)),
                  pl.BlockSpec((1, k), lambda i: (0, 0)),
                  pl.BlockSpec((1, k), lambda i: (0, 0)),
                  pl.BlockSpec((k, n_q), lambda i: (0, 0), pipeline_mode=resident),
                  pl.BlockSpec((k, n_kv), lambda i: (0, 0), pipeline_mode=resident)],
        out_specs=[pl.BlockSpec((n_q, tm), lambda i: (0, i)), pl.BlockSpec((tm, n_kv), lambda i: (i, 0))],
        out_shape=[jax.ShapeDtypeStruct((n_q, m), BF16), jax.ShapeDtypeStruct((m, n_kv), BF16)],
        compiler_params=_params("parallel"), name=name)(x, g_q, g_kv, w_q, w_kv)


def norm_matmul2_nt(x, g, wt, wt2, *, n, name):
    m, k = x.shape
    n2 = wt2.shape[0]
    tm, tn = min(ROW_TILE, m), min(IN_PROJ_COL_TILE, n)
    assert n % tn == 0
    return pl.pallas_call(
        _norm_matmul2_kernel, grid=(m // tm, n // tn),
        in_specs=[pl.BlockSpec((tm, k), lambda i, j: (i, 0)),
                  pl.BlockSpec((1, k), lambda i, j: (0, 0)),
                  pl.BlockSpec((tn, k), lambda i, j: (j, 0)),
                  pl.BlockSpec((n2, k), lambda i, j: (0, 0))],
        out_specs=[pl.BlockSpec((tm, tn), lambda i, j: (i, j)), pl.BlockSpec((tm, n2), lambda i, j: (i, 0))],
        out_shape=[jax.ShapeDtypeStruct((m, n), BF16), jax.ShapeDtypeStruct((m, n2), F32)],
        scratch_shapes=[pltpu.VMEM((tm, k), BF16)],
        compiler_params=_params("parallel", "arbitrary"), name=name)(x, g, wt, wt2)


def _matmul_norm_residual_kernel(y_ref, w_ref, h_ref, g_ref, o_ref):
    a = jnp.dot(y_ref[...], w_ref[...], preferred_element_type=F32)
    o_ref[...] = h_ref[...] + a * _rms_scale(a) * g_ref[...]


def matmul_norm_residual(y, w, h, g, *, name):
    m, k = y.shape
    n = w.shape[1]
    tm = min(OUT_PROJ_ROW_TILE, m)
    return pl.pallas_call(
        _matmul_norm_residual_kernel, grid=(m // tm,),
        in_specs=[pl.BlockSpec((tm, k), lambda i: (i, 0)),
                  pl.BlockSpec((k, n), lambda i: (0, 0), pipeline_mode=pl.Buffered(1)),
                  pl.BlockSpec((tm, n), lambda i: (i, 0)),
                  pl.BlockSpec((1, n), lambda i: (0, 0))],
        out_specs=pl.BlockSpec((tm, n), lambda i: (i, 0)),
        out_shape=jax.ShapeDtypeStruct((m, n), F32),
        compiler_params=_params("parallel"), name=name)(y, w, h, g)


def _ffn_kernel(h_ref, gpre_ref, wg_ref, wu_ref, wd_ref, gpost_ref, o_ref, xn_ref):
    def step(first, final):
        xn = _normed_rows(h_ref, gpre_ref, xn_ref, first)
        gate = jnp.dot(xn, wg_ref[...], preferred_element_type=F32)
        up = jnp.dot(xn, wu_ref[...], preferred_element_type=F32)
        act = (_silu(gate) * up).astype(BF16)
        d = jnp.dot(act, wd_ref[...], preferred_element_type=F32)
        a = d if first else o_ref[...] + d
        o_ref[...] = h_ref[...] + a * _rms_scale(a) * gpost_ref[...] if final else a

    _first_middle_last(step)


def ffn(h, g_pre, w_gate, w_up, w_down, g_post, layer, *, name):
    m, d = h.shape
    dff = w_gate.shape[2]
    tm, tf = min(ROW_TILE, m), FFN_CHUNK
    assert dff // tf >= 2
    return pl.pallas_call(
        _ffn_kernel, grid=(m // tm, dff // tf),
        in_specs=[pl.BlockSpec((tm, d), lambda i, f: (i, 0)),
                  pl.BlockSpec((1, d), lambda i, f: (0, 0)),
                  pl.BlockSpec((None, d, tf), lambda i, f: (layer, 0, f)),
                  pl.BlockSpec((None, d, tf), lambda i, f: (layer, 0, f)),
                  pl.BlockSpec((None, tf, d), lambda i, f: (layer, f, 0)),
                  pl.BlockSpec((1, d), lambda i, f: (0, 0))],
        out_specs=pl.BlockSpec((tm, d), lambda i, f: (i, 0)),
        out_shape=jax.ShapeDtypeStruct((m, d), F32),
        scratch_shapes=[pltpu.VMEM((tm, d), BF16)],
        compiler_params=_params("parallel", "arbitrary"), name=name)(h, g_pre, w_gate, w_up, w_down, g_post)


def _expand_heads(v, rows):
    lane = lax.broadcasted_iota(jnp.int32, (rows, LANES), 1)
    first = lane < SSM_HEAD_DIM
    pairs = [jnp.where(first, v[:, 2 * p:2 * p + 1], v[:, 2 * p + 1:2 * p + 2]) for p in range(HEADS_PER_GROUP // 2)]
    return jnp.concatenate(pairs, axis=1)


def _split_bf16(v):
    hi = v.astype(BF16)
    lo = (v - hi.astype(F32)).astype(BF16)
    return jnp.concatenate([hi, lo], axis=1)


def _ssd_kernel(*refs, n_side):
    (z_ref, xs_ref, b_ref, c_ref, dt_ref, cw_ref, cb_ref, dtb_ref, alog_ref, dskip_ref, nw_ref) = refs[:N_SSD_INPUTS]
    side_in = refs[N_SSD_INPUTS:N_SSD_INPUTS + n_side]
    y_ref = refs[N_SSD_INPUTS + n_side]
    side_out = refs[N_SSD_INPUTS + n_side + 1:N_SSD_INPUTS + 2 * n_side + 1]
    (state_ref, tail_ref, pad_ref, act_ref, ccol_ref, crowT_ref, e1_ref, wl_ref,
     elast_ref, expand_ref) = refs[N_SSD_INPUTS + 2 * n_side + 1:]
    c = pl.program_id(0)
    g = pl.program_id(1)

    for w32_ref, w16_ref in zip(side_in, side_out):
        w16_ref[...] = w32_ref[...].astype(BF16)

    rows = z_ref.shape[0]
    n_sub = rows // SSD_SUB
    sub_row = lax.broadcasted_iota(jnp.int32, (SSD_SUB, SSD_SUB), 0)
    sub_col = lax.broadcasted_iota(jnp.int32, (SSD_SUB, SSD_SUB), 1)
    causal = sub_row >= sub_col

    @pl.when(c == 0)
    def _():
        state_ref[g] = jnp.zeros((SSM_STATE, GROUP_INNER), F32)
        tail_ref[g] = jnp.zeros((SUBLANES, GROUP_CONV), F32)
        sel_row = lax.broadcasted_iota(jnp.int32, (2 * LANES, GROUP_INNER), 0) & (LANES - 1)
        sel_head = lax.broadcasted_iota(jnp.int32, (2 * LANES, GROUP_INNER), 1) // SSM_HEAD_DIM
        expand_ref[g] = (sel_row == sel_head + HEADS_PER_GROUP * g).astype(F32).astype(BF16)

    @pl.when(g == 0)
    def _():
        dtv = jax.nn.softplus(dt_ref[...] + dtb_ref[...])
        dta = dtv * (-jnp.exp(alog_ref[...]))
        log_dt = jnp.log(dtv)
        tri = causal.astype(F32)
        for s in range(n_sub):
            sl = slice(s * SSD_SUB, (s + 1) * SSD_SUB)
            cum = jnp.dot(tri, dta[sl], precision=lax.Precision.HIGHEST,
                          preferred_element_type=F32)
            clast = cum[SSD_SUB - 1:SSD_SUB, :]
            ccol_ref[sl, :] = cum * LOG2E
            crowT_ref[:, sl] = ((cum - log_dt[sl]) * LOG2E).T
            e1_ref[sl, :] = _split_bf16(jnp.exp(cum))
            wl_ref[sl, :] = _split_bf16(jnp.exp(clast - cum) * dtv[sl])
            elast_ref[s] = jnp.exp(clast)

    xcur = jnp.concatenate([xs_ref[...].astype(F32), b_ref[...].astype(F32), c_ref[...].astype(F32)], axis=1)
    cw = cw_ref[...]
    bias = cb_ref[...]
    pad_ref[0:SUBLANES, :] = tail_ref[g]
    pad_ref[SUBLANES:2 * SUBLANES, :] = xcur[0:SUBLANES]
    tail_ref[g] = xcur[rows - SUBLANES:rows]
    head = bias
    body = bias + cw[CONV_WIDTH - 1:CONV_WIDTH, :] * xcur
    for k in range(CONV_WIDTH):
        lo = SUBLANES - (CONV_WIDTH - 1) + k
        head = head + cw[k:k + 1, :] * pad_ref[lo:lo + SUBLANES, :]
        if k < CONV_WIDTH - 1:
            body = body + cw[k:k + 1, :] * pltpu.roll(xcur, CONV_WIDTH - 1 - k, 0)
    act_ref[...] = _silu(jnp.concatenate([head, body[SUBLANES:]], axis=0))

    shift = lax.rem(LANES - HEADS_PER_GROUP * g, LANES)
    r0 = pl.multiple_of(g * HEADS_PER_GROUP, HEADS_PER_GROUP)
    lane16 = sub_col.astype(F32).astype(BF16)
    low_half = lane16 < SSM_HEAD_DIM
    zero16 = jnp.zeros((SSD_SUB, LANES), BF16)
    expand = expand_ref[g]

    for s in range(n_sub):
        rs = slice(s * SSD_SUB, (s + 1) * SSD_SUB)
        act = act_ref[rs, :]
        xg = act[:, 0:GROUP_INNER]
        xb = xg.astype(BF16)
        bb = act[:, GROUP_INNER:GROUP_INNER + SSM_STATE].astype(BF16)
        cc = act[:, GROUP_INNER + SSM_STATE:GROUP_CONV].astype(BF16)
        ccol = pltpu.roll(ccol_ref[rs, :], shift, 1)
        crow = crowT_ref[pl.ds(r0, HEADS_PER_GROUP), rs]
        cbm = lax.dot_general(cc, bb, (((1,), (1,)), ((), ())), preferred_element_type=F32).astype(BF16)

        ys = []
        for p in range(HEADS_PER_GROUP // 2):
            ms = []
            for r in (2 * p, 2 * p + 1):
                seg = ccol[:, r:r + 1] - crow[r:r + 1, :]
                ms.append(cbm * jnp.exp2(jnp.where(causal, seg, -jnp.inf)).astype(BF16))
            xp = xb[:, p * LANES:(p + 1) * LANES]
            bd = jnp.concatenate([jnp.where(low_half, xp, zero16), jnp.where(low_half, zero16, xp)], axis=0)
            ys.append(jnp.dot(jnp.concatenate(ms, axis=1), bd, preferred_element_type=F32))
        y = jnp.concatenate(ys, axis=1)

        st = state_ref[g]
        e1 = jnp.dot(e1_ref[rs, :], expand, preferred_element_type=F32)
        y = y + jnp.dot(cc, st.astype(BF16), preferred_element_type=F32) * e1
        wl = jnp.dot(wl_ref[rs, :], expand, preferred_element_type=F32)
        xw = (xg * wl).astype(BF16)
        ds = lax.dot_general(bb, xw, (((0,), (0,)), ((), ())), preferred_element_type=F32)
        el = pltpu.roll(elast_ref[s], shift, 1)
        state_ref[g] = st * _expand_heads(el, 1) + ds

        y = y + dskip_ref[...] * xg
        y = y * _silu(z_ref[rs, :].astype(F32))
        y_ref[rs, :] = (y * _rms_scale(y) * nw_ref[...]).astype(y_ref.dtype)


def _slab_spec(w, n_steps):
    layers, r, c = w.shape
    steps_per_slab = 1 if (layers * r // n_steps) % (2 * SUBLANES) == 0 else 2
    n_slabs = n_steps // steps_per_slab
    assert (layers * r) % n_slabs == 0
    slab_rows = layers * r // n_slabs
    assert r % slab_rows == 0 and slab_rows % (2 * SUBLANES) == 0
    per_layer = r // slab_rows

    def index(cb, g):
        slab = (cb * SSM_GROUPS + g) // steps_per_slab
        return slab // per_layer, slab % per_layer, 0

    return pl.BlockSpec((None, slab_rows, c), index)


def ssd_mixer(proj, dt_raw, conv_w_g, conv_b_g, dt_bias, a_log, d_skip, norm_w, side_casts):
    seq = proj.shape[0]
    rows = min(SSD_ROWS, seq)
    n_sub = rows // SSD_SUB
    xcol0 = D_INNER // GROUP_INNER
    bcol0 = 2 * D_INNER // SSM_STATE
    ccol0 = bcol0 + GN // SSM_STATE
    n_steps = (seq // rows) * SSM_GROUPS
    w_specs = [_slab_spec(w, n_steps) for w in side_casts]
    return pl.pallas_call(
        functools.partial(_ssd_kernel, n_side=len(side_casts)), grid=(seq // rows, SSM_GROUPS),
        in_specs=[pl.BlockSpec((rows, GROUP_INNER), lambda c, g: (c, g)),
                  pl.BlockSpec((rows, GROUP_INNER), lambda c, g: (c, xcol0 + g)),
                  pl.BlockSpec((rows, SSM_STATE), lambda c, g: (c, bcol0 + g)),
                  pl.BlockSpec((rows, SSM_STATE), lambda c, g: (c, ccol0 + g)),
                  pl.BlockSpec((rows, LANES), lambda c, g: (c, 0)),
                  pl.BlockSpec((None, CONV_WIDTH, GROUP_CONV), lambda c, g: (g, 0, 0)),
                  pl.BlockSpec((None, 1, GROUP_CONV), lambda c, g: (g, 0, 0)),
                  pl.BlockSpec((1, LANES), lambda c, g: (0, 0)),
                  pl.BlockSpec((1, LANES), lambda c, g: (0, 0)),
                  pl.BlockSpec((1, GROUP_INNER), lambda c, g: (0, g)),
                  pl.BlockSpec((1, GROUP_INNER), lambda c, g: (0, g))] + w_specs,
        out_specs=[pl.BlockSpec((rows, GROUP_INNER), lambda c, g: (c, g))] + w_specs,
        out_shape=[jax.ShapeDtypeStruct((seq, D_INNER), BF16)]
        + [jax.ShapeDtypeStruct(w.shape, BF16) for w in side_casts],
        scratch_shapes=[pltpu.VMEM((SSM_GROUPS, SSM_STATE, GROUP_INNER), F32),
                        pltpu.VMEM((SSM_GROUPS, SUBLANES, GROUP_CONV), F32),
                        pltpu.VMEM((2 * SUBLANES, GROUP_CONV), F32),
                        pltpu.VMEM((rows, GROUP_CONV), F32),
                        pltpu.VMEM((rows, LANES), F32),
                        pltpu.VMEM((LANES, rows), F32),
                        pltpu.VMEM((rows, 2 * LANES), BF16),
                        pltpu.VMEM((rows, 2 * LANES), BF16),
                        pltpu.VMEM((n_sub, 1, LANES), F32),
                        pltpu.VMEM((SSM_GROUPS, 2 * LANES, GROUP_INNER), BF16)],
        compiler_params=_params("arbitrary", "arbitrary"), name="ssd_mixer",
    )(proj, proj, proj, proj, dt_raw, conv_w_g, conv_b_g, dt_bias, a_log, d_skip, norm_w, *side_casts)


def _attn_kernel(sink_ref, qt_ref, kvp_ref, kvc_ref, o_ref, bias_ref):
    i = pl.program_id(0)
    group_lanes = Q_PER_KV * BLOCK

    @pl.when(i == 0)
    def _():
        sj = lax.broadcasted_iota(jnp.int32, (2 * BLOCK, group_lanes), 0)
        lane = lax.broadcasted_iota(jnp.int32, (2 * BLOCK, group_lanes), 1)
        dist_i = BLOCK + (lane & (BLOCK - 1)) - sj
        valid = (dist_i >= 0) & (dist_i < WINDOW)
        dist = dist_i.astype(F32)
        head_in_group = (lane // BLOCK).astype(F32)
        for kh in range(N_KV_HEADS):
            slope = jnp.exp2((head_in_group + (kh * Q_PER_KV + 1)) * (-8.0 / N_Q_HEADS))
            b = jnp.where(valid, -(slope * dist), -jnp.inf)
            bias_ref[1, kh] = b
            bias_ref[0, kh] = jnp.where(sj >= BLOCK, b, -jnp.inf)

    scale = HEAD_DIM ** -0.5
    kv_all = jnp.concatenate([kvp_ref[...], kvc_ref[...]], axis=0)
    for blk in range(ATTN_BLOCKS):
        has_prev = jnp.minimum(i, 1) if blk == 0 else 1
        kv = kv_all[blk * BLOCK:(blk + 2) * BLOCK]
        qs = slice(blk * BLOCK, (blk + 1) * BLOCK)
        for kh in range(N_KV_HEADS):
            k = kv[:, kh * HEAD_DIM:(kh + 1) * HEAD_DIM] * scale
            v = kv[:, KV_DIM + kh * HEAD_DIM:KV_DIM + (kh + 1) * HEAD_DIM]
            h0 = kh * Q_PER_KV
            qg = jnp.concatenate([qt_ref[(h0 + r) * HEAD_DIM:(h0 + r + 1) * HEAD_DIM, qs] for r in range(Q_PER_KV)],
                                 axis=1)
            sink = sink_ref[kh]
            logits = jnp.dot(k, qg, preferred_element_type=F32) + bias_ref[has_prev, kh]
            m = jnp.maximum(jnp.max(logits, axis=0, keepdims=True), sink)
            p = jnp.exp(logits - m)
            denom = jnp.sum(p, axis=0, keepdims=True) + jnp.exp(sink - m)
            ot = lax.dot_general(v, p.astype(BF16), (((0,), (0,)), ((), ())), preferred_element_type=F32) / denom
            for r in range(0, Q_PER_KV, 2):
                pair = jnp.concatenate([ot[:, r * BLOCK:(r + 1) * BLOCK], ot[:, (r + 1) * BLOCK:(r + 2) * BLOCK]], axis=0)
                o_ref[qs, (h0 + r) * HEAD_DIM:(h0 + r + 2) * HEAD_DIM] = pair.T.astype(o_ref.dtype)


def swa_attention(qt, kv, sinks):
    seq = kv.shape[0]
    step_rows = ATTN_BLOCKS * BLOCK
    nb = seq // step_rows
    sink_rows = jnp.repeat(sinks.astype(F32), BLOCK).reshape(N_KV_HEADS, 1, Q_PER_KV * BLOCK)
    return pl.pallas_call(
        _attn_kernel, grid=(nb,),
        in_specs=[pl.BlockSpec((N_KV_HEADS, 1, Q_PER_KV * BLOCK), lambda i: (0, 0, 0)),
                  pl.BlockSpec((N_Q_HEADS * HEAD_DIM, step_rows), lambda i: (0, i)),
                  pl.BlockSpec((BLOCK, 2 * KV_DIM), lambda i: (jnp.maximum(ATTN_BLOCKS * i - 1, 0), 0)),
                  pl.BlockSpec((step_rows, 2 * KV_DIM), lambda i: (i, 0))],
        out_specs=pl.BlockSpec((step_rows, N_Q_HEADS * HEAD_DIM), lambda i: (i, 0)),
        out_shape=jax.ShapeDtypeStruct((seq, N_Q_HEADS * HEAD_DIM), BF16),
        scratch_shapes=[pltpu.VMEM((2, N_KV_HEADS, 2 * BLOCK, Q_PER_KV * BLOCK), F32)],
        compiler_params=_params("arbitrary"), name="swa_attention",
    )(sink_rows, qt, kv, kv)


def _pad_lanes(v, width=LANES):
    v = v.reshape(1, -1).astype(F32)
    return jnp.pad(v, ((0, 0), (0, width - v.shape[1])))


def kernel(x, norm_w, ssm_w_in, ssm_conv_w, ssm_conv_b, ssm_dt_bias, ssm_A_log, ssm_D, ssm_norm_w, ssm_w_out,
           kv_norm_w, w_kv, attn_w_q, attn_sinks, attn_w_o, ffn_w_gate, ffn_w_up, ffn_w_down):
    batch, seq, d = x.shape
    assert batch == 1 and d == D_MODEL and seq % SSD_ROWS == 0
    h = x.reshape(seq, d)
    row = lambda v: v.reshape(1, -1).astype(F32)

    g = norm_w[0]
    w_in_t = ssm_w_in[0].T.astype(BF16)
    n_main = 2 * D_INNER + 2 * GN
    w_dt_t = jnp.pad(w_in_t[n_main:], ((0, LANES - SSM_HEADS), (0, 0)))
    proj, dt_raw = norm_matmul2_nt(h, row(g[0]), w_in_t, w_dt_t, n=n_main, name="in_proj")

    def per_group(t):
        xs = t[:, :D_INNER].reshape(-1, SSM_GROUPS, GROUP_INNER)
        bs = t[:, D_INNER:D_INNER + GN].reshape(-1, SSM_GROUPS, SSM_STATE)
        cs = t[:, D_INNER + GN:].reshape(-1, SSM_GROUPS, SSM_STATE)
        return jnp.transpose(jnp.concatenate([xs, bs, cs], axis=-1), (1, 0, 2)).astype(F32)

    later_weights = (ffn_w_gate, ffn_w_up, ffn_w_down, ssm_w_out, attn_w_q, attn_w_o, w_kv[None])
    y, w_gate, w_up, w_down, w_out, w_q, w_o, w_kv16 = ssd_mixer(
        proj, dt_raw, per_group(ssm_conv_w[0]), per_group(ssm_conv_b[0].reshape(1, -1)),
        _pad_lanes(ssm_dt_bias[0]), _pad_lanes(ssm_A_log[0]),
        row(jnp.repeat(ssm_D[0], SSM_HEAD_DIM)), row(ssm_norm_w[0]), later_weights)
    h = matmul_norm_residual(y, w_out[0], h, row(g[1]), name="ssm_out_proj")
    h = ffn(h, row(g[2]), w_gate, w_up, w_down, row(g[3]), 0, name="ffn0")

    g = norm_w[1]
    qt, kv = qkv_proj(h, row(g[0]), row(kv_norm_w), w_q[0], w_kv16[0], name="qkv_proj")
    a = swa_attention(qt, kv, attn_sinks[0])
    h = matmul_norm_residual(a, w_o[0], h, row(g[1]), name="attn_out_proj")
    h = ffn(h, row(g[2]), w_gate, w_up, w_down, row(g[3]), 1, name="ffn1")
    return h.reshape(batch, seq, d)
```
